```python
import math
import jax
import jax.numpy as jnp
from jax import lax
import numpy as np

D_MODEL = 4096
BATCH = 4
SEQ = 2048
DEPTH = 2

CTX_LEN = 256
GRID_W = 64
N_MIXERS = 4
GROUP_W = D_MODEL // N_MIXERS
HEAD_DIM = 128
BLOCK = 128
WINDOW = 128
A_HEADS = GROUP_W // HEAD_DIM
A_KV_HEADS = max(1, A_HEADS // 4)
B_HEADS = GROUP_W // HEAD_DIM
B_QK_DIM = HEAD_DIM // 2
B_V_DIM = 2 * B_QK_DIM
C_HEADS = GROUP_W // HEAD_DIM
C_Q_LORA = 1536
C_KV_LORA = 512
C_NOPE = 128
C_ROPE = 64
C_V = 128
D_GROUPS = GROUP_W // HEAD_DIM
D_GROUP_DIM = GROUP_W // D_GROUPS
MIX_W = A_HEADS * HEAD_DIM + B_HEADS * B_V_DIM + C_HEADS * C_V + D_GROUPS * D_GROUP_DIM
Q_SPLITS = (A_HEADS * HEAD_DIM, B_HEADS * 2 * B_QK_DIM, C_Q_LORA, D_GROUPS * D_GROUP_DIM)
KV_SPLITS = (A_KV_HEADS * HEAD_DIM, A_KV_HEADS * HEAD_DIM, B_HEADS * 2 * B_QK_DIM,
             B_HEADS * B_V_DIM, C_KV_LORA, C_ROPE)
Q_COLS = sum(Q_SPLITS)
IN_W = Q_COLS + sum(KV_SPLITS)
D_FF = 2 * D_MODEL
N_MOD = 9
ALPHA = (2.0 * DEPTH) ** 0.25
BETA = (8.0 * DEPTH) ** -0.25
ROPE_BASE = 10000.0
LN_EPS = 1e-5
RMS_EPS = 1e-6
NEG_INF = -1e30

kernel_name = 'hybrid_parallel_heads_prefix_dit'


def layer_norm(x, g, b):
    xf = x.astype(jnp.float32)
    mu = jnp.mean(xf, axis=-1, keepdims=True)
    var = jnp.mean(jnp.square(xf - mu), axis=-1, keepdims=True)
    return ((xf - mu) * lax.rsqrt(var + LN_EPS) * g.astype(jnp.float32) + b.astype(jnp.float32)).astype(x.dtype)


def rms_norm(x, g):
    xf = x.astype(jnp.float32)
    return (xf * lax.rsqrt(jnp.mean(xf * xf, axis=-1, keepdims=True) + RMS_EPS) * g.astype(jnp.float32)).astype(x.dtype)


def modulate(x, shift, scale):
    return x * (1.0 + scale) + shift


def swiglu(u, w_gu, w_d):
    g, up = jnp.split(u @ w_gu, 2, axis=-1)
    return (jax.nn.silu(g) * up) @ w_d


def split_cols(h, sizes):
    idx = np.cumsum(np.array(sizes))[:-1].tolist()
    return jnp.split(h, idx, axis=-1)


def axial_angles(n_tok, rot_dim):
    rows = n_tok // GRID_W
    row = jnp.repeat(jnp.arange(rows, dtype=jnp.float32), GRID_W)
    col = jnp.tile(jnp.arange(GRID_W, dtype=jnp.float32), rows)
    axis_dim = rot_dim // 2
    inv = ROPE_BASE ** (-jnp.arange(0, axis_dim, 2, dtype=jnp.float32) / axis_dim)
    return row[:, None] * inv[None, :], col[:, None] * inv[None, :]


def rotate(x, ang):
    shape = (1, ang.shape[0]) + (1,) * (x.ndim - 3) + (ang.shape[1],)
    cos = jnp.cos(ang).reshape(shape)
    sin = jnp.sin(ang).reshape(shape)
    x1, x2 = jnp.split(x.astype(jnp.float32), 2, axis=-1)
    return jnp.concatenate([x1 * cos - x2 * sin, x2 * cos + x1 * sin], axis=-1).astype(x.dtype)


def rope2d(x, ang):
    ang_row, ang_col = ang
    d = x.shape[-1]
    return jnp.concatenate([rotate(x[..., : d // 2], ang_row), rotate(x[..., d // 2:], ang_col)], axis=-1)


def softmax_with_sink(logits, sink):
    s = jnp.broadcast_to(sink.astype(jnp.float32), logits.shape[:-1] + (1,))
    return jax.nn.softmax(jnp.concatenate([s, logits], axis=-1), axis=-1)[..., 1:]


def sweep_blocks(fn, qs):
    b_, t_ = qs[0].shape[:2]
    nb = t_ // BLOCK
    blocks = tuple(q.reshape((b_, nb, BLOCK) + q.shape[2:]).swapaxes(0, 1) for q in qs)
    out = lax.map(lambda qb: fn(*qb), blocks)
    return out.swapaxes(0, 1).reshape((b_, t_) + out.shape[3:])


def gqa_heads(h, n_heads, ang):
    b_, l_, _ = h.shape
    t = h.reshape(b_, l_, n_heads, HEAD_DIM)
    return t if ang is None else rope2d(t, ang)


def window_gqa_latent(q, k, v, kc, vc, sink):
    b_, t_ = q.shape[:2]
    nb = t_ // BLOCK
    grp = A_HEADS // A_KV_HEADS
    n_ctx = kc.shape[1]
    scale = HEAD_DIM ** -0.5
    qb = q.reshape(b_, nb, BLOCK, A_KV_HEADS, grp, HEAD_DIM)
    pad = ((0, 0), (BLOCK, BLOCK), (0, 0), (0, 0))
    kp = jnp.pad(k, pad).reshape(b_, nb + 2, BLOCK, A_KV_HEADS, HEAD_DIM)
    vp = jnp.pad(v, pad).reshape(b_, nb + 2, BLOCK, A_KV_HEADS, HEAD_DIM)
    kband = jnp.concatenate([kp[:, :-2], kp[:, 1:-1], kp[:, 2:]], axis=2)
    vband = jnp.concatenate([vp[:, :-2], vp[:, 1:-1], vp[:, 2:]], axis=2)
    s_loc = jnp.einsum('bnqhgd,bnkhd->bnhgqk', qb, kband).astype(jnp.float32) * scale
    qpos = jnp.arange(nb)[:, None] * BLOCK + jnp.arange(BLOCK)[None, :]
    kpos = (jnp.arange(nb)[:, None] - 1) * BLOCK + jnp.arange(3 * BLOCK)[None, :]
    valid = ((kpos[:, None, :] >= 0) & (kpos[:, None, :] < t_)
             & (jnp.abs(qpos[:, :, None] - kpos[:, None, :]) <= WINDOW))
    s_loc = jnp.where(valid[None, :, None, None], s_loc, NEG_INF)
    s_ctx = jnp.einsum('bnqhgd,bchd->bnhgqc', qb, kc).astype(jnp.float32) * scale
    p = softmax_with_sink(jnp.concatenate([s_ctx, s_loc], axis=-1),
                          sink.reshape(A_KV_HEADS, grp)[:, :, None, None]).astype(v.dtype)
    o = (jnp.einsum('bnhgqc,bchd->bnqhgd', p[..., :n_ctx], vc)
         + jnp.einsum('bnhgqk,bnkhd->bnqhgd', p[..., n_ctx:], vband))
    return o.reshape(b_, t_, A_HEADS * HEAD_DIM)


def gqa_context(q, k, v, sink):
    b_, l_ = q.shape[:2]
    grp = A_HEADS // A_KV_HEADS
    qg = q.reshape(b_, l_, A_KV_HEADS, grp, HEAD_DIM)
    s = jnp.einsum('bqhgd,bkhd->bhgqk', qg, k).astype(jnp.float32) * HEAD_DIM ** -0.5
    p = softmax_with_sink(s, sink.reshape(A_KV_HEADS, grp)[:, :, None, None]).astype(v.dtype)
    o = jnp.einsum('bhgqk,bkhd->bqhgd', p, v)
    return o.reshape(b_, l_, A_HEADS * HEAD_DIM)


def diff_qk(h, ang):
    b_, l_, _ = h.shape
    t = h.reshape(b_, l_, B_HEADS, 2, B_QK_DIM)
    return t if ang is None else rope2d(t, ang)


def diff_lambda(b_lambda, lam_init):
    lf = b_lambda.astype(jnp.float32)
    return jnp.exp(jnp.sum(lf[0] * lf[1])) - jnp.exp(jnp.sum(lf[2] * lf[3])) + lam_init


def diff_attend(q, k, v, lam):
    s = jnp.einsum('bqhmd,bkhmd->bhmqk', q, k).astype(jnp.float32) * B_QK_DIM ** -0.5
    p = jax.nn.softmax(s, axis=-1)
    w = (p[:, :, 0] - lam * p[:, :, 1]).astype(v.dtype)
    return jnp.einsum('bhqk,bkhd->bqhd', w, v)


def diff_out(o, g, lam_init):
    b_, l_ = o.shape[:2]
    return (rms_norm(o, g) * (1.0 - lam_init)).reshape(b_, l_, B_HEADS * B_V_DIM)


def mla_q(h, g, w_uq, ang):
    b_, l_, _ = h.shape
    q = (rms_norm(h, g) @ w_uq).reshape(b_, l_, C_HEADS, C_NOPE + C_ROPE)
    qn, qr = q[..., :C_NOPE], q[..., C_NOPE:]
    return qn, (qr if ang is None else rope2d(qr, ang))


def mla_kv(h_kv, h_kr, g, w_ukv, ang):
    b_, l_, _ = h_kv.shape
    kv = (rms_norm(h_kv, g) @ w_ukv).reshape(b_, l_, C_HEADS, C_NOPE + C_V)
    kr = h_kr if ang is None else rope2d(h_kr, ang)
    return kv[..., :C_NOPE], kr, kv[..., C_NOPE:]


def mla_attend(qn, qr, kn, kr, v):
    s = (jnp.einsum('bqhd,bkhd->bhqk', qn, kn)
         + jnp.einsum('bqhd,bkd->bhqk', qr, kr)).astype(jnp.float32) * (C_NOPE + C_ROPE) ** -0.5
    p = jax.nn.softmax(s, axis=-1).astype(v.dtype)
    return jnp.einsum('bhqk,bkhd->bqhd', p, v)


def fourier_mix(h, d_w):
    b_, l_, _ = h.shape
    z = h.reshape(b_, l_, D_GROUPS, D_GROUP_DIM).astype(jnp.float32)
    f = jnp.fft.fftn(z, axes=(1, 3), norm='ortho').real.astype(h.dtype)
    return jnp.einsum('blgc,gce->blge', f, d_w).reshape(b_, l_, D_GROUPS * D_GROUP_DIM)


def mixing(u, uc, w_in, w_out, a_sink, b_lambda, b_subln_g, c_q_norm_g, c_kv_norm_g,
           c_w_uq, c_w_ukv, d_w, lam_init, angs, need_ctx):
    ang_a, ang_b, ang_c = angs
    b_, t_, _ = u.shape
    h = u @ w_in
    aq, bq, cq, dx = split_cols(h[..., :Q_COLS], Q_SPLITS)
    ak, av, bk, bv, ckv, ckr = split_cols(h[..., Q_COLS:], KV_SPLITS)
    hc_kv = uc @ w_in[:, Q_COLS:]
    cak, cav, cbk, cbv, cckv, cckr = split_cols(hc_kv, KV_SPLITS)
    ka_c, va_c = gqa_heads(cak, A_KV_HEADS, None), gqa_heads(cav, A_KV_HEADS, None)
    kb_c = diff_qk(cbk, None)
    vb_c = cbv.reshape(b_, -1, B_HEADS, B_V_DIM)
    kn_c, kr_c, vc_c = mla_kv(cckv, cckr, c_kv_norm_g, c_w_ukv, None)
    lam = diff_lambda(b_lambda, lam_init)
    qa = gqa_heads(aq, A_HEADS, ang_a)
    ka, va = gqa_heads(ak, A_KV_HEADS, ang_a), gqa_heads(av, A_KV_HEADS, None)
    ya = window_gqa_latent(qa, ka, va, ka_c, va_c, a_sink)
    qb = diff_qk(bq, ang_b)
    kb_all = jnp.concatenate([kb_c, diff_qk(bk, ang_b)], axis=1)
    vb_all = jnp.concatenate([vb_c, bv.reshape(b_, t_, B_HEADS, B_V_DIM)], axis=1)
    yb = diff_out(sweep_blocks(lambda q_: diff_attend(q_, kb_all, vb_all, lam), (qb,)), b_subln_g, lam_init)
    qn, qr = mla_q(cq, c_q_norm_g, c_w_uq, ang_c)
    kn, kr, vc = mla_kv(ckv, ckr, c_kv_norm_g, c_w_ukv, ang_c)
    kn_all = jnp.concatenate([kn_c, kn], axis=1)
    kr_all = jnp.concatenate([kr_c, kr], axis=1)
    vc_all = jnp.concatenate([vc_c, vc], axis=1)
    yc = sweep_blocks(lambda qn_, qr_: mla_attend(qn_, qr_, kn_all, kr_all, vc_all), (qn, qr))
    yc = yc.reshape(b_, t_, C_HEADS * C_V)
    yd = fourier_mix(dx, d_w)
    y = jnp.concatenate([ya, yb, yc, yd], axis=-1) @ w_out
    if not need_ctx:
        return y, None
    l_c = uc.shape[1]
    caq, cbq, ccq, cdx = split_cols(uc @ w_in[:, :Q_COLS], Q_SPLITS)
    ya_c = gqa_context(gqa_heads(caq, A_HEADS, None), ka_c, va_c, a_sink)
    yb_c = diff_out(diff_attend(diff_qk(cbq, None), kb_c, vb_c, lam), b_subln_g, lam_init)
    qn_c, qr_c = mla_q(ccq, c_q_norm_g, c_w_uq, None)
    yc_c = mla_attend(qn_c, qr_c, kn_c, kr_c, vc_c).reshape(b_, l_c, C_HEADS * C_V)
    yd_c = fourier_mix(cdx, d_w)
    y_c = jnp.concatenate([ya_c, yb_c, yc_c, yd_c], axis=-1) @ w_out
    return y, y_c


def setup_inputs(seed: int = 0) -> dict:
    key = jax.random.key(seed)
    ks = jax.random.split(key, 24)
    L = DEPTH

    def nrm(k, shape, scale):
        return jax.random.normal(k, shape, jnp.float32) * scale

    return {
        'x': nrm(ks[0], (BATCH, SEQ, D_MODEL), 1.0),
        'c': nrm(ks[1], (BATCH, D_MODEL), 1.0),
        'ctx': nrm(ks[2], (BATCH, CTX_LEN, D_MODEL), 1.0),
        'c_ctx': nrm(ks[3], (D_MODEL,), 1.0),
        'w_mod': nrm(ks[4], (L, D_MODEL, N_MOD * D_MODEL), 0.5 * D_MODEL ** -0.5),
        'b_mod': nrm(ks[5], (L, N_MOD * D_MODEL), 0.01),
        'ffn1_w_gu': nrm(ks[6], (L, D_MODEL, 2 * D_FF), D_MODEL ** -0.5),
        'ffn1_w_d': nrm(ks[7], (L, D_FF, D_MODEL), BETA * D_FF ** -0.5),
        'ffn2_w_gu': nrm(ks[8], (L, D_MODEL, 2 * D_FF), D_MODEL ** -0.5),
        'ffn2_w_d': nrm(ks[9], (L, D_FF, D_MODEL), BETA * D_FF ** -0.5),
        'ln_g': 1.0 + nrm(ks[10], (L, 3, D_MODEL), 0.02),
        'ln_b': nrm(ks[11], (L, 3, D_MODEL), 0.02),
        'w_in': nrm(ks[12], (L, D_MODEL, IN_W), D_MODEL ** -0.5),
        'w_out': nrm(ks[13], (L, MIX_W, D_MODEL), BETA * MIX_W ** -0.5),
        'a_sink': nrm(ks[14], (L, A_HEADS), 0.5),
        'b_lambda': nrm(ks[15], (L, 4, B_QK_DIM), 0.1),
        'b_subln_g': 1.0 + nrm(ks[16], (L, B_V_DIM), 0.02),
        'c_q_norm_g': 1.0 + nrm(ks[17], (L, C_Q_LORA), 0.02),
        'c_kv_norm_g': 1.0 + nrm(ks[18], (L, C_KV_LORA), 0.02),
        'c_w_uq': nrm(ks[19], (L, C_Q_LORA, C_HEADS * (C_NOPE + C_ROPE)), C_Q_LORA ** -0.5),
        'c_w_ukv': nrm(ks[20], (L, C_KV_LORA, C_HEADS * (C_NOPE + C_V)), C_KV_LORA ** -0.5),
        'd_w': nrm(ks[21], (L, D_GROUPS, D_GROUP_DIM, D_GROUP_DIM), D_GROUP_DIM ** -0.5),
    }


def reference(x, c, ctx, c_ctx, w_mod, b_mod, ffn1_w_gu, ffn1_w_d, ffn2_w_gu, ffn2_w_d, ln_g, ln_b,
              w_in, w_out, a_sink, b_lambda, b_subln_g, c_q_norm_g, c_kv_norm_g, c_w_uq, c_w_ukv, d_w):
    t_ = x.shape[1]
    angs = (axial_angles(t_, HEAD_DIM), axial_angles(t_, B_QK_DIM), axial_angles(t_, C_ROPE))
    sc = jax.nn.silu(jnp.concatenate([c, c_ctx[None, :]], axis=0))
    xc = ctx
    for l in range(DEPTH):
        last = l == DEPTH - 1
        lam_init = 0.8 - 0.6 * math.exp(-0.3 * l)
        mod = (sc @ w_mod[l] + b_mod[l]).reshape(-1, N_MOD, D_MODEL)
        m = mod[:-1, :, None, :]
        mc = mod[-1]
        x = layer_norm(ALPHA * x + 0.5 * m[:, 2] * swiglu(modulate(x, m[:, 0], m[:, 1]), ffn1_w_gu[l], ffn1_w_d[l]),
                       ln_g[l, 0], ln_b[l, 0])
        xc = layer_norm(ALPHA * xc + 0.5 * mc[2] * swiglu(modulate(xc, mc[0], mc[1]), ffn1_w_gu[l], ffn1_w_d[l]),
                        ln_g[l, 0], ln_b[l, 0])
        y, y_c = mixing(modulate(x, m[:, 3], m[:, 4]), modulate(xc, mc[3], mc[4]), w_in[l], w_out[l],
                        a_sink[l], b_lambda[l], b_subln_g[l], c_q_norm_g[l], c_kv_norm_g[l],
                        c_w_uq[l], c_w_ukv[l], d_w[l], lam_init, angs, not last)
        x = layer_norm(ALPHA * x + m[:, 5] * y, ln_g[l, 1], ln_b[l, 1])
        x = layer_norm(ALPHA * x + 0.5 * m[:, 8] * swiglu(modulate(x, m[:, 6], m[:, 7]), ffn2_w_gu[l], ffn2_w_d[l]),
                       ln_g[l, 2], ln_b[l, 2])
        if not last:
            xc = layer_norm(ALPHA * xc + mc[5] * y_c, ln_g[l, 1], ln_b[l, 1])
            xc = layer_norm(ALPHA * xc + 0.5 * mc[8] * swiglu(modulate(xc, mc[6], mc[7]), ffn2_w_gu[l], ffn2_w_d[l]),
                            ln_g[l, 2], ln_b[l, 2])
    return x
```

```python
import functools
import math

import numpy as np
import jax
import jax.numpy as jnp
from jax import lax
from jax.experimental import pallas as pl
from jax.experimental.pallas import tpu as pltpu

F32 = jnp.float32
BF16 = jnp.bfloat16

GRID_W = 64
HEAD_DIM = 128
WINDOW = 128
N_HEADS = 8
A_KV_HEADS = 2
A_GROUP = N_HEADS // A_KV_HEADS
B_QK_DIM = 64
C_Q_LORA = 1536
C_KV_LORA = 512
C_NOPE = 128
C_ROPE = 64
N_MOD = 9
ROPE_BASE = 10000.0
LN_EPS = 1e-5
RMS_EPS = 1e-6
NEG_INF = -1e30

H_CQ, H_AQ, H_BQ, H_DX = 0, 1536, 2560, 3584
H_AK, H_AV, H_BK, H_BV, H_CKV, H_CKR = 4608, 4864, 5120, 6144, 7168, 7680
H_COLS = 7936

ROPE_NONE, ROPE_A, ROPE_B, ROPE_KR, ROPE_QR = 0, 1, 2, 3, 4

VMEM_LIMIT_BYTES = 56 * 1024 * 1024
ROPE_TN = 256


def _cparams(n_axes):
    return pltpu.CompilerParams(dimension_semantics=("arbitrary",) * n_axes,
                                vmem_limit_bytes=VMEM_LIMIT_BYTES)


def _dot(a, b):
    return jnp.dot(a, b, preferred_element_type=F32)


def _dot_nt(a, b):
    return lax.dot_general(a, b, (((1,), (1,)), ((), ())), preferred_element_type=F32)


def _mm_kernel(a_ref, b_ref, o_ref, *scratch, nk):
    part = _dot(a_ref[...], b_ref[...].astype(BF16))
    if nk == 1:
        o_ref[...] = part.astype(o_ref.dtype)
        return
    acc_ref, = scratch
    k = pl.program_id(2)

    @pl.when(k == 0)
    def _():
        acc_ref[...] = part

    @pl.when(k > 0)
    def _():
        acc_ref[...] += part

    @pl.when(k == nk - 1)
    def _():
        o_ref[...] = acc_ref[...].astype(o_ref.dtype)


def _mm(a, b, *, rows, tm, tn, tk, out_dtype, a_col0=0, name):
    kdim, n = b.shape
    assert rows % tm == 0 and n % tn == 0 and kdim % tk == 0 and a_col0 % tk == 0
    nk = kdim // tk
    koff = a_col0 // tk
    scratch = [pltpu.VMEM((tm, tn), F32)] if nk > 1 else []
    return pl.pallas_call(
        functools.partial(_mm_kernel, nk=nk),
        out_shape=jax.ShapeDtypeStruct((rows, n), out_dtype),
        grid=(rows // tm, n // tn, nk),
        in_specs=[pl.BlockSpec((tm, tk), lambda i, j, k: (i, koff + k)),
                  pl.BlockSpec((tk, tn), lambda i, j, k: (k, j))],
        out_specs=pl.BlockSpec((tm, tn), lambda i, j, k: (i, j)),
        scratch_shapes=scratch,
        compiler_params=_cparams(3),
        name=name,
    )(a, b)


def _mod_kernel(c_ref, w_ref, b_ref, o_ref):
    c = c_ref[...]
    sc = (c * jax.nn.sigmoid(c)).astype(BF16)
    o_ref[...] = _dot(sc, w_ref[...].astype(BF16)) + b_ref[...]


def _mod_vectors(c_rows, w_mod, b_mod):
    d, n = w_mod.shape
    tn = 1024
    return pl.pallas_call(
        _mod_kernel,
        out_shape=jax.ShapeDtypeStruct((c_rows.shape[0], n), F32),
        grid=(n // tn,),
        in_specs=[pl.BlockSpec((c_rows.shape[0], d), lambda j: (0, 0)),
                  pl.BlockSpec((d, tn), lambda j: (0, j)),
                  pl.BlockSpec((1, tn), lambda j: (0, j))],
        out_specs=pl.BlockSpec((c_rows.shape[0], tn), lambda j: (0, j)),
        compiler_params=_cparams(1),
        name="mod_vectors",
    )(c_rows, w_mod, b_mod.reshape(1, n))


def _seg_index(i, n_lat_tiles, tiles_per_batch, n_batch):
    return jnp.where(i < n_lat_tiles, i // tiles_per_batch, n_batch)


def _modulate_kernel(x_ref, mod_ref, u_ref, *, shift_idx):
    shift = mod_ref[shift_idx:shift_idx + 1, :]
    scale = mod_ref[shift_idx + 1:shift_idx + 2, :]
    u_ref[...] = (x_ref[...] * (1.0 + scale) + shift).astype(u_ref.dtype)


def _ln_res_kernel(x_ref, y_ref, mod_ref, g_ref, b_ref, *rest, alpha, gate_idx, gate_mul, next_shift_idx):
    gate = mod_ref[gate_idx:gate_idx + 1, :]
    z = alpha * x_ref[...] + (gate_mul * gate) * y_ref[...].astype(F32)
    mu = jnp.mean(z, axis=-1, keepdims=True)
    zc = z - mu
    var = jnp.mean(zc * zc, axis=-1, keepdims=True)
    xn = zc * lax.rsqrt(var + LN_EPS) * g_ref[...] + b_ref[...]
    if next_shift_idx is None:
        xo_ref, = rest
        xo_ref[...] = xn
    else:
        nmod_ref, xo_ref, u_ref = rest
        xo_ref[...] = xn
        shift = nmod_ref[next_shift_idx:next_shift_idx + 1, :]
        scale = nmod_ref[next_shift_idx + 1:next_shift_idx + 2, :]
        u_ref[...] = (xn * (1.0 + scale) + shift).astype(u_ref.dtype)


class _Rows:
    def __init__(self, n_batch, seq, ctx_len):
        self.b, self.t, self.c = n_batch, seq, ctx_len
        self.nl, self.nc = n_batch * seq, n_batch * ctx_len
        self.n = self.nl + self.nc
        self.tr = math.gcd(256, math.gcd(seq, ctx_len))
        self.tm = math.gcd(1024, math.gcd(seq, self.nc))

    def mod_spec(self, tile, d):
        n_lat_tiles, per_batch, nb = self.nl // tile, self.t // tile, self.b
        return pl.BlockSpec((None, N_MOD, d),
                            lambda i, *_: (_seg_index(i, n_lat_tiles, per_batch, nb), 0, 0))


def _modulate(x, mod, rg, *, shift_idx):
    n, d = x.shape
    tr = rg.tr
    return pl.pallas_call(
        functools.partial(_modulate_kernel, shift_idx=shift_idx),
        out_shape=jax.ShapeDtypeStruct((n, d), BF16),
        grid=(n // tr,),
        in_specs=[pl.BlockSpec((tr, d), lambda i: (i, 0)), rg.mod_spec(tr, d)],
        out_specs=pl.BlockSpec((tr, d), lambda i: (i, 0)),
        compiler_params=_cparams(1),
        name="modulate",
    )(x, mod)


def _ln_res(x, y, mod, ln_g, ln_b, rg, *, rows, alpha, gate_idx, gate_mul, next_mod, next_shift_idx, name):
    d = x.shape[1]
    tr = rg.tr
    row_spec = pl.BlockSpec((tr, d), lambda i: (i, 0))
    vec_spec = pl.BlockSpec((1, d), lambda i: (0, 0))
    out_shape = [jax.ShapeDtypeStruct((rows, d), F32)]
    out_specs = [row_spec]
    in_specs = [row_spec, row_spec, rg.mod_spec(tr, d), vec_spec, vec_spec]
    args = [x, y, mod, ln_g.reshape(1, d), ln_b.reshape(1, d)]
    if next_shift_idx is not None:
        in_specs.append(rg.mod_spec(tr, d))
        args.append(next_mod)
        out_shape.append(jax.ShapeDtypeStruct((rows, d), BF16))
        out_specs.append(row_spec)
    res = pl.pallas_call(
        functools.partial(_ln_res_kernel, alpha=alpha, gate_idx=gate_idx, gate_mul=gate_mul,
                          next_shift_idx=next_shift_idx),
        out_shape=out_shape,
        grid=(rows // tr,),
        in_specs=in_specs,
        out_specs=out_specs,
        compiler_params=_cparams(1),
        name=name,
    )(*args)
    return (res[0], res[1]) if next_shift_idx is not None else (res[0], None)


def _ffn_up_kernel(a_ref, wg_ref, wu_ref, o_ref):
    a = a_ref[...]
    g = _dot(a, wg_ref[...])
    up = _dot(a, wu_ref[...])
    o_ref[...] = (g * jax.nn.sigmoid(g) * up).astype(o_ref.dtype)


def _ffn_up(u, w_gu, *, rows, tm, tn):
    d, two_ff = w_gu.shape
    ff = two_ff // 2
    up0 = ff // tn
    return pl.pallas_call(
        _ffn_up_kernel,
        out_shape=jax.ShapeDtypeStruct((rows, ff), BF16),
        grid=(rows // tm, ff // tn),
        in_specs=[pl.BlockSpec((tm, d), lambda i, j: (i, 0)),
                  pl.BlockSpec((d, tn), lambda i, j: (0, j)),
                  pl.BlockSpec((d, tn), lambda i, j: (0, up0 + j))],
        out_specs=pl.BlockSpec((tm, tn), lambda i, j: (i, j)),
        compiler_params=_cparams(2),
        name="ffn_up",
    )(u, w_gu, w_gu)


def _rope_tile(x, cos, sin, half):
    lane = lax.broadcasted_iota(jnp.int32, x.shape, 1)
    first = (lane % (2 * half)) < half
    width = x.shape[1]
    partner = jnp.where(first, pltpu.roll(x, width - half, 1), pltpu.roll(x, half, 1))
    return x * cos + partner * sin


def _proj_kernel(kind_ref, a_ref, b_ref, *rest, has_gain, n_lat_tiles):
    if has_gain:
        gain_ref, cos_ref, sin_ref, o_ref, acc_ref, an_ref = rest
    else:
        cos_ref, sin_ref, o_ref, acc_ref = rest
    i, j = pl.program_id(0), pl.program_id(1)

    if has_gain:
        @pl.when(j == 0)
        def _():
            x = a_ref[...].astype(F32)
            ms = jnp.mean(x * x, axis=-1, keepdims=True)
            an_ref[...] = (x * lax.rsqrt(ms + RMS_EPS) * gain_ref[...]).astype(BF16)
        a = an_ref[...]
    else:
        a = a_ref[...]
    acc_ref[...] = _dot(a, b_ref[...])
    kind = jnp.where(i < n_lat_tiles, kind_ref[j], ROPE_NONE)

    @pl.when(kind == ROPE_NONE)
    def _():
        o_ref[...] = acc_ref[...].astype(o_ref.dtype)

    @pl.when(kind == ROPE_A)
    def _():
        o_ref[...] = _rope_tile(acc_ref[...], cos_ref[...], sin_ref[...], 32).astype(o_ref.dtype)

    @pl.when(kind > ROPE_A)
    def _():
        o_ref[...] = _rope_tile(acc_ref[...], cos_ref[...], sin_ref[...], 16).astype(o_ref.dtype)


def _proj(a, w, kinds, cos_tab, sin_tab, rg, *, a_col0=0, gain=None, name):
    kdim, n = w.shape
    tm, tn = rg.tm, ROPE_TN
    assert a_col0 % kdim == 0 and n % tn == 0
    a_blk = a_col0 // kdim
    n_lat_tiles = rg.nl // tm
    t_tiles = rg.t // tm
    has_gain = gain is not None

    def tab_map(i, j, kind_ref):
        return (jnp.where(i < n_lat_tiles, kind_ref[j], ROPE_NONE), i % t_tiles, 0)

    in_specs = [pl.BlockSpec((tm, kdim), lambda i, j, kr: (i, a_blk)),
                pl.BlockSpec((kdim, tn), lambda i, j, kr: (0, j))]
    args = [a, w]
    scratch = [pltpu.VMEM((tm, tn), F32)]
    if has_gain:
        in_specs.append(pl.BlockSpec((1, kdim), lambda i, j, kr: (0, 0)))
        args.append(gain.reshape(1, kdim))
        scratch.append(pltpu.VMEM((tm, kdim), BF16))
    in_specs += [pl.BlockSpec((None, tm, tn), tab_map), pl.BlockSpec((None, tm, tn), tab_map)]
    args += [cos_tab, sin_tab]
    return pl.pallas_call(
        functools.partial(_proj_kernel, has_gain=has_gain, n_lat_tiles=n_lat_tiles),
        out_shape=jax.ShapeDtypeStruct((rg.n, n), BF16),
        grid_spec=pltpu.PrefetchScalarGridSpec(
            num_scalar_prefetch=1,
            grid=(rg.n // tm, n // tn),
            in_specs=in_specs,
            out_specs=pl.BlockSpec((tm, tn), lambda i, j, kr: (i, j)),
            scratch_shapes=scratch),
        compiler_params=_cparams(2),
        name=name,
    )(kinds, *args)


def _rope_tables(seq):
    t = np.arange(seq)
    row, col = (t // GRID_W).astype(np.float64), (t % GRID_W).astype(np.float64)

    def pattern(rot_dim):
        axis_dim = rot_dim // 2
        inv = ROPE_BASE ** (-np.arange(0, axis_dim, 2, dtype=np.float64) / axis_dim)
        ar, ac = row[:, None] * inv[None, :], col[:, None] * inv[None, :]
        cos = np.concatenate([np.cos(ar), np.cos(ar), np.cos(ac), np.cos(ac)], axis=1)
        sin = np.concatenate([-np.sin(ar), np.sin(ar), -np.sin(ac), np.sin(ac)], axis=1)
        return cos, sin

    cos = np.ones((5, seq, ROPE_TN), np.float64)
    sin = np.zeros((5, seq, ROPE_TN), np.float64)
    c128, s128 = pattern(HEAD_DIM)
    c64, s64 = pattern(2 * 32)
    cos[ROPE_A], sin[ROPE_A] = np.tile(c128, (1, 2)), np.tile(s128, (1, 2))
    cos[ROPE_B], sin[ROPE_B] = np.tile(c64, (1, 4)), np.tile(s64, (1, 4))
    cos[ROPE_KR, :, :64], sin[ROPE_KR, :, :64] = c64, s64
    cos[ROPE_QR, :, 128:192], sin[ROPE_QR, :, 128:192] = c64, s64
    return jnp.asarray(cos, F32), jnp.asarray(sin, F32)


def _attn_win_kernel(sink_ref, q_ref, kc_ref, vc_ref, *rest, has_lat, tq, seq):
    if has_lat:
        kl_ref, vl_ref, o_ref = rest
    else:
        o_ref, = rest
    g, qi = pl.program_id(1), pl.program_id(2)
    scale = HEAD_DIM ** -0.5
    kc, vc = kc_ref[...], vc_ref[...]
    if has_lat:
        win = min(seq, tq + 2 * WINDOW)
        ws = pl.multiple_of(jnp.clip(qi * tq - WINDOW, 0, seq - win), WINDOW)
        kw, vw = kl_ref[pl.ds(ws, win), :], vl_ref[pl.ds(ws, win), :]
        qpos = qi * tq + lax.broadcasted_iota(jnp.int32, (tq, win), 0)
        kpos = ws + lax.broadcasted_iota(jnp.int32, (tq, win), 1)
        valid = jnp.abs(qpos - kpos) <= WINDOW
    for j in range(A_GROUP):
        q = q_ref[:, j * HEAD_DIM:(j + 1) * HEAD_DIM]
        sink = sink_ref[g * A_GROUP + j]
        s_c = _dot_nt(q, kc) * scale
        m = jnp.maximum(jnp.max(s_c, axis=-1, keepdims=True), sink)
        if has_lat:
            s_l = jnp.where(valid, _dot_nt(q, kw) * scale, NEG_INF)
            m = jnp.maximum(m, jnp.max(s_l, axis=-1, keepdims=True))
        p_c = jnp.exp(s_c - m)
        den = jnp.sum(p_c, axis=-1, keepdims=True) + jnp.exp(sink - m)
        o = _dot(p_c.astype(BF16), vc)
        if has_lat:
            p_l = jnp.exp(s_l - m)
            den = den + jnp.sum(p_l, axis=-1, keepdims=True)
            o = o + _dot(p_l.astype(BF16), vw)
        o_ref[:, j * HEAD_DIM:(j + 1) * HEAD_DIM] = (o / den).astype(o_ref.dtype)


def _attn_win(h, a_sink, rg, *, latent):
    gw = A_GROUP * HEAD_DIM
    if latent:
        tq, n_q, rows, q0 = min(256, rg.t), rg.t // min(256, rg.t), rg.nl, 0
    else:
        tq, n_q, rows, q0 = rg.c, 1, rg.nc, rg.nl // rg.c
    c_blk0 = rg.nl // rg.c
    in_specs = [pl.BlockSpec(memory_space=pltpu.SMEM),
                pl.BlockSpec((tq, gw), lambda b, g, qi: (q0 + b * n_q + qi, H_AQ // gw + g)),
                pl.BlockSpec((rg.c, HEAD_DIM), lambda b, g, qi: (c_blk0 + b, H_AK // HEAD_DIM + g)),
                pl.BlockSpec((rg.c, HEAD_DIM), lambda b, g, qi: (c_blk0 + b, H_AV // HEAD_DIM + g))]
    args = [a_sink, h, h, h]
    if latent:
        in_specs += [pl.BlockSpec((rg.t, HEAD_DIM), lambda b, g, qi: (b, H_AK // HEAD_DIM + g)),
                     pl.BlockSpec((rg.t, HEAD_DIM), lambda b, g, qi: (b, H_AV // HEAD_DIM + g))]
        args += [h, h]
    return pl.pallas_call(
        functools.partial(_attn_win_kernel, has_lat=latent, tq=tq, seq=rg.t),
        out_shape=jax.ShapeDtypeStruct((rows, N_HEADS * HEAD_DIM), BF16),
        grid=(rg.b, A_KV_HEADS, n_q),
        in_specs=in_specs,
        out_specs=pl.BlockSpec((tq, gw), lambda b, g, qi: (b * n_q + qi, g)),
        compiler_params=_cparams(3),
        name="attn_win_lat" if latent else "attn_win_ctx",
    )(*args)


def _softmax_pv(score_fns, values, scale):
    scores = [f() * scale for f in score_fns]
    m = functools.reduce(jnp.maximum, [jnp.max(s, axis=-1, keepdims=True) for s in scores])
    den, out = None, None
    for s, v in zip(scores, values):
        p = jnp.exp(s - m)
        d = jnp.sum(p, axis=-1, keepdims=True)
        o = _dot(p.astype(BF16), v)
        den = d if den is None else den + d
        out = o if out is None else out + o
    return out / den


def _attn_diff_kernel(lam_ref, gsub_ref, q_ref, kc_ref, vc_ref, *rest, has_lat, lam_init):
    if has_lat:
        kl_ref, vl_ref, o_ref = rest
    else:
        o_ref, = rest
    lf = lam_ref[...]
    lam = (jnp.exp(jnp.sum(lf[0:1] * lf[1:2], axis=-1, keepdims=True))
           - jnp.exp(jnp.sum(lf[2:3] * lf[3:4], axis=-1, keepdims=True)) + lam_init)
    q = q_ref[...]
    first_map = lax.broadcasted_iota(jnp.int32, q.shape, 1) < B_QK_DIM
    zero = jnp.zeros_like(q)
    keys = [kc_ref[...]] + ([kl_ref[...]] if has_lat else [])
    values = [vc_ref[...]] + ([vl_ref[...]] if has_lat else [])

    def one_map(qm):
        return _softmax_pv([functools.partial(_dot_nt, qm, k) for k in keys], values, B_QK_DIM ** -0.5)

    o = one_map(jnp.where(first_map, q, zero)) - lam * one_map(jnp.where(first_map, zero, q))
    ms = jnp.mean(o * o, axis=-1, keepdims=True)
    o = o * lax.rsqrt(ms + RMS_EPS) * gsub_ref[...] * (1.0 - lam_init)
    o_ref[...] = o.astype(o_ref.dtype)


def _kv_specs(rg, col_k, col_v, latent, k_width=HEAD_DIM):
    c_blk0 = rg.nl // rg.c
    specs = [pl.BlockSpec((rg.c, k_width), lambda b, hh, qi: (c_blk0 + b, col_k(hh))),
             pl.BlockSpec((rg.c, HEAD_DIM), lambda b, hh, qi: (c_blk0 + b, col_v(hh)))]
    if latent:
        specs += [pl.BlockSpec((rg.t, k_width), lambda b, hh, qi: (b, col_k(hh))),
                  pl.BlockSpec((rg.t, HEAD_DIM), lambda b, hh, qi: (b, col_v(hh)))]
    return specs


def _q_geometry(rg, latent, tq_lat):
    if latent:
        tq = min(tq_lat, rg.t)
        return tq, rg.t // tq, rg.nl, 0
    return rg.c, 1, rg.nc, rg.nl // rg.c


def _attn_diff(h, b_lambda, b_subln_g, rg, *, latent, lam_init):
    tq, n_q, rows, q0 = _q_geometry(rg, latent, 512)
    in_specs = [pl.BlockSpec((4, B_QK_DIM), lambda b, hh, qi: (0, 0)),
                pl.BlockSpec((1, HEAD_DIM), lambda b, hh, qi: (0, 0)),
                pl.BlockSpec((tq, HEAD_DIM), lambda b, hh, qi: (q0 + b * n_q + qi, H_BQ // HEAD_DIM + hh))]
    in_specs += _kv_specs(rg, lambda hh: H_BK // HEAD_DIM + hh, lambda hh: H_BV // HEAD_DIM + hh, latent)
    n_kv = 4 if latent else 2
    return pl.pallas_call(
        functools.partial(_attn_diff_kernel, has_lat=latent, lam_init=lam_init),
        out_shape=jax.ShapeDtypeStruct((rows, N_HEADS * HEAD_DIM), BF16),
        grid=(rg.b, N_HEADS, n_q),
        in_specs=in_specs,
        out_specs=pl.BlockSpec((tq, HEAD_DIM), lambda b, hh, qi: (b * n_q + qi, hh)),
        compiler_params=_cparams(3),
        name="attn_diff_lat" if latent else "attn_diff_ctx",
    )(b_lambda, b_subln_g.reshape(1, HEAD_DIM), h, *([h] * n_kv))


def _attn_mla_kernel(q_ref, knc_ref, vc_ref, krc_ref, *rest, has_lat):
    if has_lat:
        knl_ref, vl_ref, krl_ref, o_ref = rest
    else:
        o_ref, = rest
    qn, qr = q_ref[:, :C_NOPE], q_ref[:, C_NOPE:]
    segs = [(knc_ref, krc_ref, vc_ref)] + ([(knl_ref, krl_ref, vl_ref)] if has_lat else [])
    fns = [lambda kn=kn, kr=kr: _dot_nt(qn, kn[...]) + _dot_nt(qr, kr[...]) for kn, kr, _ in segs]
    o = _softmax_pv(fns, [v[...] for _, _, v in segs], (C_NOPE + C_ROPE) ** -0.5)
    o_ref[...] = o.astype(o_ref.dtype)


def _attn_mla(qc, kv2, h, rg, *, latent):
    tq, n_q, rows, q0 = _q_geometry(rg, latent, 512)
    c_blk0 = rg.nl // rg.c
    kr_col = H_CKR // HEAD_DIM
    kv = _kv_specs(rg, lambda hh: 2 * hh, lambda hh: 2 * hh + 1, latent)
    in_specs = [pl.BlockSpec((tq, 2 * HEAD_DIM), lambda b, hh, qi: (q0 + b * n_q + qi, hh)),
                kv[0], kv[1],
                pl.BlockSpec((rg.c, HEAD_DIM), lambda b, hh, qi: (c_blk0 + b, kr_col))]
    args = [qc, kv2, kv2, h]
    if latent:
        in_specs += [kv[2], kv[3], pl.BlockSpec((rg.t, HEAD_DIM), lambda b, hh, qi: (b, kr_col))]
        args += [kv2, kv2, h]
    return pl.pallas_call(
        functools.partial(_attn_mla_kernel, has_lat=latent),
        out_shape=jax.ShapeDtypeStruct((rows, N_HEADS * HEAD_DIM), BF16),
        grid=(rg.b, N_HEADS, n_q),
        in_specs=in_specs,
        out_specs=pl.BlockSpec((tq, HEAD_DIM), lambda b, hh, qi: (b * n_q + qi, hh)),
        compiler_params=_cparams(3),
        name="attn_mla_lat" if latent else "attn_mla_ctx",
    )(*args)


def _dft_tables(n):
    idx = np.arange(n, dtype=np.int64)
    ang = (np.outer(idx, idx) % n).astype(np.float64) * (2.0 * np.pi / n)
    return jnp.asarray(np.cos(ang), F32), jnp.asarray(np.sin(ang), F32)


def _fourier_chan_kernel(z_ref, cc_ref, sc_ref, dw_ref, zc_ref, zs_ref):
    dw = dw_ref[...].astype(BF16)
    z = z_ref[...]
    zc_ref[...] = _dot(z, _dot(cc_ref[...], dw).astype(BF16)).astype(zc_ref.dtype)
    zs_ref[...] = _dot(z, _dot(sc_ref[...], dw).astype(BF16)).astype(zs_ref.dtype)


def _fourier_chan(h, cos_c, sin_c, d_w, rg):
    tm = rg.tm
    col0 = H_DX // HEAD_DIM
    tile = pl.BlockSpec((tm, HEAD_DIM), lambda i, g: (i, g))
    const = pl.BlockSpec((HEAD_DIM, HEAD_DIM), lambda i, g: (0, 0))
    out = jax.ShapeDtypeStruct((rg.n, N_HEADS * HEAD_DIM), BF16)
    return pl.pallas_call(
        _fourier_chan_kernel,
        out_shape=[out, out],
        grid=(rg.n // tm, N_HEADS),
        in_specs=[pl.BlockSpec((tm, HEAD_DIM), lambda i, g: (i, col0 + g)), const, const,
                  pl.BlockSpec((None, HEAD_DIM, HEAD_DIM), lambda i, g: (g, 0, 0))],
        out_specs=[tile, tile],
        compiler_params=_cparams(2),
        name="fourier_chan",
    )(h, cos_c, sin_c, d_w)


def _fourier_seq_kernel(ct_ref, st_ref, zc_ref, zs_ref, o_ref, *, norm):
    o = _dot(ct_ref[...], zc_ref[...]) - _dot(st_ref[...], zs_ref[...])
    o_ref[...] = (o * norm).astype(o_ref.dtype)


def _fourier_seq(cos_t, sin_t, zc, zs, rg, *, latent):
    length = rg.t if latent else rg.c
    tm = min(512, length)
    n_i = length // tm
    z_blk0 = 0 if latent else rg.nl // rg.c
    width = zc.shape[1]
    dft_spec = pl.BlockSpec((tm, length), lambda b, i: (i, 0))
    z_spec = pl.BlockSpec((length, width), lambda b, i: (z_blk0 + b, 0))
    return pl.pallas_call(
        functools.partial(_fourier_seq_kernel, norm=(length * HEAD_DIM) ** -0.5),
        out_shape=jax.ShapeDtypeStruct((rg.b * length, width), BF16),
        grid=(rg.b, n_i),
        in_specs=[dft_spec, dft_spec, z_spec, z_spec],
        out_specs=pl.BlockSpec((tm, width), lambda b, i: (b * n_i + i, 0)),
        compiler_params=_cparams(2),
        name="fourier_seq_lat" if latent else "fourier_seq_ctx",
    )(cos_t, sin_t, zc, zs)


def _prep_w_in(w_in):
    d = w_in.shape[0]
    aq, bq, cq, dx, ak, av, bk, bv, ckv, ckr = jnp.split(
        w_in, np.cumsum([1024, 1024, C_Q_LORA, 1024, 256, 256, 1024, 1024, C_KV_LORA]).tolist(), axis=1)
    pad = jnp.zeros((d, H_COLS - H_CKR - C_ROPE), w_in.dtype)
    return jnp.concatenate([cq, aq, bq, dx, ak, av, bk, bv, ckv, ckr, pad], axis=1).astype(BF16)


def _prep_w_uq(w_uq):
    k = w_uq.shape[0]
    w = w_uq.reshape(k, N_HEADS, C_NOPE + C_ROPE)
    w = jnp.pad(w, ((0, 0), (0, 0), (0, 2 * HEAD_DIM - C_NOPE - C_ROPE)))
    return w.reshape(k, N_HEADS * 2 * HEAD_DIM).astype(BF16)


def _in_proj_kinds():
    kinds = np.zeros((H_COLS // ROPE_TN,), np.int32)
    kinds[H_AQ // ROPE_TN:H_BQ // ROPE_TN] = ROPE_A
    kinds[H_BQ // ROPE_TN:H_DX // ROPE_TN] = ROPE_B
    kinds[H_AK // ROPE_TN:H_AV // ROPE_TN] = ROPE_A
    kinds[H_BK // ROPE_TN:H_BV // ROPE_TN] = ROPE_B
    kinds[H_CKR // ROPE_TN] = ROPE_KR
    return jnp.asarray(kinds)


def _ffn(u, w_gu, w_d, *, rows, tm):
    hid = _ffn_up(u, w_gu.astype(BF16), rows=rows, tm=tm, tn=512)
    return _mm(hid, w_d.astype(BF16), rows=rows, tm=tm, tn=1024, tk=2048, out_dtype=F32, name="ffn_down")


def _mixing(u, p, consts, rg, *, lam_init, need_ctx):
    cos_tab, sin_tab, kinds_in, dft = consts
    h = _proj(u, _prep_w_in(p["w_in"]), kinds_in, cos_tab, sin_tab, rg, name="in_proj")
    n_tiles = N_HEADS * 2 * HEAD_DIM // ROPE_TN
    qc = _proj(h, _prep_w_uq(p["c_w_uq"]), jnp.full((n_tiles,), ROPE_QR, jnp.int32), cos_tab, sin_tab, rg,
               a_col0=H_CQ, gain=p["c_q_norm_g"], name="mla_q_up")
    kv2 = _proj(h, p["c_w_ukv"].astype(BF16), jnp.zeros((n_tiles,), jnp.int32), cos_tab, sin_tab, rg,
                a_col0=H_CKV, gain=p["c_kv_norm_g"], name="mla_kv_up")
    zc, zs = _fourier_chan(h, dft["cos_c"], dft["sin_c"], p["d_w"], rg)

    def mixers(latent):
        ya = _attn_win(h, p["a_sink"], rg, latent=latent)
        yb = _attn_diff(h, p["b_lambda"], p["b_subln_g"], rg, latent=latent, lam_init=lam_init)
        yc = _attn_mla(qc, kv2, h, rg, latent=latent)
        cos_t, sin_t = (dft["cos_t"], dft["sin_t"]) if latent else (dft["cos_x"], dft["sin_x"])
        yd = _fourier_seq(cos_t, sin_t, zc, zs, rg, latent=latent)
        return jnp.concatenate([ya, yb, yc, yd], axis=1)

    y_in = mixers(True)
    if need_ctx:
        y_in = jnp.concatenate([y_in, mixers(False)], axis=0)
    rows = y_in.shape[0]
    return _mm(y_in, p["w_out"].astype(BF16), rows=rows, tm=rg.tm, tn=1024, tk=y_in.shape[1],
               out_dtype=F32, name="out_proj")


def kernel(x, c, ctx, c_ctx, w_mod, b_mod, ffn1_w_gu, ffn1_w_d, ffn2_w_gu, ffn2_w_d, ln_g, ln_b, w_in, w_out,
           a_sink, b_lambda, b_subln_g, c_q_norm_g, c_kv_norm_g, c_w_uq, c_w_ukv, d_w):
    n_batch, seq, d = x.shape
    ctx_len = ctx.shape[1]
    depth = w_mod.shape[0]
    rg = _Rows(n_batch, seq, ctx_len)
    alpha = (2.0 * depth) ** 0.25

    cos_tab, sin_tab = _rope_tables(seq)
    cos_t, sin_t = _dft_tables(seq)
    cos_x, sin_x = _dft_tables(ctx_len)
    cos_c, sin_c = _dft_tables(HEAD_DIM)
    dft = {"cos_t": cos_t.astype(BF16), "sin_t": sin_t.astype(BF16),
           "cos_x": cos_x.astype(BF16), "sin_x": sin_x.astype(BF16),
           "cos_c": cos_c.astype(BF16), "sin_c": sin_c.astype(BF16)}
    consts = (cos_tab, sin_tab, _in_proj_kinds(), dft)

    xs = jnp.concatenate([x.reshape(n_batch * seq, d), ctx.reshape(n_batch * ctx_len, d)], axis=0)
    c_rows = jnp.concatenate([c, c_ctx[None, :], jnp.zeros((8 - n_batch - 1, d), c.dtype)], axis=0)

    mods = [_mod_vectors(c_rows, w_mod[l], b_mod[l])[:n_batch + 1].reshape(n_batch + 1, N_MOD, d)
            for l in range(depth)]
    u = _modulate(xs, mods[0], rg, shift_idx=0)
    for l in range(depth):
        last = l == depth - 1
        lam_init = 0.8 - 0.6 * math.exp(-0.3 * l)
        mod = mods[l]
        y = _ffn(u, ffn1_w_gu[l], ffn1_w_d[l], rows=rg.n, tm=rg.tm)
        xs, u = _ln_res(xs, y, mod, ln_g[l, 0], ln_b[l, 0], rg, rows=rg.n, alpha=alpha,
                        gate_idx=2, gate_mul=0.5, next_mod=mod, next_shift_idx=3, name="ln_ffn1")
        p = {"w_in": w_in[l], "w_out": w_out[l], "a_sink": a_sink[l], "b_lambda": b_lambda[l],
             "b_subln_g": b_subln_g[l], "c_q_norm_g": c_q_norm_g[l], "c_kv_norm_g": c_kv_norm_g[l],
             "c_w_uq": c_w_uq[l], "c_w_ukv": c_w_ukv[l], "d_w": d_w[l]}
        y = _mixing(u, p, consts, rg, lam_init=lam_init, need_ctx=not last)
        rows = rg.nl if last else rg.n
        xs, u = _ln_res(xs, y, mod, ln_g[l, 1], ln_b[l, 1], rg, rows=rows, alpha=alpha,
                        gate_idx=5, gate_mul=1.0, next_mod=mod, next_shift_idx=6, name="ln_mix")
        y = _ffn(u, ffn2_w_gu[l], ffn2_w_d[l], rows=rows, tm=rg.tm)
        xs, u = _ln_res(xs, y, mod, ln_g[l, 2], ln_b[l, 2], rg, rows=rows, alpha=alpha,
                        gate_idx=8, gate_mul=0.5, next_mod=None if last else mods[l + 1],
                        next_shift_idx=None if last else 0, name="ln_ffn2")
    return xs[:rg.nl].reshape(n_batch, seq, d)
```

```python
import functools
import math

import numpy as np
import jax
import jax.numpy as jnp
from jax import lax
from jax.experimental import pallas as pl
from jax.experimental.pallas import tpu as pltpu

F32 = jnp.float32
BF16 = jnp.bfloat16

GRID_W = 64
HEAD_DIM = 128
WINDOW = 128
N_HEADS = 8
A_KV_HEADS = 2
A_GROUP = N_HEADS // A_KV_HEADS
B_QK_DIM = 64
C_Q_LORA = 1536
C_KV_LORA = 512
C_NOPE = 128
C_ROPE = 64
N_MOD = 9
ROPE_BASE = 10000.0
LN_EPS = 1e-5
RMS_EPS = 1e-6
NEG_INF = -1e30

H_CQ, H_AQ, H_BQ, H_DX = 0, 1536, 2560, 3584
H_AK, H_AV, H_BK, H_BV, H_CKV, H_CKR = 4608, 4864, 5120, 6144, 7168, 7680
H_COLS = 7936

ROPE_NONE, ROPE_A, ROPE_B, ROPE_KR, ROPE_QR = 0, 1, 2, 3, 4

VMEM_LIMIT_BYTES = 56 * 1024 * 1024
ROPE_TN = 256


def _cparams(n_axes):
    return pltpu.CompilerParams(dimension_semantics=("arbitrary",) * n_axes,
                                vmem_limit_bytes=VMEM_LIMIT_BYTES)


def _dot(a, b):
    return jnp.dot(a, b, preferred_element_type=F32)


def _dot_nt(a, b):
    return lax.dot_general(a, b, (((1,), (1,)), ((), ())), preferred_element_type=F32)


def _mm_kernel(a_ref, b_ref, o_ref):
    @pl.when(pl.program_id(2) == 0)
    def _():
        o_ref[...] = jnp.zeros_like(o_ref)

    o_ref[...] += _dot(a_ref[...], b_ref[...].astype(BF16))


def _mm(a, b, layer, *, rows, tm, tn, tk, name):
    _, kdim, n = b.shape
    assert rows % tm == 0 and n % tn == 0 and kdim % tk == 0
    return pl.pallas_call(
        _mm_kernel,
        out_shape=jax.ShapeDtypeStruct((rows, n), F32),
        grid=(rows // tm, n // tn, kdim // tk),
        in_specs=[pl.BlockSpec((tm, tk), lambda i, j, k: (i, k)),
                  pl.BlockSpec((None, tk, tn), lambda i, j, k: (layer, k, j))],
        out_specs=pl.BlockSpec((tm, tn), lambda i, j, k: (i, j)),
        compiler_params=_cparams(3),
        name=name,
    )(a, b)


def _out_proj_kernel(*refs):
    *y_refs, w_ref, o_ref = refs
    width = y_refs[0].shape[1]
    acc = None
    for m, y_ref in enumerate(y_refs):
        part = _dot(y_ref[...], w_ref[m * width:(m + 1) * width, :].astype(BF16))
        acc = part if acc is None else acc + part
    o_ref[...] = acc.astype(o_ref.dtype)


def _out_proj(ys, w_out, layer, *, rows, tm, tn):
    _, kdim, n = w_out.shape
    width = ys[0].shape[1]
    assert kdim == width * len(ys)
    y_spec = pl.BlockSpec((tm, width), lambda i, j: (i, 0))
    return pl.pallas_call(
        _out_proj_kernel,
        out_shape=jax.ShapeDtypeStruct((rows, n), F32),
        grid=(rows // tm, n // tn),
        in_specs=[y_spec] * len(ys) + [pl.BlockSpec((None, kdim, tn), lambda i, j: (layer, 0, j))],
        out_specs=pl.BlockSpec((tm, tn), lambda i, j: (i, j)),
        compiler_params=_cparams(2),
        name="out_proj",
    )(*ys, w_out)


def _mod_kernel(c_ref, w_ref, b_ref, o_ref):
    c = c_ref[...]
    sc = (c * jax.nn.sigmoid(c)).astype(BF16)
    o_ref[...] = _dot(sc, w_ref[...].astype(BF16)) + b_ref[...]


def _mod_vectors(c_rows, w_mod, b_mod):
    depth, d, n = w_mod.shape
    r = c_rows.shape[0]
    tn = 1024
    return pl.pallas_call(
        _mod_kernel,
        out_shape=jax.ShapeDtypeStruct((depth, r, n), F32),
        grid=(depth, n // tn),
        in_specs=[pl.BlockSpec((r, d), lambda l, j: (0, 0)),
                  pl.BlockSpec((None, d, tn), lambda l, j: (l, 0, j)),
                  pl.BlockSpec((None, 1, tn), lambda l, j: (l, 0, j))],
        out_specs=pl.BlockSpec((None, r, tn), lambda l, j: (l, 0, j)),
        compiler_params=_cparams(2),
        name="mod_vectors",
    )(c_rows, w_mod, b_mod.reshape(depth, 1, n))


def _seg_index(i, n_lat_tiles, tiles_per_batch, n_batch):
    return jnp.where(i < n_lat_tiles, i // tiles_per_batch, n_batch)


def _modulate_kernel(x_ref, mod_ref, u_ref, *, shift_idx):
    shift = mod_ref[shift_idx:shift_idx + 1, :]
    scale = mod_ref[shift_idx + 1:shift_idx + 2, :]
    u_ref[...] = (x_ref[...] * (1.0 + scale) + shift).astype(u_ref.dtype)


def _ln_res_kernel(x_ref, y_ref, mod_ref, g_ref, b_ref, *rest, alpha, gate_idx, gate_mul, next_shift_idx):
    gate = mod_ref[gate_idx:gate_idx + 1, :]
    z = alpha * x_ref[...] + (gate_mul * gate) * y_ref[...].astype(F32)
    mu = jnp.mean(z, axis=-1, keepdims=True)
    zc = z - mu
    var = jnp.mean(zc * zc, axis=-1, keepdims=True)
    xn = zc * lax.rsqrt(var + LN_EPS) * g_ref[...] + b_ref[...]
    if next_shift_idx is None:
        xo_ref, = rest
        xo_ref[...] = xn
    else:
        nmod_ref, xo_ref, u_ref = rest
        xo_ref[...] = xn
        shift = nmod_ref[next_shift_idx:next_shift_idx + 1, :]
        scale = nmod_ref[next_shift_idx + 1:next_shift_idx + 2, :]
        u_ref[...] = (xn * (1.0 + scale) + shift).astype(u_ref.dtype)


class _Rows:
    def __init__(self, n_batch, seq, ctx_len):
        self.b, self.t, self.c = n_batch, seq, ctx_len
        self.nl, self.nc = n_batch * seq, n_batch * ctx_len
        self.n = self.nl + self.nc
        self.tr = math.gcd(256, math.gcd(seq, ctx_len))
        self.tm = math.gcd(1024, math.gcd(seq, self.nc))

    def mod_spec(self, tile, d):
        n_lat_tiles, per_batch, nb = self.nl // tile, self.t // tile, self.b
        return pl.BlockSpec((None, N_MOD, d),
                            lambda i, *_: (_seg_index(i, n_lat_tiles, per_batch, nb), 0, 0))


def _modulate(x, mod, rg, *, shift_idx):
    n, d = x.shape
    tr = rg.tr
    return pl.pallas_call(
        functools.partial(_modulate_kernel, shift_idx=shift_idx),
        out_shape=jax.ShapeDtypeStruct((n, d), BF16),
        grid=(n // tr,),
        in_specs=[pl.BlockSpec((tr, d), lambda i: (i, 0)), rg.mod_spec(tr, d)],
        out_specs=pl.BlockSpec((tr, d), lambda i: (i, 0)),
        compiler_params=_cparams(1),
        name="modulate",
    )(x, mod)


def _ln_res(x, y, mod, ln_g, ln_b, rg, *, rows, alpha, gate_idx, gate_mul, next_mod, next_shift_idx, name):
    d = x.shape[1]
    tr = rg.tr
    row_spec = pl.BlockSpec((tr, d), lambda i: (i, 0))
    vec_spec = pl.BlockSpec((1, d), lambda i: (0, 0))
    out_shape = [jax.ShapeDtypeStruct((rows, d), F32)]
    out_specs = [row_spec]
    in_specs = [row_spec, row_spec, rg.mod_spec(tr, d), vec_spec, vec_spec]
    args = [x, y, mod, ln_g.reshape(1, d), ln_b.reshape(1, d)]
    if next_shift_idx is not None:
        in_specs.append(rg.mod_spec(tr, d))
        args.append(next_mod)
        out_shape.append(jax.ShapeDtypeStruct((rows, d), BF16))
        out_specs.append(row_spec)
    res = pl.pallas_call(
        functools.partial(_ln_res_kernel, alpha=alpha, gate_idx=gate_idx, gate_mul=gate_mul,
                          next_shift_idx=next_shift_idx),
        out_shape=out_shape,
        grid=(rows // tr,),
        in_specs=in_specs,
        out_specs=out_specs,
        compiler_params=_cparams(1),
        name=name,
    )(*args)
    return (res[0], res[1]) if next_shift_idx is not None else (res[0], None)


def _ffn_up_kernel(a_ref, wg_ref, wu_ref, o_ref):
    a = a_ref[...]
    g = _dot(a, wg_ref[...].astype(BF16))
    up = _dot(a, wu_ref[...].astype(BF16))
    o_ref[...] = (g * jax.nn.sigmoid(g) * up).astype(o_ref.dtype)


def _ffn_up(u, w_gu, layer, *, rows, tm, tn):
    _, d, two_ff = w_gu.shape
    ff = two_ff // 2
    up0 = ff // tn
    return pl.pallas_call(
        _ffn_up_kernel,
        out_shape=jax.ShapeDtypeStruct((rows, ff), BF16),
        grid=(rows // tm, ff // tn),
        in_specs=[pl.BlockSpec((tm, d), lambda i, j: (i, 0)),
                  pl.BlockSpec((None, d, tn), lambda i, j: (layer, 0, j)),
                  pl.BlockSpec((None, d, tn), lambda i, j: (layer, 0, up0 + j))],
        out_specs=pl.BlockSpec((tm, tn), lambda i, j: (i, j)),
        compiler_params=_cparams(2),
        name="ffn_up",
    )(u, w_gu, w_gu)


def _rope_tile(x, cos, sin, half):
    lane = lax.broadcasted_iota(jnp.int32, x.shape, 1)
    first = (lane % (2 * half)) < half
    width = x.shape[1]
    partner = jnp.where(first, pltpu.roll(x, width - half, 1), pltpu.roll(x, half, 1))
    return x * cos + partner * sin


def _proj_kernel(kind_ref, src_ref, a_ref, b_ref, *rest, has_gain, n_lat_tiles):
    if has_gain:
        gain_ref, cos_ref, sin_ref, o_ref, an_ref = rest
    else:
        cos_ref, sin_ref, o_ref = rest
    i, j = pl.program_id(0), pl.program_id(1)

    if has_gain:
        @pl.when(j == 0)
        def _():
            x = a_ref[...].astype(F32)
            ms = jnp.mean(x * x, axis=-1, keepdims=True)
            an_ref[...] = (x * lax.rsqrt(ms + RMS_EPS) * gain_ref[...]).astype(BF16)
        a = an_ref[...]
    else:
        a = a_ref[...]
    kind = jnp.where(i < n_lat_tiles, kind_ref[j], ROPE_NONE)

    @pl.when(kind == ROPE_NONE)
    def _():
        o_ref[...] = _dot(a, b_ref[...].astype(BF16)).astype(o_ref.dtype)

    @pl.when(kind == ROPE_A)
    def _():
        acc = _dot(a, b_ref[...].astype(BF16))
        o_ref[...] = _rope_tile(acc, cos_ref[...], sin_ref[...], 32).astype(o_ref.dtype)

    @pl.when(kind > ROPE_A)
    def _():
        acc = _dot(a, b_ref[...].astype(BF16))
        o_ref[...] = _rope_tile(acc, cos_ref[...], sin_ref[...], 16).astype(o_ref.dtype)


def _proj(a, w, layer, kinds, src_tiles, cos_tab, sin_tab, rg, *, a_col0=0, gain=None, name):
    _, kdim, n = w.shape
    tm, tn = rg.tm, ROPE_TN
    assert a_col0 % kdim == 0 and n % tn == 0
    a_blk = a_col0 // kdim
    n_lat_tiles = rg.nl // tm
    t_tiles = rg.t // tm
    has_gain = gain is not None

    def tab_map(i, j, kind_ref, src_ref):
        return (jnp.where(i < n_lat_tiles, kind_ref[j], ROPE_NONE), i % t_tiles, 0)

    in_specs = [pl.BlockSpec((tm, kdim), lambda i, j, kr, sr: (i, a_blk)),
                pl.BlockSpec((None, kdim, tn), lambda i, j, kr, sr: (layer, 0, sr[j]))]
    args = [a, w]
    scratch = []
    if has_gain:
        in_specs.append(pl.BlockSpec((None, 1, kdim), lambda i, j, kr, sr: (layer, 0, 0)))
        args.append(gain.reshape(gain.shape[0], 1, kdim))
        scratch.append(pltpu.VMEM((tm, kdim), BF16))
    in_specs += [pl.BlockSpec((None, tm, tn), tab_map), pl.BlockSpec((None, tm, tn), tab_map)]
    args += [cos_tab, sin_tab]
    return pl.pallas_call(
        functools.partial(_proj_kernel, has_gain=has_gain, n_lat_tiles=n_lat_tiles),
        out_shape=jax.ShapeDtypeStruct((rg.n, n), BF16),
        grid_spec=pltpu.PrefetchScalarGridSpec(
            num_scalar_prefetch=2,
            grid=(rg.n // tm, n // tn),
            in_specs=in_specs,
            out_specs=pl.BlockSpec((tm, tn), lambda i, j, kr, sr: (i, j)),
            scratch_shapes=scratch),
        compiler_params=_cparams(2),
        name=name,
    )(kinds, src_tiles, *args)


def _rope_tables(seq):
    t = np.arange(seq)
    row, col = (t // GRID_W).astype(np.float64), (t % GRID_W).astype(np.float64)

    def pattern(rot_dim):
        axis_dim = rot_dim // 2
        inv = ROPE_BASE ** (-np.arange(0, axis_dim, 2, dtype=np.float64) / axis_dim)
        ar, ac = row[:, None] * inv[None, :], col[:, None] * inv[None, :]
        cos = np.concatenate([np.cos(ar), np.cos(ar), np.cos(ac), np.cos(ac)], axis=1)
        sin = np.concatenate([-np.sin(ar), np.sin(ar), -np.sin(ac), np.sin(ac)], axis=1)
        return cos, sin

    cos = np.ones((5, seq, ROPE_TN), np.float64)
    sin = np.zeros((5, seq, ROPE_TN), np.float64)
    c128, s128 = pattern(HEAD_DIM)
    c64, s64 = pattern(2 * 32)
    cos[ROPE_A], sin[ROPE_A] = np.tile(c128, (1, 2)), np.tile(s128, (1, 2))
    cos[ROPE_B], sin[ROPE_B] = np.tile(c64, (1, 4)), np.tile(s64, (1, 4))
    cos[ROPE_KR, :, :64], sin[ROPE_KR, :, :64] = c64, s64
    cos[ROPE_QR, :, 128:192], sin[ROPE_QR, :, 128:192] = c64, s64
    return jnp.asarray(cos, F32), jnp.asarray(sin, F32)


def _mixer_call(body, rg, *, latent, dst, grid, in_specs, args, tq, n_q, out_block_w, name):
    row0 = 0 if latent else rg.nl // tq
    aliases = {}
    if not latent:
        in_specs = in_specs + [pl.BlockSpec(memory_space=pl.ANY)]
        args = args + [dst]
        aliases = {len(args) - 1: 0}
    return pl.pallas_call(
        body,
        out_shape=jax.ShapeDtypeStruct((rg.n, N_HEADS * HEAD_DIM), BF16),
        grid=grid,
        in_specs=in_specs,
        out_specs=pl.BlockSpec((tq, out_block_w), lambda b, hh, qi: (row0 + b * n_q + qi, hh)),
        input_output_aliases=aliases,
        compiler_params=_cparams(3),
        name=name,
    )(*args)


def _attn_win_kernel(sink_ref, q_ref, kc_ref, vc_ref, *rest, has_lat, tq, seq, layer):
    if has_lat:
        kl_ref, vl_ref, o_ref = rest
    else:
        _, o_ref = rest
    g, qi = pl.program_id(1), pl.program_id(2)
    scale = HEAD_DIM ** -0.5
    kc, vc = kc_ref[...], vc_ref[...]
    if has_lat:
        win = min(seq, tq + 2 * WINDOW)
        ws = pl.multiple_of(jnp.clip(qi * tq - WINDOW, 0, seq - win), WINDOW)
        kw, vw = kl_ref[pl.ds(ws, win), :], vl_ref[pl.ds(ws, win), :]
        qpos = qi * tq + lax.broadcasted_iota(jnp.int32, (tq, win), 0)
        kpos = ws + lax.broadcasted_iota(jnp.int32, (tq, win), 1)
        valid = jnp.abs(qpos - kpos) <= WINDOW
    for j in range(A_GROUP):
        q = q_ref[:, j * HEAD_DIM:(j + 1) * HEAD_DIM]
        sink = sink_ref[layer, g * A_GROUP + j]
        s_c = _dot_nt(q, kc) * scale
        m = jnp.maximum(jnp.max(s_c, axis=-1, keepdims=True), sink)
        if has_lat:
            s_l = jnp.where(valid, _dot_nt(q, kw) * scale, NEG_INF)
            m = jnp.maximum(m, jnp.max(s_l, axis=-1, keepdims=True))
        p_c = jnp.exp(s_c - m)
        den = jnp.sum(p_c, axis=-1, keepdims=True) + jnp.exp(sink - m)
        o = _dot(p_c.astype(BF16), vc)
        if has_lat:
            p_l = jnp.exp(s_l - m)
            den = den + jnp.sum(p_l, axis=-1, keepdims=True)
            o = o + _dot(p_l.astype(BF16), vw)
        o_ref[:, j * HEAD_DIM:(j + 1) * HEAD_DIM] = (o / den).astype(o_ref.dtype)


def _attn_win(h, a_sink, layer, rg, *, latent, dst=None):
    gw = A_GROUP * HEAD_DIM
    tq, n_q, q0 = _q_geometry(rg, latent, 256)
    c_blk0 = rg.nl // rg.c
    in_specs = [pl.BlockSpec(memory_space=pltpu.SMEM),
                pl.BlockSpec((tq, gw), lambda b, g, qi: (q0 + b * n_q + qi, H_AQ // gw + g)),
                pl.BlockSpec((rg.c, HEAD_DIM), lambda b, g, qi: (c_blk0 + b, H_AK // HEAD_DIM + g)),
                pl.BlockSpec((rg.c, HEAD_DIM), lambda b, g, qi: (c_blk0 + b, H_AV // HEAD_DIM + g))]
    args = [a_sink, h, h, h]
    if latent:
        in_specs += [pl.BlockSpec((rg.t, HEAD_DIM), lambda b, g, qi: (b, H_AK // HEAD_DIM + g)),
                     pl.BlockSpec((rg.t, HEAD_DIM), lambda b, g, qi: (b, H_AV // HEAD_DIM + g))]
        args += [h, h]
    return _mixer_call(functools.partial(_attn_win_kernel, has_lat=latent, tq=tq, seq=rg.t, layer=layer), rg,
                       latent=latent, dst=dst, grid=(rg.b, A_KV_HEADS, n_q), in_specs=in_specs, args=args,
                       tq=tq, n_q=n_q, out_block_w=gw, name="attn_win_lat" if latent else "attn_win_ctx")


def _softmax_pv(score_fns, values, scale):
    scores = [f() * scale for f in score_fns]
    m = functools.reduce(jnp.maximum, [jnp.max(s, axis=-1, keepdims=True) for s in scores])
    den, out = None, None
    for s, v in zip(scores, values):
        p = jnp.exp(s - m)
        d = jnp.sum(p, axis=-1, keepdims=True)
        o = _dot(p.astype(BF16), v)
        den = d if den is None else den + d
        out = o if out is None else out + o
    return out / den


def _attn_diff_kernel(lam_ref, gsub_ref, q_ref, kc_ref, vc_ref, *rest, has_lat, lam_init):
    if has_lat:
        kl_ref, vl_ref, o_ref = rest
    else:
        _, o_ref = rest
    lf = lam_ref[...]
    lam = (jnp.exp(jnp.sum(lf[0:1] * lf[1:2], axis=-1, keepdims=True))
           - jnp.exp(jnp.sum(lf[2:3] * lf[3:4], axis=-1, keepdims=True)) + lam_init)
    q = q_ref[...]
    first_map = lax.broadcasted_iota(jnp.int32, q.shape, 1) < B_QK_DIM
    zero = jnp.zeros_like(q)
    keys = [kc_ref[...]] + ([kl_ref[...]] if has_lat else [])
    values = [vc_ref[...]] + ([vl_ref[...]] if has_lat else [])

    def one_map(qm):
        return _softmax_pv([functools.partial(_dot_nt, qm, k) for k in keys], values, B_QK_DIM ** -0.5)

    o = one_map(jnp.where(first_map, q, zero)) - lam * one_map(jnp.where(first_map, zero, q))
    ms = jnp.mean(o * o, axis=-1, keepdims=True)
    o = o * lax.rsqrt(ms + RMS_EPS) * gsub_ref[...] * (1.0 - lam_init)
    o_ref[...] = o.astype(o_ref.dtype)


def _kv_specs(rg, col_k, col_v, latent, k_width=HEAD_DIM):
    c_blk0 = rg.nl // rg.c
    specs = [pl.BlockSpec((rg.c, k_width), lambda b, hh, qi: (c_blk0 + b, col_k(hh))),
             pl.BlockSpec((rg.c, HEAD_DIM), lambda b, hh, qi: (c_blk0 + b, col_v(hh)))]
    if latent:
        specs += [pl.BlockSpec((rg.t, k_width), lambda b, hh, qi: (b, col_k(hh))),
                  pl.BlockSpec((rg.t, HEAD_DIM), lambda b, hh, qi: (b, col_v(hh)))]
    return specs


def _q_geometry(rg, latent, tq_lat):
    if latent:
        tq = min(tq_lat, rg.t)
        return tq, rg.t // tq, 0
    return rg.c, 1, rg.nl // rg.c


def _attn_diff(h, b_lambda, b_subln_g, layer, rg, *, latent, lam_init, dst=None):
    tq, n_q, q0 = _q_geometry(rg, latent, 512)
    in_specs = [pl.BlockSpec((None, 4, B_QK_DIM), lambda b, hh, qi: (layer, 0, 0)),
                pl.BlockSpec((None, 1, HEAD_DIM), lambda b, hh, qi: (layer, 0, 0)),
                pl.BlockSpec((tq, HEAD_DIM), lambda b, hh, qi: (q0 + b * n_q + qi, H_BQ // HEAD_DIM + hh))]
    in_specs += _kv_specs(rg, lambda hh: H_BK // HEAD_DIM + hh, lambda hh: H_BV // HEAD_DIM + hh, latent)
    n_kv = 4 if latent else 2
    args = [b_lambda, b_subln_g.reshape(b_subln_g.shape[0], 1, HEAD_DIM), h] + [h] * n_kv
    return _mixer_call(functools.partial(_attn_diff_kernel, has_lat=latent, lam_init=lam_init), rg,
                       latent=latent, dst=dst, grid=(rg.b, N_HEADS, n_q), in_specs=in_specs, args=args,
                       tq=tq, n_q=n_q, out_block_w=HEAD_DIM, name="attn_diff_lat" if latent else "attn_diff_ctx")


def _attn_mla_kernel(q_ref, knc_ref, vc_ref, krc_ref, *rest, has_lat):
    if has_lat:
        knl_ref, vl_ref, krl_ref, o_ref = rest
    else:
        _, o_ref = rest
    qn, qr = q_ref[:, :C_NOPE], q_ref[:, C_NOPE:]
    segs = [(knc_ref, krc_ref, vc_ref)] + ([(knl_ref, krl_ref, vl_ref)] if has_lat else [])
    fns = [lambda kn=kn, kr=kr: _dot_nt(qn, kn[...]) + _dot_nt(qr, kr[...]) for kn, kr, _ in segs]
    o = _softmax_pv(fns, [v[...] for _, _, v in segs], (C_NOPE + C_ROPE) ** -0.5)
    o_ref[...] = o.astype(o_ref.dtype)


def _attn_mla(qc, kv2, h, rg, *, latent, dst=None):
    tq, n_q, q0 = _q_geometry(rg, latent, 512)
    c_blk0 = rg.nl // rg.c
    kr_col = H_CKR // HEAD_DIM
    kv = _kv_specs(rg, lambda hh: 2 * hh, lambda hh: 2 * hh + 1, latent)
    in_specs = [pl.BlockSpec((tq, 2 * HEAD_DIM), lambda b, hh, qi: (q0 + b * n_q + qi, hh)),
                kv[0], kv[1],
                pl.BlockSpec((rg.c, HEAD_DIM), lambda b, hh, qi: (c_blk0 + b, kr_col))]
    args = [qc, kv2, kv2, h]
    if latent:
        in_specs += [kv[2], kv[3], pl.BlockSpec((rg.t, HEAD_DIM), lambda b, hh, qi: (b, kr_col))]
        args += [kv2, kv2, h]
    return _mixer_call(functools.partial(_attn_mla_kernel, has_lat=latent), rg,
                       latent=latent, dst=dst, grid=(rg.b, N_HEADS, n_q), in_specs=in_specs, args=args,
                       tq=tq, n_q=n_q, out_block_w=HEAD_DIM, name="attn_mla_lat" if latent else "attn_mla_ctx")


def _dft_tables(n):
    idx = np.arange(n, dtype=np.int64)
    ang = (np.outer(idx, idx) % n).astype(np.float64) * (2.0 * np.pi / n)
    return jnp.asarray(np.cos(ang), F32), jnp.asarray(np.sin(ang), F32)


def _fourier_chan_kernel(z_ref, cc_ref, sc_ref, dw_ref, zc_ref, zs_ref):
    dw = dw_ref[...].astype(BF16)
    z = z_ref[...]
    zc_ref[...] = _dot(z, _dot(cc_ref[...], dw).astype(BF16)).astype(zc_ref.dtype)
    zs_ref[...] = _dot(z, _dot(sc_ref[...], dw).astype(BF16)).astype(zs_ref.dtype)


def _fourier_chan(h, cos_c, sin_c, d_w, layer, rg):
    tm = rg.tm
    col0 = H_DX // HEAD_DIM
    tile = pl.BlockSpec((tm, HEAD_DIM), lambda i, g: (i, g))
    const = pl.BlockSpec((HEAD_DIM, HEAD_DIM), lambda i, g: (0, 0))
    out = jax.ShapeDtypeStruct((rg.n, N_HEADS * HEAD_DIM), BF16)
    return pl.pallas_call(
        _fourier_chan_kernel,
        out_shape=[out, out],
        grid=(rg.n // tm, N_HEADS),
        in_specs=[pl.BlockSpec((tm, HEAD_DIM), lambda i, g: (i, col0 + g)), const, const,
                  pl.BlockSpec((None, None, HEAD_DIM, HEAD_DIM), lambda i, g: (layer, g, 0, 0))],
        out_specs=[tile, tile],
        compiler_params=_cparams(2),
        name="fourier_chan",
    )(h, cos_c, sin_c, d_w)


def _fourier_seq_kernel(ct_ref, st_ref, zc_ref, zs_ref, *rest, norm):
    o_ref = rest[-1]
    o = _dot(ct_ref[...], zc_ref[...]) - _dot(st_ref[...], zs_ref[...])
    o_ref[...] = (o * norm).astype(o_ref.dtype)


def _fourier_seq(cos_t, sin_t, zc, zs, rg, *, latent, dst=None):
    length = rg.t if latent else rg.c
    tm = min(512, length)
    n_i = length // tm
    z_blk0 = 0 if latent else rg.nl // rg.c
    row0 = 0 if latent else rg.nl // tm
    width = zc.shape[1]
    dft_spec = pl.BlockSpec((tm, length), lambda b, i: (i, 0))
    z_spec = pl.BlockSpec((length, width), lambda b, i: (z_blk0 + b, 0))
    in_specs, args, aliases = [dft_spec, dft_spec, z_spec, z_spec], [cos_t, sin_t, zc, zs], {}
    if not latent:
        in_specs.append(pl.BlockSpec(memory_space=pl.ANY))
        args.append(dst)
        aliases = {len(args) - 1: 0}
    return pl.pallas_call(
        functools.partial(_fourier_seq_kernel, norm=(length * HEAD_DIM) ** -0.5),
        out_shape=jax.ShapeDtypeStruct((rg.n, width), BF16),
        grid=(rg.b, n_i),
        in_specs=in_specs,
        out_specs=pl.BlockSpec((tm, width), lambda b, i: (row0 + b * n_i + i, 0)),
        input_output_aliases=aliases,
        compiler_params=_cparams(2),
        name="fourier_seq_lat" if latent else "fourier_seq_ctx",
    )(*args)


def _prep_w_in(w_in):
    return jnp.pad(w_in, ((0, 0), (0, 0), (0, H_COLS - w_in.shape[2]))).astype(BF16)


def _prep_w_uq(w_uq):
    depth, k, _ = w_uq.shape
    w = w_uq.reshape(depth, k, N_HEADS, C_NOPE + C_ROPE)
    w = jnp.pad(w, ((0, 0), (0, 0), (0, 0), (0, 2 * HEAD_DIM - C_NOPE - C_ROPE)))
    return w.reshape(depth, k, N_HEADS * 2 * HEAD_DIM).astype(BF16)


def _in_proj_tiles():
    src_cols = {"aq": 0, "bq": 1024, "cq": 2048, "dx": 3584, "ak": 4608, "av": 4864, "bk": 5120, "bv": 6144,
                "ckv": 7168, "ckr": 7680}
    layout = [("cq", H_CQ, H_AQ, ROPE_NONE), ("aq", H_AQ, H_BQ, ROPE_A), ("bq", H_BQ, H_DX, ROPE_B),
              ("dx", H_DX, H_AK, ROPE_NONE), ("ak", H_AK, H_AV, ROPE_A), ("av", H_AV, H_BK, ROPE_NONE),
              ("bk", H_BK, H_BV, ROPE_B), ("bv", H_BV, H_CKV, ROPE_NONE), ("ckv", H_CKV, H_CKR, ROPE_NONE),
              ("ckr", H_CKR, H_COLS, ROPE_KR)]
    kinds = np.zeros((H_COLS // ROPE_TN,), np.int32)
    src = np.zeros((H_COLS // ROPE_TN,), np.int32)
    for name, lo, hi, kind in layout:
        n = (hi - lo) // ROPE_TN
        kinds[lo // ROPE_TN:hi // ROPE_TN] = kind
        src[lo // ROPE_TN:hi // ROPE_TN] = src_cols[name] // ROPE_TN + np.arange(n)
    return jnp.asarray(kinds), jnp.asarray(src)


def _ffn(u, w_gu, w_d, layer, *, rows, tm):
    hid = _ffn_up(u, w_gu, layer, rows=rows, tm=tm, tn=256)
    return _mm(hid, w_d, layer, rows=rows, tm=tm, tn=1024, tk=2048, name="ffn_down")


def _mixing(u, p, layer, consts, rg, *, lam_init, need_ctx):
    cos_tab, sin_tab, (kinds_in, src_in), dft = consts
    h = _proj(u, p["w_in"], layer, kinds_in, src_in, cos_tab, sin_tab, rg, name="in_proj")
    n_tiles = N_HEADS * 2 * HEAD_DIM // ROPE_TN
    ident = jnp.arange(n_tiles, dtype=jnp.int32)
    qc = _proj(h, p["c_w_uq"], layer, jnp.full((n_tiles,), ROPE_QR, jnp.int32), ident, cos_tab, sin_tab, rg,
               a_col0=H_CQ, gain=p["c_q_norm_g"], name="mla_q_up")
    kv2 = _proj(h, p["c_w_ukv"], layer, jnp.zeros((n_tiles,), jnp.int32), ident, cos_tab, sin_tab, rg,
                a_col0=H_CKV, gain=p["c_kv_norm_g"], name="mla_kv_up")
    zc, zs = _fourier_chan(h, dft["cos_c"], dft["sin_c"], p["d_w"], layer, rg)

    ya = _attn_win(h, p["a_sink"], layer, rg, latent=True)
    yb = _attn_diff(h, p["b_lambda"], p["b_subln_g"], layer, rg, latent=True, lam_init=lam_init)
    yc = _attn_mla(qc, kv2, h, rg, latent=True)
    yd = _fourier_seq(dft["cos_t"], dft["sin_t"], zc, zs, rg, latent=True)
    if need_ctx:
        ya = _attn_win(h, p["a_sink"], layer, rg, latent=False, dst=ya)
        yb = _attn_diff(h, p["b_lambda"], p["b_subln_g"], layer, rg, latent=False, lam_init=lam_init, dst=yb)
        yc = _attn_mla(qc, kv2, h, rg, latent=False, dst=yc)
        yd = _fourier_seq(dft["cos_x"], dft["sin_x"], zc, zs, rg, latent=False, dst=yd)
    rows = rg.n if need_ctx else rg.nl
    return _out_proj([ya, yb, yc, yd], p["w_out"], layer, rows=rows, tm=rg.tm, tn=512)


def kernel(x, c, ctx, c_ctx, w_mod, b_mod, ffn1_w_gu, ffn1_w_d, ffn2_w_gu, ffn2_w_d, ln_g, ln_b, w_in, w_out,
           a_sink, b_lambda, b_subln_g, c_q_norm_g, c_kv_norm_g, c_w_uq, c_w_ukv, d_w):
    n_batch, seq, d = x.shape
    ctx_len = ctx.shape[1]
    depth = w_mod.shape[0]
    rg = _Rows(n_batch, seq, ctx_len)
    alpha = (2.0 * depth) ** 0.25

    cos_tab, sin_tab = _rope_tables(seq)
    cos_t, sin_t = _dft_tables(seq)
    cos_x, sin_x = _dft_tables(ctx_len)
    cos_c, sin_c = _dft_tables(HEAD_DIM)
    dft = {"cos_t": cos_t.astype(BF16), "sin_t": sin_t.astype(BF16),
           "cos_x": cos_x.astype(BF16), "sin_x": sin_x.astype(BF16),
           "cos_c": cos_c.astype(BF16), "sin_c": sin_c.astype(BF16)}
    consts = (cos_tab, sin_tab, _in_proj_tiles(), dft)

    xs = jnp.concatenate([x.reshape(n_batch * seq, d), ctx.reshape(n_batch * ctx_len, d)], axis=0)
    n_c_rows = 8
    c_rows = jnp.concatenate([c, c_ctx[None, :], jnp.zeros((n_c_rows - n_batch - 1, d), c.dtype)], axis=0)

    p = {"w_in": _prep_w_in(w_in), "w_out": w_out, "a_sink": a_sink, "b_lambda": b_lambda,
         "b_subln_g": b_subln_g, "c_q_norm_g": c_q_norm_g, "c_kv_norm_g": c_kv_norm_g,
         "c_w_uq": _prep_w_uq(c_w_uq), "c_w_ukv": c_w_ukv, "d_w": d_w}
    mod_all = _mod_vectors(c_rows, w_mod, b_mod).reshape(depth, n_c_rows, N_MOD, d)
    mods = [mod_all[l] for l in range(depth)]
    u = _modulate(xs, mods[0], rg, shift_idx=0)
    for l in range(depth):
        last = l == depth - 1
        lam_init = 0.8 - 0.6 * math.exp(-0.3 * l)
        mod = mods[l]
        y = _ffn(u, ffn1_w_gu, ffn1_w_d, l, rows=rg.n, tm=rg.tm)
        xs, u = _ln_res(xs, y, mod, ln_g[l, 0], ln_b[l, 0], rg, rows=rg.n, alpha=alpha,
                        gate_idx=2, gate_mul=0.5, next_mod=mod, next_shift_idx=3, name="ln_ffn1")
        y = _mixing(u, p, l, consts, rg, lam_init=lam_init, need_ctx=not last)
        rows = rg.nl if last else rg.n
        xs, u = _ln_res(xs, y, mod, ln_g[l, 1], ln_b[l, 1], rg, rows=rows, alpha=alpha,
                        gate_idx=5, gate_mul=1.0, next_mod=mod, next_shift_idx=6, name="ln_mix")
        y = _ffn(u, ffn2_w_gu, ffn2_w_d, l, rows=rows, tm=rg.tm)
        xs, u = _ln_res(xs, y, mod, ln_g[l, 2], ln_b[l, 2], rg, rows=rows, alpha=alpha,
                        gate_idx=8, gate_mul=0.5, next_mod=None if last else mods[l + 1],
                        next_shift_idx=None if last else 0, name="ln_ffn2")
    return xs[:rg.nl].reshape(n_batch, seq, d)
```

```python
import functools
import math

import numpy as np
import jax
import jax.numpy as jnp
from jax import lax
from jax.experimental import pallas as pl
from jax.experimental.pallas import tpu as pltpu

F32 = jnp.float32
BF16 = jnp.bfloat16

GRID_W = 64
HEAD_DIM = 128
WINDOW = 128
N_HEADS = 8
A_KV_HEADS = 2
A_GROUP = N_HEADS // A_KV_HEADS
B_QK_DIM = 64
C_Q_LORA = 1536
C_KV_LORA = 512
C_NOPE = 128
C_ROPE = 64
N_MOD = 9
ROPE_BASE = 10000.0
LN_EPS = 1e-5
RMS_EPS = 1e-6
NEG_INF = -1e30

H_CQ, H_AQ, H_BQ, H_DX = 0, 1536, 2560, 3584
H_AK, H_AV, H_BK, H_BV, H_CKV, H_CKR = 4608, 4864, 5120, 6144, 7168, 7680
H_COLS = 8192

ROPE_NONE, ROPE_A, ROPE_B, ROPE_KR, ROPE_QR = 0, 1, 2, 3, 4

VMEM_LIMIT_BYTES = 56 * 1024 * 1024
ROPE_TN = 256
PROJ_HALVES = 2
Q_SUB = 256
LOG2E = math.log2(math.e)


def _cparams(n_axes):
    return pltpu.CompilerParams(dimension_semantics=("arbitrary",) * n_axes,
                                vmem_limit_bytes=VMEM_LIMIT_BYTES)


def _dot(a, b):
    return jnp.dot(a, b, preferred_element_type=F32)


def _dot_nt(a, b):
    return lax.dot_general(a, b, (((1,), (1,)), ((), ())), preferred_element_type=F32)


def _mm_kernel(a_ref, b_ref, o_ref):
    @pl.when(pl.program_id(2) == 0)
    def _():
        o_ref[...] = jnp.zeros_like(o_ref)

    o_ref[...] += _dot(a_ref[...], b_ref[...].astype(BF16))


def _mm(a, b, layer, *, rows, tm, tn, tk, name):
    _, kdim, n = b.shape
    assert rows % tm == 0 and n % tn == 0 and kdim % tk == 0
    return pl.pallas_call(
        _mm_kernel,
        out_shape=jax.ShapeDtypeStruct((rows, n), F32),
        grid=(rows // tm, n // tn, kdim // tk),
        in_specs=[pl.BlockSpec((tm, tk), lambda i, j, k: (i, k)),
                  pl.BlockSpec((None, tk, tn), lambda i, j, k: (layer, k, j))],
        out_specs=pl.BlockSpec((tm, tn), lambda i, j, k: (i, j)),
        compiler_params=_cparams(3),
        name=name,
    )(a, b)


def _out_proj_kernel(*refs):
    *y_refs, w_ref, o_ref = refs
    width = y_refs[0].shape[1]
    acc = None
    for m, y_ref in enumerate(y_refs):
        part = _dot(y_ref[...], w_ref[m * width:(m + 1) * width, :].astype(BF16))
        acc = part if acc is None else acc + part
    o_ref[...] = acc.astype(o_ref.dtype)


def _out_proj(ys, w_out, layer, *, rows, tm, tn):
    _, kdim, n = w_out.shape
    width = ys[0].shape[1]
    assert kdim == width * len(ys)
    y_spec = pl.BlockSpec((tm, width), lambda i, j: (i, 0))
    return pl.pallas_call(
        _out_proj_kernel,
        out_shape=jax.ShapeDtypeStruct((rows, n), F32),
        grid=(rows // tm, n // tn),
        in_specs=[y_spec] * len(ys) + [pl.BlockSpec((None, kdim, tn), lambda i, j: (layer, 0, j))],
        out_specs=pl.BlockSpec((tm, tn), lambda i, j: (i, j)),
        compiler_params=_cparams(2),
        name="out_proj",
    )(*ys, w_out)


def _mod_kernel(c_ref, w_ref, b_ref, o_ref):
    c = c_ref[...]
    sc = (c * jax.nn.sigmoid(c)).astype(BF16)
    o_ref[...] = _dot(sc, w_ref[...].astype(BF16)) + b_ref[...]


def _mod_vectors(c_rows, w_mod, b_mod):
    depth, d, n = w_mod.shape
    r = c_rows.shape[0]
    tn = 1024
    return pl.pallas_call(
        _mod_kernel,
        out_shape=jax.ShapeDtypeStruct((depth, r, n), F32),
        grid=(depth, n // tn),
        in_specs=[pl.BlockSpec((r, d), lambda l, j: (0, 0)),
                  pl.BlockSpec((None, d, tn), lambda l, j: (l, 0, j)),
                  pl.BlockSpec((None, 1, tn), lambda l, j: (l, 0, j))],
        out_specs=pl.BlockSpec((None, r, tn), lambda l, j: (l, 0, j)),
        compiler_params=_cparams(2),
        name="mod_vectors",
    )(c_rows, w_mod, b_mod.reshape(depth, 1, n))


def _seg_index(i, n_lat_tiles, tiles_per_batch, n_batch):
    return jnp.where(i < n_lat_tiles, i // tiles_per_batch, n_batch)


def _modulate_kernel(x_ref, mod_ref, u_ref, *, shift_idx):
    shift = mod_ref[shift_idx:shift_idx + 1, :]
    scale = mod_ref[shift_idx + 1:shift_idx + 2, :]
    u_ref[...] = (x_ref[...] * (1.0 + scale) + shift).astype(u_ref.dtype)


def _ln_res_kernel(x_ref, y_ref, mod_ref, g_ref, b_ref, *rest, alpha, gate_idx, gate_mul, next_shift_idx):
    gate = mod_ref[gate_idx:gate_idx + 1, :]
    z = alpha * x_ref[...] + (gate_mul * gate) * y_ref[...].astype(F32)
    mu = jnp.mean(z, axis=-1, keepdims=True)
    zc = z - mu
    var = jnp.mean(zc * zc, axis=-1, keepdims=True)
    xn = zc * lax.rsqrt(var + LN_EPS) * g_ref[...] + b_ref[...]
    if next_shift_idx is None:
        xo_ref, = rest
        xo_ref[...] = xn
    else:
        nmod_ref, xo_ref, u_ref = rest
        xo_ref[...] = xn
        shift = nmod_ref[next_shift_idx:next_shift_idx + 1, :]
        scale = nmod_ref[next_shift_idx + 1:next_shift_idx + 2, :]
        u_ref[...] = (xn * (1.0 + scale) + shift).astype(u_ref.dtype)


class _Rows:
    def __init__(self, n_batch, seq, ctx_len):
        self.b, self.t, self.c = n_batch, seq, ctx_len
        self.nl, self.nc = n_batch * seq, n_batch * ctx_len
        self.n = self.nl + self.nc
        self.tr = math.gcd(256, math.gcd(seq, ctx_len))
        self.tm = math.gcd(1024, math.gcd(seq, self.nc))

    def mod_spec(self, tile, d):
        n_lat_tiles, per_batch, nb = self.nl // tile, self.t // tile, self.b
        return pl.BlockSpec((None, N_MOD, d),
                            lambda i, *_: (_seg_index(i, n_lat_tiles, per_batch, nb), 0, 0))


def _modulate(x, mod, rg, *, shift_idx):
    n, d = x.shape
    tr = rg.tr
    return pl.pallas_call(
        functools.partial(_modulate_kernel, shift_idx=shift_idx),
        out_shape=jax.ShapeDtypeStruct((n, d), BF16),
        grid=(n // tr,),
        in_specs=[pl.BlockSpec((tr, d), lambda i: (i, 0)), rg.mod_spec(tr, d)],
        out_specs=pl.BlockSpec((tr, d), lambda i: (i, 0)),
        compiler_params=_cparams(1),
        name="modulate",
    )(x, mod)


def _ln_res(x, y, mod, ln_g, ln_b, rg, *, rows, alpha, gate_idx, gate_mul, next_mod, next_shift_idx, name):
    d = x.shape[1]
    tr = rg.tr
    row_spec = pl.BlockSpec((tr, d), lambda i: (i, 0))
    vec_spec = pl.BlockSpec((1, d), lambda i: (0, 0))
    out_shape = [jax.ShapeDtypeStruct((rows, d), F32)]
    out_specs = [row_spec]
    in_specs = [row_spec, row_spec, rg.mod_spec(tr, d), vec_spec, vec_spec]
    args = [x, y, mod, ln_g.reshape(1, d), ln_b.reshape(1, d)]
    if next_shift_idx is not None:
        in_specs.append(rg.mod_spec(tr, d))
        args.append(next_mod)
        out_shape.append(jax.ShapeDtypeStruct((rows, d), BF16))
        out_specs.append(row_spec)
    res = pl.pallas_call(
        functools.partial(_ln_res_kernel, alpha=alpha, gate_idx=gate_idx, gate_mul=gate_mul,
                          next_shift_idx=next_shift_idx),
        out_shape=out_shape,
        grid=(rows // tr,),
        in_specs=in_specs,
        out_specs=out_specs,
        compiler_params=_cparams(1),
        name=name,
    )(*args)
    return (res[0], res[1]) if next_shift_idx is not None else (res[0], None)


def _ffn_up_kernel(a_ref, wg_ref, wu_ref, o_ref):
    a = a_ref[...]
    g = _dot(a, wg_ref[...].astype(BF16))
    up = _dot(a, wu_ref[...].astype(BF16))
    o_ref[...] = (g * jax.nn.sigmoid(g) * up).astype(o_ref.dtype)


def _ffn_up(u, w_gu, layer, *, rows, tm, tn):
    _, d, two_ff = w_gu.shape
    ff = two_ff // 2
    up0 = ff // tn
    return pl.pallas_call(
        _ffn_up_kernel,
        out_shape=jax.ShapeDtypeStruct((rows, ff), BF16),
        grid=(rows // tm, ff // tn),
        in_specs=[pl.BlockSpec((tm, d), lambda i, j: (i, 0)),
                  pl.BlockSpec((None, d, tn), lambda i, j: (layer, 0, j)),
                  pl.BlockSpec((None, d, tn), lambda i, j: (layer, 0, up0 + j))],
        out_specs=pl.BlockSpec((tm, tn), lambda i, j: (i, j)),
        compiler_params=_cparams(2),
        name="ffn_up",
    )(u, w_gu, w_gu)


def _rope_tile(x, cos, sin, half):
    lane = lax.broadcasted_iota(jnp.int32, x.shape, 1)
    first = (lane % (2 * half)) < half
    width = x.shape[1]
    partner = jnp.where(first, pltpu.roll(x, width - half, 1), pltpu.roll(x, half, 1))
    return x * cos + partner * sin


def _proj_kernel(kind_ref, src_ref, scale_ref, a_ref, b_ref, *rest, has_gain, n_lat_tiles):
    if has_gain:
        gain_ref, *tab_refs, o_ref, an_ref = rest
    else:
        *tab_refs, o_ref = rest
    i, j = pl.program_id(0), pl.program_id(1)

    if has_gain:
        @pl.when(j == 0)
        def _():
            x = a_ref[...].astype(F32)
            ms = jnp.mean(x * x, axis=-1, keepdims=True)
            an_ref[...] = (x * lax.rsqrt(ms + RMS_EPS) * gain_ref[...]).astype(BF16)
        a = an_ref[...]
    else:
        a = a_ref[...]

    for half in range(PROJ_HALVES):
        cos_ref, sin_ref = tab_refs[2 * half], tab_refs[2 * half + 1]
        cols = slice(half * ROPE_TN, (half + 1) * ROPE_TN)
        tile = PROJ_HALVES * j + half
        kind = jnp.where(i < n_lat_tiles, kind_ref[tile], ROPE_NONE)
        out_scale = scale_ref[tile]

        def acc():
            return _dot(a, b_ref[:, cols].astype(BF16)) * out_scale

        @pl.when(kind == ROPE_NONE)
        def _():
            o_ref[:, cols] = acc().astype(o_ref.dtype)

        @pl.when(kind == ROPE_A)
        def _():
            o_ref[:, cols] = _rope_tile(acc(), cos_ref[...], sin_ref[...], 32).astype(o_ref.dtype)

        @pl.when(kind > ROPE_A)
        def _():
            o_ref[:, cols] = _rope_tile(acc(), cos_ref[...], sin_ref[...], 16).astype(o_ref.dtype)


def _proj(a, w, layer, kinds, scales, src_tiles, cos_tab, sin_tab, rg, *, a_col0=0, gain=None, name):
    _, kdim, n = w.shape
    tm, tn = rg.tm, PROJ_HALVES * ROPE_TN
    assert a_col0 % kdim == 0 and n % tn == 0
    a_blk = a_col0 // kdim
    n_lat_tiles = rg.nl // tm
    t_tiles = rg.t // tm
    has_gain = gain is not None

    def tab_spec(half):
        def tab_map(i, j, kind_ref, src_ref):
            kind = kind_ref[PROJ_HALVES * j + half]
            return (jnp.where(i < n_lat_tiles, kind, ROPE_NONE), i % t_tiles, 0)
        return pl.BlockSpec((None, tm, ROPE_TN), tab_map)

    in_specs = [pl.BlockSpec(memory_space=pltpu.SMEM),
                pl.BlockSpec((tm, kdim), lambda i, j, kr, sr: (i, a_blk)),
                pl.BlockSpec((None, kdim, tn), lambda i, j, kr, sr: (layer, 0, sr[j]))]
    args = [scales, a, w]
    scratch = []
    if has_gain:
        in_specs.append(pl.BlockSpec((None, 1, kdim), lambda i, j, kr, sr: (layer, 0, 0)))
        args.append(gain.reshape(gain.shape[0], 1, kdim))
        scratch.append(pltpu.VMEM((tm, kdim), BF16))
    for half in range(PROJ_HALVES):
        in_specs += [tab_spec(half), tab_spec(half)]
        args += [cos_tab, sin_tab]
    return pl.pallas_call(
        functools.partial(_proj_kernel, has_gain=has_gain, n_lat_tiles=n_lat_tiles),
        out_shape=jax.ShapeDtypeStruct((rg.n, n), BF16),
        grid_spec=pltpu.PrefetchScalarGridSpec(
            num_scalar_prefetch=2,
            grid=(rg.n // tm, n // tn),
            in_specs=in_specs,
            out_specs=pl.BlockSpec((tm, tn), lambda i, j, kr, sr: (i, j)),
            scratch_shapes=scratch),
        compiler_params=_cparams(2),
        name=name,
    )(kinds, src_tiles, *args)


def _rope_tables(seq):
    t = np.arange(seq)
    row, col = (t // GRID_W).astype(np.float64), (t % GRID_W).astype(np.float64)

    def pattern(rot_dim):
        axis_dim = rot_dim // 2
        inv = ROPE_BASE ** (-np.arange(0, axis_dim, 2, dtype=np.float64) / axis_dim)
        ar, ac = row[:, None] * inv[None, :], col[:, None] * inv[None, :]
        cos = np.concatenate([np.cos(ar), np.cos(ar), np.cos(ac), np.cos(ac)], axis=1)
        sin = np.concatenate([-np.sin(ar), np.sin(ar), -np.sin(ac), np.sin(ac)], axis=1)
        return cos, sin

    cos = np.ones((5, seq, ROPE_TN), np.float64)
    sin = np.zeros((5, seq, ROPE_TN), np.float64)
    c128, s128 = pattern(HEAD_DIM)
    c64, s64 = pattern(2 * 32)
    cos[ROPE_A], sin[ROPE_A] = np.tile(c128, (1, 2)), np.tile(s128, (1, 2))
    cos[ROPE_B], sin[ROPE_B] = np.tile(c64, (1, 4)), np.tile(s64, (1, 4))
    cos[ROPE_KR, :, :64], sin[ROPE_KR, :, :64] = c64, s64
    cos[ROPE_QR, :, 128:192], sin[ROPE_QR, :, 128:192] = c64, s64
    return jnp.asarray(cos, F32), jnp.asarray(sin, F32)


def _mixer_call(body, rg, *, latent, dst, grid, in_specs, args, tq, n_q, out_block_w, name):
    row0 = 0 if latent else rg.nl // tq
    aliases = {}
    if not latent:
        in_specs = in_specs + [pl.BlockSpec(memory_space=pl.ANY)]
        args = args + [dst]
        aliases = {len(args) - 1: 0}
    return pl.pallas_call(
        body,
        out_shape=jax.ShapeDtypeStruct((rg.n, N_HEADS * HEAD_DIM), BF16),
        grid=grid,
        in_specs=in_specs,
        out_specs=pl.BlockSpec((tq, out_block_w), lambda b, hh, qi: (row0 + b * n_q + qi, hh)),
        input_output_aliases=aliases,
        compiler_params=_cparams(3),
        name=name,
    )(*args)


def _attn_win_kernel(sink_ref, q_ref, kc_ref, vc_ref, *rest, has_lat, tq, seq, layer):
    if has_lat:
        kl_ref, vl_ref, o_ref = rest
    else:
        _, o_ref = rest
    g, qi = pl.program_id(1), pl.program_id(2)
    kc, vc = kc_ref[...], vc_ref[...]
    if has_lat:
        win = min(seq, tq + 2 * WINDOW)
        ws = pl.multiple_of(jnp.clip(qi * tq - WINDOW, 0, seq - win), WINDOW)
        kw, vw = kl_ref[pl.ds(ws, win), :], vl_ref[pl.ds(ws, win), :]
        qpos = qi * tq + lax.broadcasted_iota(jnp.int32, (tq, win), 0)
        kpos = ws + lax.broadcasted_iota(jnp.int32, (tq, win), 1)
        valid = jnp.abs(qpos - kpos) <= WINDOW
    for j in range(A_GROUP):
        q = q_ref[:, j * HEAD_DIM:(j + 1) * HEAD_DIM]
        sink = sink_ref[layer, g * A_GROUP + j] * LOG2E
        s_c = _dot_nt(q, kc)
        m = jnp.maximum(jnp.max(s_c, axis=-1, keepdims=True), sink)
        if has_lat:
            s_l = jnp.where(valid, _dot_nt(q, kw), NEG_INF)
            m = jnp.maximum(m, jnp.max(s_l, axis=-1, keepdims=True))
        p_c = jnp.exp2(s_c - m)
        den = jnp.sum(p_c, axis=-1, keepdims=True) + jnp.exp2(sink - m)
        o = _dot(p_c.astype(BF16), vc)
        if has_lat:
            p_l = jnp.exp2(s_l - m)
            den = den + jnp.sum(p_l, axis=-1, keepdims=True)
            o = o + _dot(p_l.astype(BF16), vw)
        o_ref[:, j * HEAD_DIM:(j + 1) * HEAD_DIM] = (o / den).astype(o_ref.dtype)


def _attn_win(h, a_sink, layer, rg, *, latent, dst=None):
    gw = A_GROUP * HEAD_DIM
    tq, n_q, q0 = _q_geometry(rg, latent, 256)
    c_blk0 = rg.nl // rg.c
    in_specs = [pl.BlockSpec(memory_space=pltpu.SMEM),
                pl.BlockSpec((tq, gw), lambda b, g, qi: (q0 + b * n_q + qi, H_AQ // gw + g)),
                pl.BlockSpec((rg.c, HEAD_DIM), lambda b, g, qi: (c_blk0 + b, H_AK // HEAD_DIM + g)),
                pl.BlockSpec((rg.c, HEAD_DIM), lambda b, g, qi: (c_blk0 + b, H_AV // HEAD_DIM + g))]
    args = [a_sink, h, h, h]
    if latent:
        in_specs += [pl.BlockSpec((rg.t, HEAD_DIM), lambda b, g, qi: (b, H_AK // HEAD_DIM + g)),
                     pl.BlockSpec((rg.t, HEAD_DIM), lambda b, g, qi: (b, H_AV // HEAD_DIM + g))]
        args += [h, h]
    return _mixer_call(functools.partial(_attn_win_kernel, has_lat=latent, tq=tq, seq=rg.t, layer=layer), rg,
                       latent=latent, dst=dst, grid=(rg.b, A_KV_HEADS, n_q), in_specs=in_specs, args=args,
                       tq=tq, n_q=n_q, out_block_w=gw, name="attn_win_lat" if latent else "attn_win_ctx")


def _softmax_pv(q, keys, values):
    scores = [_dot_nt(q, k) for k in keys]
    m = functools.reduce(jnp.maximum, [jnp.max(s, axis=-1, keepdims=True) for s in scores])
    den, out = None, None
    for s, v in zip(scores, values):
        p = jnp.exp2(s - m)
        d = jnp.sum(p, axis=-1, keepdims=True)
        o = _dot(p.astype(BF16), v)
        den = d if den is None else den + d
        out = o if out is None else out + o
    return out / den


def _attn_diff_kernel(lam_ref, gsub_ref, q_ref, kc_ref, vc_ref, *rest, has_lat, lam_init):
    if has_lat:
        kl_ref, vl_ref, o_ref = rest
    else:
        _, o_ref = rest
    lf = lam_ref[...]
    lam = (jnp.exp(jnp.sum(lf[0:1] * lf[1:2], axis=-1, keepdims=True))
           - jnp.exp(jnp.sum(lf[2:3] * lf[3:4], axis=-1, keepdims=True)) + lam_init)
    keys = [kc_ref[...]] + ([kl_ref[...]] if has_lat else [])
    values = [vc_ref[...]] + ([vl_ref[...]] if has_lat else [])
    tq = q_ref.shape[0]
    sub = min(tq, Q_SUB)
    for r in range(tq // sub):
        q = q_ref[r * sub:(r + 1) * sub, :]
        first_map = lax.broadcasted_iota(jnp.int32, q.shape, 1) < B_QK_DIM
        zero = jnp.zeros_like(q)

        o = (_softmax_pv(jnp.where(first_map, q, zero), keys, values)
             - lam * _softmax_pv(jnp.where(first_map, zero, q), keys, values))
        ms = jnp.mean(o * o, axis=-1, keepdims=True)
        o = o * lax.rsqrt(ms + RMS_EPS) * gsub_ref[...] * (1.0 - lam_init)
        o_ref[r * sub:(r + 1) * sub, :] = o.astype(o_ref.dtype)


def _kv_specs(rg, col_k, col_v, latent, k_width=HEAD_DIM):
    c_blk0 = rg.nl // rg.c
    specs = [pl.BlockSpec((rg.c, k_width), lambda b, hh, qi: (c_blk0 + b, col_k(hh))),
             pl.BlockSpec((rg.c, HEAD_DIM), lambda b, hh, qi: (c_blk0 + b, col_v(hh)))]
    if latent:
        specs += [pl.BlockSpec((rg.t, k_width), lambda b, hh, qi: (b, col_k(hh))),
                  pl.BlockSpec((rg.t, HEAD_DIM), lambda b, hh, qi: (b, col_v(hh)))]
    return specs


def _q_geometry(rg, latent, tq_lat):
    if latent:
        tq = min(tq_lat, rg.t)
        return tq, rg.t // tq, 0
    return rg.c, 1, rg.nl // rg.c


def _attn_diff(h, b_lambda, b_subln_g, layer, rg, *, latent, lam_init, dst=None):
    tq, n_q, q0 = _q_geometry(rg, latent, 2048)
    in_specs = [pl.BlockSpec((None, 4, B_QK_DIM), lambda b, hh, qi: (layer, 0, 0)),
                pl.BlockSpec((None, 1, HEAD_DIM), lambda b, hh, qi: (layer, 0, 0)),
                pl.BlockSpec((tq, HEAD_DIM), lambda b, hh, qi: (q0 + b * n_q + qi, H_BQ // HEAD_DIM + hh))]
    in_specs += _kv_specs(rg, lambda hh: H_BK // HEAD_DIM + hh, lambda hh: H_BV // HEAD_DIM + hh, latent)
    n_kv = 4 if latent else 2
    args = [b_lambda, b_subln_g.reshape(b_subln_g.shape[0], 1, HEAD_DIM), h] + [h] * n_kv
    return _mixer_call(functools.partial(_attn_diff_kernel, has_lat=latent, lam_init=lam_init), rg,
                       latent=latent, dst=dst, grid=(rg.b, N_HEADS, n_q), in_specs=in_specs, args=args,
                       tq=tq, n_q=n_q, out_block_w=HEAD_DIM, name="attn_diff_lat" if latent else "attn_diff_ctx")


def _attn_mla_kernel(q_ref, knc_ref, vc_ref, krc_ref, *rest, has_lat):
    if has_lat:
        knl_ref, vl_ref, krl_ref, o_ref = rest
    else:
        _, o_ref = rest
    segs = [(knc_ref, krc_ref, vc_ref)] + ([(knl_ref, krl_ref, vl_ref)] if has_lat else [])
    keys = [jnp.concatenate([kn[...], kr[...]], axis=1) for kn, kr, _ in segs]
    values = [v[...] for _, _, v in segs]
    tq = q_ref.shape[0]
    sub = min(tq, Q_SUB)
    for r in range(tq // sub):
        rows = slice(r * sub, (r + 1) * sub)
        o_ref[rows, :] = _softmax_pv(q_ref[rows, :], keys, values).astype(o_ref.dtype)


def _attn_mla(qc, kv2, h, rg, *, latent, dst=None):
    tq, n_q, q0 = _q_geometry(rg, latent, 2048)
    c_blk0 = rg.nl // rg.c
    kr_col = H_CKR // HEAD_DIM
    kv = _kv_specs(rg, lambda hh: 2 * hh, lambda hh: 2 * hh + 1, latent)
    in_specs = [pl.BlockSpec((tq, 2 * HEAD_DIM), lambda b, hh, qi: (q0 + b * n_q + qi, hh)),
                kv[0], kv[1],
                pl.BlockSpec((rg.c, HEAD_DIM), lambda b, hh, qi: (c_blk0 + b, kr_col))]
    args = [qc, kv2, kv2, h]
    if latent:
        in_specs += [kv[2], kv[3], pl.BlockSpec((rg.t, HEAD_DIM), lambda b, hh, qi: (b, kr_col))]
        args += [kv2, kv2, h]
    return _mixer_call(functools.partial(_attn_mla_kernel, has_lat=latent), rg,
                       latent=latent, dst=dst, grid=(rg.b, N_HEADS, n_q), in_specs=in_specs, args=args,
                       tq=tq, n_q=n_q, out_block_w=HEAD_DIM, name="attn_mla_lat" if latent else "attn_mla_ctx")


def _dft_tables(n):
    idx = np.arange(n, dtype=np.int64)
    ang = (np.outer(idx, idx) % n).astype(np.float64) * (2.0 * np.pi / n)
    return jnp.asarray(np.cos(ang), F32), jnp.asarray(np.sin(ang), F32)


def _fourier_chan_kernel(z_ref, cc_ref, sc_ref, dw_ref, zc_ref, zs_ref):
    dw = dw_ref[...].astype(BF16)
    z = z_ref[...]
    zc_ref[...] = _dot(z, _dot(cc_ref[...], dw).astype(BF16)).astype(zc_ref.dtype)
    zs_ref[...] = _dot(z, _dot(sc_ref[...], dw).astype(BF16)).astype(zs_ref.dtype)


def _fourier_chan(h, cos_c, sin_c, d_w, layer, rg):
    tm = rg.tm
    col0 = H_DX // HEAD_DIM
    tile = pl.BlockSpec((tm, HEAD_DIM), lambda i, g: (i, g))
    const = pl.BlockSpec((HEAD_DIM, HEAD_DIM), lambda i, g: (0, 0))
    out = jax.ShapeDtypeStruct((rg.n, N_HEADS * HEAD_DIM), BF16)
    return pl.pallas_call(
        _fourier_chan_kernel,
        out_shape=[out, out],
        grid=(rg.n // tm, N_HEADS),
        in_specs=[pl.BlockSpec((tm, HEAD_DIM), lambda i, g: (i, col0 + g)), const, const,
                  pl.BlockSpec((None, None, HEAD_DIM, HEAD_DIM), lambda i, g: (layer, g, 0, 0))],
        out_specs=[tile, tile],
        compiler_params=_cparams(2),
        name="fourier_chan",
    )(h, cos_c, sin_c, d_w)


def _fourier_seq_kernel(ct_ref, st_ref, zc_ref, zs_ref, *rest, norm):
    o_ref = rest[-1]
    o = _dot(ct_ref[...], zc_ref[...]) - _dot(st_ref[...], zs_ref[...])
    o_ref[...] = (o * norm).astype(o_ref.dtype)


def _fourier_seq(cos_t, sin_t, zc, zs, rg, *, latent, dst=None):
    length = rg.t if latent else rg.c
    tm = min(512, length)
    n_i = length // tm
    z_blk0 = 0 if latent else rg.nl // rg.c
    row0 = 0 if latent else rg.nl // tm
    width = zc.shape[1]
    dft_spec = pl.BlockSpec((tm, length), lambda b, i: (i, 0))
    z_spec = pl.BlockSpec((length, width), lambda b, i: (z_blk0 + b, 0))
    in_specs, args, aliases = [dft_spec, dft_spec, z_spec, z_spec], [cos_t, sin_t, zc, zs], {}
    if not latent:
        in_specs.append(pl.BlockSpec(memory_space=pl.ANY))
        args.append(dst)
        aliases = {len(args) - 1: 0}
    return pl.pallas_call(
        functools.partial(_fourier_seq_kernel, norm=(length * HEAD_DIM) ** -0.5),
        out_shape=jax.ShapeDtypeStruct((rg.n, width), BF16),
        grid=(rg.b, n_i),
        in_specs=in_specs,
        out_specs=pl.BlockSpec((tm, width), lambda b, i: (row0 + b * n_i + i, 0)),
        input_output_aliases=aliases,
        compiler_params=_cparams(2),
        name="fourier_seq_lat" if latent else "fourier_seq_ctx",
    )(*args)


def _prep_w_in(w_in):
    return jnp.pad(w_in, ((0, 0), (0, 0), (0, H_COLS - w_in.shape[2]))).astype(BF16)


def _prep_w_uq(w_uq):
    depth, k, _ = w_uq.shape
    w = w_uq.reshape(depth, k, N_HEADS, C_NOPE + C_ROPE)
    w = jnp.pad(w, ((0, 0), (0, 0), (0, 0), (0, 2 * HEAD_DIM - C_NOPE - C_ROPE)))
    return w.reshape(depth, k, N_HEADS * 2 * HEAD_DIM).astype(BF16)


def _in_proj_tiles():
    step = PROJ_HALVES * ROPE_TN
    src_cols = {"aq": 0, "bq": 1024, "cq": 2048, "dx": 3584, "ak": 4608, "bk": 5120, "bv": 6144,
                "ckv": 7168, "ckr": 7680}
    layout = [("cq", H_CQ, H_AQ, (ROPE_NONE, ROPE_NONE), 1.0),
              ("aq", H_AQ, H_BQ, (ROPE_A, ROPE_A), HEAD_DIM ** -0.5 * LOG2E),
              ("bq", H_BQ, H_DX, (ROPE_B, ROPE_B), B_QK_DIM ** -0.5 * LOG2E),
              ("dx", H_DX, H_AK, (ROPE_NONE, ROPE_NONE), 1.0),
              ("ak", H_AK, H_BK, (ROPE_A, ROPE_NONE), 1.0),
              ("bk", H_BK, H_BV, (ROPE_B, ROPE_B), 1.0),
              ("bv", H_BV, H_CKV, (ROPE_NONE, ROPE_NONE), 1.0),
              ("ckv", H_CKV, H_CKR, (ROPE_NONE, ROPE_NONE), 1.0),
              ("ckr", H_CKR, H_COLS, (ROPE_KR, ROPE_NONE), 1.0)]
    kinds = np.zeros((H_COLS // ROPE_TN,), np.int32)
    scales = np.ones((H_COLS // ROPE_TN,), np.float32)
    src = np.zeros((H_COLS // step,), np.int32)
    for name, lo, hi, kind, scale in layout:
        assert lo % step == 0 and hi % step == 0 and src_cols[name] % step == 0
        n = (hi - lo) // step
        kinds[lo // ROPE_TN:hi // ROPE_TN] = np.tile(np.asarray(kind, np.int32), n)
        scales[lo // ROPE_TN:hi // ROPE_TN] = scale
        src[lo // step:hi // step] = src_cols[name] // step + np.arange(n)
    return jnp.asarray(kinds), jnp.asarray(scales), jnp.asarray(src)


def _ffn(u, w_gu, w_d, layer, *, rows, tm):
    hid = _ffn_up(u, w_gu, layer, rows=rows, tm=tm, tn=256)
    return _mm(hid, w_d, layer, rows=rows, tm=tm, tn=1024, tk=2048, name="ffn_down")


def _mixing(u, p, layer, consts, rg, *, lam_init, need_ctx):
    cos_tab, sin_tab, (kinds_in, scales_in, src_in), dft = consts
    h = _proj(u, p["w_in"], layer, kinds_in, scales_in, src_in, cos_tab, sin_tab, rg, name="in_proj")
    n_tiles = N_HEADS * 2 * HEAD_DIM // ROPE_TN
    ident = jnp.arange(n_tiles // PROJ_HALVES, dtype=jnp.int32)
    q_scale = jnp.full((n_tiles,), (C_NOPE + C_ROPE) ** -0.5 * LOG2E, F32)
    qc = _proj(h, p["c_w_uq"], layer, jnp.full((n_tiles,), ROPE_QR, jnp.int32), q_scale, ident, cos_tab, sin_tab,
               rg, a_col0=H_CQ, gain=p["c_q_norm_g"], name="mla_q_up")
    kv2 = _proj(h, p["c_w_ukv"], layer, jnp.zeros((n_tiles,), jnp.int32), jnp.ones((n_tiles,), F32), ident,
                cos_tab, sin_tab, rg, a_col0=H_CKV, gain=p["c_kv_norm_g"], name="mla_kv_up")
    zc, zs = _fourier_chan(h, dft["cos_c"], dft["sin_c"], p["d_w"], layer, rg)

    ya = _attn_win(h, p["a_sink"], layer, rg, latent=True)
    yb = _attn_diff(h, p["b_lambda"], p["b_subln_g"], layer, rg, latent=True, lam_init=lam_init)
    yc = _attn_mla(qc, kv2, h, rg, latent=True)
    yd = _fourier_seq(dft["cos_t"], dft["sin_t"], zc, zs, rg, latent=True)
    if need_ctx:
        ya = _attn_win(h, p["a_sink"], layer, rg, latent=False, dst=ya)
        yb = _attn_diff(h, p["b_lambda"], p["b_subln_g"], layer, rg, latent=False, lam_init=lam_init, dst=yb)
        yc = _attn_mla(qc, kv2, h, rg, latent=False, dst=yc)
        yd = _fourier_seq(dft["cos_x"], dft["sin_x"], zc, zs, rg, latent=False, dst=yd)
    rows = rg.n if need_ctx else rg.nl
    return _out_proj([ya, yb, yc, yd], p["w_out"], layer, rows=rows, tm=rg.tm, tn=512)


def kernel(x, c, ctx, c_ctx, w_mod, b_mod, ffn1_w_gu, ffn1_w_d, ffn2_w_gu, ffn2_w_d, ln_g, ln_b, w_in, w_out,
           a_sink, b_lambda, b_subln_g, c_q_norm_g, c_kv_norm_g, c_w_uq, c_w_ukv, d_w):
    n_batch, seq, d = x.shape
    ctx_len = ctx.shape[1]
    depth = w_mod.shape[0]
    rg = _Rows(n_batch, seq, ctx_len)
    alpha = (2.0 * depth) ** 0.25

    cos_tab, sin_tab = _rope_tables(seq)
    cos_t, sin_t = _dft_tables(seq)
    cos_x, sin_x = _dft_tables(ctx_len)
    cos_c, sin_c = _dft_tables(HEAD_DIM)
    dft = {"cos_t": cos_t.astype(BF16), "sin_t": sin_t.astype(BF16),
           "cos_x": cos_x.astype(BF16), "sin_x": sin_x.astype(BF16),
           "cos_c": cos_c.astype(BF16), "sin_c": sin_c.astype(BF16)}
    consts = (cos_tab, sin_tab, _in_proj_tiles(), dft)

    xs = jnp.concatenate([x.reshape(n_batch * seq, d), ctx.reshape(n_batch * ctx_len, d)], axis=0)
    n_c_rows = 8
    c_rows = jnp.concatenate([c, c_ctx[None, :], jnp.zeros((n_c_rows - n_batch - 1, d), c.dtype)], axis=0)

    p = {"w_in": _prep_w_in(w_in), "w_out": w_out, "a_sink": a_sink, "b_lambda": b_lambda,
         "b_subln_g": b_subln_g, "c_q_norm_g": c_q_norm_g, "c_kv_norm_g": c_kv_norm_g,
         "c_w_uq": _prep_w_uq(c_w_uq), "c_w_ukv": c_w_ukv, "d_w": d_w}
    mod_all = _mod_vectors(c_rows, w_mod, b_mod).reshape(depth, n_c_rows, N_MOD, d)
    mods = [mod_all[l] for l in range(depth)]
    u = _modulate(xs, mods[0], rg, shift_idx=0)
    for l in range(depth):
        last = l == depth - 1
        lam_init = 0.8 - 0.6 * math.exp(-0.3 * l)
        mod = mods[l]
        y = _ffn(u, ffn1_w_gu, ffn1_w_d, l, rows=rg.n, tm=rg.tm)
        xs, u = _ln_res(xs, y, mod, ln_g[l, 0], ln_b[l, 0], rg, rows=rg.n, alpha=alpha,
                        gate_idx=2, gate_mul=0.5, next_mod=mod, next_shift_idx=3, name="ln_ffn1")
        y = _mixing(u, p, l, consts, rg, lam_init=lam_init, need_ctx=not last)
        rows = rg.nl if last else rg.n
        xs, u = _ln_res(xs, y, mod, ln_g[l, 1], ln_b[l, 1], rg, rows=rows, alpha=alpha,
                        gate_idx=5, gate_mul=1.0, next_mod=mod, next_shift_idx=6, name="ln_mix")
        y = _ffn(u, ffn2_w_gu, ffn2_w_d, l, rows=rows, tm=rg.tm)
        xs, u = _ln_res(xs, y, mod, ln_g[l, 2], ln_b[l, 2], rg, rows=rows, alpha=alpha,
                        gate_idx=8, gate_mul=0.5, next_mod=None if last else mods[l + 1],
                        next_shift_idx=None if last else 0, name="ln_ffn2")
    return xs[:rg.nl].reshape(n_batch, seq, d)
```

```python
import functools
import math

import numpy as np
import jax
import jax.numpy as jnp
from jax import lax
from jax.experimental import pallas as pl
from jax.experimental.pallas import tpu as pltpu

F32 = jnp.float32
BF16 = jnp.bfloat16

GRID_W = 64
HEAD_DIM = 128
WINDOW = 128
N_HEADS = 8
A_KV_HEADS = 2
A_GROUP = N_HEADS // A_KV_HEADS
B_QK_DIM = 64
C_Q_LORA = 1536
C_KV_LORA = 512
C_NOPE = 128
C_ROPE = 64
N_MOD = 9
ROPE_BASE = 10000.0
LN_EPS = 1e-5
RMS_EPS = 1e-6
NEG_INF = -1e30

H_CQ, H_AQ, H_BQ, H_DX = 0, 1536, 2560, 3584
H_AK, H_AV, H_BK, H_BV, H_CKV, H_CKR = 4608, 4864, 5120, 6144, 7168, 7680
H_COLS = 8192

ROPE_NONE, ROPE_A, ROPE_B, ROPE_KR, ROPE_QR = 0, 1, 2, 3, 4
ROPE_HALF_LANES = {ROPE_A: HEAD_DIM // 4, ROPE_B: B_QK_DIM // 4, ROPE_KR: C_ROPE // 4, ROPE_QR: C_ROPE // 4}

VMEM_LIMIT_BYTES = 56 * 1024 * 1024
ROPE_TN = 256
PROJ_HALVES = 2
Q_SUB = 256
LOG2E = math.log2(math.e)


def _cparams(n_axes):
    return pltpu.CompilerParams(dimension_semantics=("arbitrary",) * n_axes,
                                vmem_limit_bytes=VMEM_LIMIT_BYTES)


def _dot(a, b):
    return jnp.dot(a, b, preferred_element_type=F32)


def _dot_nt(a, b):
    return lax.dot_general(a, b, (((1,), (1,)), ((), ())), preferred_element_type=F32)


def _mm_kernel(a_ref, b_ref, o_ref, acc_ref, *, nk):
    k = pl.program_id(2)

    @pl.when(k == 0)
    def _():
        acc_ref[...] = jnp.zeros_like(acc_ref)

    acc_ref[...] += _dot(a_ref[...], b_ref[...].astype(BF16))

    @pl.when(k == nk - 1)
    def _():
        o_ref[...] = acc_ref[...].astype(o_ref.dtype)


def _mm(a, b, layer, *, rows, tm, tn, tk, name):
    _, kdim, n = b.shape
    assert rows % tm == 0 and n % tn == 0 and kdim % tk == 0
    return pl.pallas_call(
        functools.partial(_mm_kernel, nk=kdim // tk),
        out_shape=jax.ShapeDtypeStruct((rows, n), BF16),
        grid=(rows // tm, n // tn, kdim // tk),
        in_specs=[pl.BlockSpec((tm, tk), lambda i, j, k: (i, k)),
                  pl.BlockSpec((None, tk, tn), lambda i, j, k: (layer, k, j))],
        out_specs=pl.BlockSpec((tm, tn), lambda i, j, k: (i, j)),
        scratch_shapes=[pltpu.VMEM((tm, tn), F32)],
        compiler_params=_cparams(3),
        name=name,
    )(a, b)


def _out_proj_kernel(*refs):
    *y_refs, w_ref, o_ref = refs
    width = y_refs[0].shape[1]
    acc = None
    for m, y_ref in enumerate(y_refs):
        part = _dot(y_ref[...], w_ref[m * width:(m + 1) * width, :].astype(BF16))
        acc = part if acc is None else acc + part
    o_ref[...] = acc.astype(o_ref.dtype)


def _out_proj(ys, w_out, layer, *, rows, tm, tn):
    _, kdim, n = w_out.shape
    width = ys[0].shape[1]
    assert kdim == width * len(ys)
    y_spec = pl.BlockSpec((tm, width), lambda i, j: (i, 0))
    return pl.pallas_call(
        _out_proj_kernel,
        out_shape=jax.ShapeDtypeStruct((rows, n), BF16),
        grid=(rows // tm, n // tn),
        in_specs=[y_spec] * len(ys) + [pl.BlockSpec((None, kdim, tn), lambda i, j: (layer, 0, j))],
        out_specs=pl.BlockSpec((tm, tn), lambda i, j: (i, j)),
        compiler_params=_cparams(2),
        name="out_proj",
    )(*ys, w_out)


def _mod_kernel(c_ref, w_ref, b_ref, o_ref):
    c = c_ref[...]
    sc = (c * jax.nn.sigmoid(c)).astype(BF16)
    o_ref[...] = _dot(sc, w_ref[...].astype(BF16)) + b_ref[...]


def _mod_vectors(c_rows, w_mod, b_mod):
    depth, d, n = w_mod.shape
    r = c_rows.shape[0]
    tn = 1024
    return pl.pallas_call(
        _mod_kernel,
        out_shape=jax.ShapeDtypeStruct((depth, r, n), F32),
        grid=(depth, n // tn),
        in_specs=[pl.BlockSpec((r, d), lambda l, j: (0, 0)),
                  pl.BlockSpec((None, d, tn), lambda l, j: (l, 0, j)),
                  pl.BlockSpec((None, 1, tn), lambda l, j: (l, 0, j))],
        out_specs=pl.BlockSpec((None, r, tn), lambda l, j: (l, 0, j)),
        compiler_params=_cparams(2),
        name="mod_vectors",
    )(c_rows, w_mod, b_mod.reshape(depth, 1, n))


def _seg_index(i, n_lat_tiles, tiles_per_batch, n_batch):
    return jnp.where(i < n_lat_tiles, i // tiles_per_batch, n_batch)


def _modulate_kernel(x_ref, c_ref, mod_ref, xs_ref, u_ref, *, shift_idx, n_lat_tiles):
    shift = mod_ref[shift_idx:shift_idx + 1, :]
    scale = mod_ref[shift_idx + 1:shift_idx + 2, :]

    def emit(src_ref):
        x = src_ref[...]
        xs_ref[...] = x
        u_ref[...] = (x * (1.0 + scale) + shift).astype(u_ref.dtype)

    @pl.when(pl.program_id(0) < n_lat_tiles)
    def _():
        emit(x_ref)

    @pl.when(pl.program_id(0) >= n_lat_tiles)
    def _():
        emit(c_ref)


def _ln_res_kernel(x_ref, y_ref, mod_ref, g_ref, b_ref, *rest, alpha, gate_idx, gate_mul, next_shift_idx):
    gate = mod_ref[gate_idx:gate_idx + 1, :]
    z = alpha * x_ref[...] + (gate_mul * gate) * y_ref[...].astype(F32)
    mu = jnp.mean(z, axis=-1, keepdims=True)
    zc = z - mu
    var = jnp.mean(zc * zc, axis=-1, keepdims=True)
    xn = zc * lax.rsqrt(var + LN_EPS) * g_ref[...] + b_ref[...]
    if next_shift_idx is None:
        xo_ref, = rest
        xo_ref[...] = xn
    else:
        nmod_ref, xo_ref, u_ref = rest
        xo_ref[...] = xn
        shift = nmod_ref[next_shift_idx:next_shift_idx + 1, :]
        scale = nmod_ref[next_shift_idx + 1:next_shift_idx + 2, :]
        u_ref[...] = (xn * (1.0 + scale) + shift).astype(u_ref.dtype)


class _Rows:
    def __init__(self, n_batch, seq, ctx_len):
        self.b, self.t, self.c = n_batch, seq, ctx_len
        self.nl, self.nc = n_batch * seq, n_batch * ctx_len
        self.n = self.nl + self.nc
        self.tr = math.gcd(256, math.gcd(seq, ctx_len))
        self.tm = math.gcd(1024, math.gcd(seq, self.nc))

    def mod_spec(self, tile, d):
        n_lat_tiles, per_batch, nb = self.nl // tile, self.t // tile, self.b
        return pl.BlockSpec((None, N_MOD, d),
                            lambda i, *_: (_seg_index(i, n_lat_tiles, per_batch, nb), 0, 0))


def _modulate(x, ctx, mod, rg, *, shift_idx):
    d = x.shape[1]
    tr = rg.tr
    n_lat_tiles, n_ctx_tiles = rg.nl // tr, rg.nc // tr
    row_spec = pl.BlockSpec((tr, d), lambda i: (i, 0))
    x_spec = pl.BlockSpec((tr, d), lambda i: (jnp.minimum(i, n_lat_tiles - 1), 0))
    c_spec = pl.BlockSpec((tr, d), lambda i: (jnp.clip(i - n_lat_tiles, 0, n_ctx_tiles - 1), 0))
    return pl.pallas_call(
        functools.partial(_modulate_kernel, shift_idx=shift_idx, n_lat_tiles=n_lat_tiles),
        out_shape=[jax.ShapeDtypeStruct((rg.n, d), F32), jax.ShapeDtypeStruct((rg.n, d), BF16)],
        grid=(rg.n // tr,),
        in_specs=[x_spec, c_spec, rg.mod_spec(tr, d)],
        out_specs=[row_spec, row_spec],
        compiler_params=_cparams(1),
        name="modulate",
    )(x, ctx, mod)


def _ln_res(x, y, mod, ln_g, ln_b, rg, *, rows, alpha, gate_idx, gate_mul, next_mod, next_shift_idx, name):
    d = x.shape[1]
    tr = rg.tr
    row_spec = pl.BlockSpec((tr, d), lambda i: (i, 0))
    vec_spec = pl.BlockSpec((1, d), lambda i: (0, 0))
    out_shape = [jax.ShapeDtypeStruct((rows, d), F32)]
    out_specs = [row_spec]
    in_specs = [row_spec, row_spec, rg.mod_spec(tr, d), vec_spec, vec_spec]
    args = [x, y, mod, ln_g.reshape(1, d), ln_b.reshape(1, d)]
    if next_shift_idx is not None:
        in_specs.append(rg.mod_spec(tr, d))
        args.append(next_mod)
        out_shape.append(jax.ShapeDtypeStruct((rows, d), BF16))
        out_specs.append(row_spec)
    res = pl.pallas_call(
        functools.partial(_ln_res_kernel, alpha=alpha, gate_idx=gate_idx, gate_mul=gate_mul,
                          next_shift_idx=next_shift_idx),
        out_shape=out_shape,
        grid=(rows // tr,),
        in_specs=in_specs,
        out_specs=out_specs,
        compiler_params=_cparams(1),
        name=name,
    )(*args)
    return (res[0], res[1]) if next_shift_idx is not None else (res[0], None)


def _ffn_up_kernel(a_ref, wg_ref, wu_ref, o_ref):
    a = a_ref[...]
    g = _dot(a, wg_ref[...].astype(BF16))
    up = _dot(a, wu_ref[...].astype(BF16))
    o_ref[...] = (g * jax.nn.sigmoid(g) * up).astype(o_ref.dtype)


def _ffn_up(u, w_gu, layer, *, rows, tm, tn):
    _, d, two_ff = w_gu.shape
    ff = two_ff // 2
    up0 = ff // tn
    return pl.pallas_call(
        _ffn_up_kernel,
        out_shape=jax.ShapeDtypeStruct((rows, ff), BF16),
        grid=(rows // tm, ff // tn),
        in_specs=[pl.BlockSpec((tm, d), lambda i, j: (i, 0), pipeline_mode=pl.Buffered(1)),
                  pl.BlockSpec((None, d, tn), lambda i, j: (layer, 0, j)),
                  pl.BlockSpec((None, d, tn), lambda i, j: (layer, 0, up0 + j))],
        out_specs=pl.BlockSpec((tm, tn), lambda i, j: (i, j)),
        compiler_params=_cparams(2),
        name="ffn_up",
    )(u, w_gu, w_gu)


def _rope_tile(x, cos, sin, half):
    lane = lax.broadcasted_iota(jnp.int32, x.shape, 1)
    first = (lane % (2 * half)) < half
    width = x.shape[1]
    partner = jnp.where(first, pltpu.roll(x, width - half, 1), pltpu.roll(x, half, 1))
    return x * cos + partner * sin


def _proj_kernel(kind_ref, src_ref, scale_ref, a_ref, b_ref, *rest, has_gain, n_lat_tiles, rope_kinds):
    if has_gain:
        gain_ref, *tab_refs, o_ref, an_ref = rest
    else:
        *tab_refs, o_ref = rest
    tables = {kind: (tab_refs[2 * n], tab_refs[2 * n + 1]) for n, kind in enumerate(rope_kinds)}
    i, j = pl.program_id(0), pl.program_id(1)

    if has_gain:
        @pl.when(j == 0)
        def _():
            x = a_ref[...].astype(F32)
            ms = jnp.mean(x * x, axis=-1, keepdims=True)
            an_ref[...] = (x * lax.rsqrt(ms + RMS_EPS) * gain_ref[...]).astype(BF16)
        a = an_ref[...]
    else:
        a = a_ref[...]

    for half in range(PROJ_HALVES):
        cols = slice(half * ROPE_TN, (half + 1) * ROPE_TN)
        tile = PROJ_HALVES * j + half
        kind = jnp.where(i < n_lat_tiles, kind_ref[tile], ROPE_NONE)
        out_scale = scale_ref[tile]

        def acc():
            return _dot(a, b_ref[:, cols].astype(BF16)) * out_scale

        @pl.when(kind == ROPE_NONE)
        def _():
            o_ref[:, cols] = acc().astype(o_ref.dtype)

        for rope_kind, (cos_ref, sin_ref) in tables.items():
            @pl.when(kind == rope_kind)
            def _(cos_ref=cos_ref, sin_ref=sin_ref, rope_kind=rope_kind):
                roped = _rope_tile(acc(), cos_ref[...], sin_ref[...], ROPE_HALF_LANES[rope_kind])
                o_ref[:, cols] = roped.astype(o_ref.dtype)


def _proj(a, w, layer, kinds, scales, src_tiles, cos_tab, sin_tab, rg, *, rope_kinds, a_col0=0, gain=None, name):
    _, kdim, n = w.shape
    tm, tn = rg.tm, PROJ_HALVES * ROPE_TN
    assert a_col0 % kdim == 0 and n % tn == 0
    a_blk = a_col0 // kdim
    n_lat_tiles = rg.nl // tm
    t_tiles = rg.t // tm
    has_gain = gain is not None

    def tab_spec(kind):
        return pl.BlockSpec((None, tm, ROPE_TN), lambda i, j, kr, sr: (kind, i % t_tiles, 0))

    in_specs = [pl.BlockSpec(memory_space=pltpu.SMEM),
                pl.BlockSpec((tm, kdim), lambda i, j, kr, sr: (i, a_blk)),
                pl.BlockSpec((None, kdim, tn), lambda i, j, kr, sr: (layer, 0, sr[j]))]
    args = [scales, a, w]
    scratch = []
    if has_gain:
        in_specs.append(pl.BlockSpec((None, 1, kdim), lambda i, j, kr, sr: (layer, 0, 0)))
        args.append(gain.reshape(gain.shape[0], 1, kdim))
        scratch.append(pltpu.VMEM((tm, kdim), BF16))
    for kind in rope_kinds:
        in_specs += [tab_spec(kind), tab_spec(kind)]
        args += [cos_tab, sin_tab]
    return pl.pallas_call(
        functools.partial(_proj_kernel, has_gain=has_gain, n_lat_tiles=n_lat_tiles, rope_kinds=tuple(rope_kinds)),
        out_shape=jax.ShapeDtypeStruct((rg.n, n), BF16),
        grid_spec=pltpu.PrefetchScalarGridSpec(
            num_scalar_prefetch=2,
            grid=(rg.n // tm, n // tn),
            in_specs=in_specs,
            out_specs=pl.BlockSpec((tm, tn), lambda i, j, kr, sr: (i, j)),
            scratch_shapes=scratch),
        compiler_params=_cparams(2),
        name=name,
    )(kinds, src_tiles, *args)


def _rope_tables(seq):
    t = np.arange(seq)
    row, col = (t // GRID_W).astype(np.float64), (t % GRID_W).astype(np.float64)

    def pattern(rot_dim):
        axis_dim = rot_dim // 2
        inv = ROPE_BASE ** (-np.arange(0, axis_dim, 2, dtype=np.float64) / axis_dim)
        ar, ac = row[:, None] * inv[None, :], col[:, None] * inv[None, :]
        cos = np.concatenate([np.cos(ar), np.cos(ar), np.cos(ac), np.cos(ac)], axis=1)
        sin = np.concatenate([-np.sin(ar), np.sin(ar), -np.sin(ac), np.sin(ac)], axis=1)
        return cos, sin

    cos = np.ones((5, seq, ROPE_TN), np.float64)
    sin = np.zeros((5, seq, ROPE_TN), np.float64)
    c128, s128 = pattern(HEAD_DIM)
    c64, s64 = pattern(2 * 32)
    cos[ROPE_A], sin[ROPE_A] = np.tile(c128, (1, 2)), np.tile(s128, (1, 2))
    cos[ROPE_B], sin[ROPE_B] = np.tile(c64, (1, 4)), np.tile(s64, (1, 4))
    cos[ROPE_KR, :, :64], sin[ROPE_KR, :, :64] = c64, s64
    cos[ROPE_QR, :, 128:192], sin[ROPE_QR, :, 128:192] = c64, s64
    return jnp.asarray(cos, F32), jnp.asarray(sin, F32)


def _mixer_call(body, rg, *, latent, dst, grid, in_specs, args, tq, n_q, out_block_w, name):
    row0 = 0 if latent else rg.nl // tq
    aliases = {}
    if not latent:
        in_specs = in_specs + [pl.BlockSpec(memory_space=pl.ANY)]
        args = args + [dst]
        aliases = {len(args) - 1: 0}
    return pl.pallas_call(
        body,
        out_shape=jax.ShapeDtypeStruct((rg.n, N_HEADS * HEAD_DIM), BF16),
        grid=grid,
        in_specs=in_specs,
        out_specs=pl.BlockSpec((tq, out_block_w), lambda b, hh, qi: (row0 + b * n_q + qi, hh)),
        input_output_aliases=aliases,
        compiler_params=_cparams(3),
        name=name,
    )(*args)


def _attn_win_kernel(sink_ref, q_ref, kc_ref, vc_ref, *rest, has_lat, tq, seq, layer):
    if has_lat:
        kl_ref, vl_ref, o_ref = rest
    else:
        _, o_ref = rest
    g, qi = pl.program_id(1), pl.program_id(2)
    kc, vc = kc_ref[...], vc_ref[...]
    if has_lat:
        win = min(seq, tq + 2 * WINDOW)
        ws = pl.multiple_of(jnp.clip(qi * tq - WINDOW, 0, seq - win), WINDOW)
        kw, vw = kl_ref[pl.ds(ws, win), :], vl_ref[pl.ds(ws, win), :]
        qpos = qi * tq + lax.broadcasted_iota(jnp.int32, (tq, win), 0)
        kpos = ws + lax.broadcasted_iota(jnp.int32, (tq, win), 1)
        valid = jnp.abs(qpos - kpos) <= WINDOW
    for j in range(A_GROUP):
        q = q_ref[:, j * HEAD_DIM:(j + 1) * HEAD_DIM]
        sink = sink_ref[layer, g * A_GROUP + j] * LOG2E
        s_c = _dot_nt(q, kc)
        m = jnp.maximum(jnp.max(s_c, axis=-1, keepdims=True), sink)
        if has_lat:
            s_l = jnp.where(valid, _dot_nt(q, kw), NEG_INF)
            m = jnp.maximum(m, jnp.max(s_l, axis=-1, keepdims=True))
        p_c = jnp.exp2(s_c - m)
        den = jnp.sum(p_c, axis=-1, keepdims=True) + jnp.exp2(sink - m)
        o = _dot(p_c.astype(BF16), vc)
        if has_lat:
            p_l = jnp.exp2(s_l - m)
            den = den + jnp.sum(p_l, axis=-1, keepdims=True)
            o = o + _dot(p_l.astype(BF16), vw)
        o_ref[:, j * HEAD_DIM:(j + 1) * HEAD_DIM] = (o / den).astype(o_ref.dtype)


def _attn_win(h, a_sink, layer, rg, *, latent, dst=None):
    gw = A_GROUP * HEAD_DIM
    tq, n_q, q0 = _q_geometry(rg, latent, 256)
    c_blk0 = rg.nl // rg.c
    in_specs = [pl.BlockSpec(memory_space=pltpu.SMEM),
                pl.BlockSpec((tq, gw), lambda b, g, qi: (q0 + b * n_q + qi, H_AQ // gw + g)),
                pl.BlockSpec((rg.c, HEAD_DIM), lambda b, g, qi: (c_blk0 + b, H_AK // HEAD_DIM + g)),
                pl.BlockSpec((rg.c, HEAD_DIM), lambda b, g, qi: (c_blk0 + b, H_AV // HEAD_DIM + g))]
    args = [a_sink, h, h, h]
    if latent:
        in_specs += [pl.BlockSpec((rg.t, HEAD_DIM), lambda b, g, qi: (b, H_AK // HEAD_DIM + g)),
                     pl.BlockSpec((rg.t, HEAD_DIM), lambda b, g, qi: (b, H_AV // HEAD_DIM + g))]
        args += [h, h]
    return _mixer_call(functools.partial(_attn_win_kernel, has_lat=latent, tq=tq, seq=rg.t, layer=layer), rg,
                       latent=latent, dst=dst, grid=(rg.b, A_KV_HEADS, n_q), in_specs=in_specs, args=args,
                       tq=tq, n_q=n_q, out_block_w=gw, name="attn_win_lat" if latent else "attn_win_ctx")


def _softmax_pv(q, keys, values):
    scores = [_dot_nt(q, k) for k in keys]
    m = functools.reduce(jnp.maximum, [jnp.max(s, axis=-1, keepdims=True) for s in scores])
    den, out = None, None
    for s, v in zip(scores, values):
        p = jnp.exp2(s - m)
        d = jnp.sum(p, axis=-1, keepdims=True)
        o = _dot(p.astype(BF16), v)
        den = d if den is None else den + d
        out = o if out is None else out + o
    return out / den


def _attn_diff_kernel(lam_ref, gsub_ref, q_ref, kc_ref, vc_ref, *rest, has_lat, lam_init):
    if has_lat:
        kl_ref, vl_ref, o_ref = rest
    else:
        _, o_ref = rest
    lf = lam_ref[...]
    lam = (jnp.exp(jnp.sum(lf[0:1] * lf[1:2], axis=-1, keepdims=True))
           - jnp.exp(jnp.sum(lf[2:3] * lf[3:4], axis=-1, keepdims=True)) + lam_init)
    keys = [kc_ref[...]] + ([kl_ref[...]] if has_lat else [])
    values = [vc_ref[...]] + ([vl_ref[...]] if has_lat else [])
    tq = q_ref.shape[0]
    sub = min(tq, Q_SUB)
    for r in range(tq // sub):
        q = q_ref[r * sub:(r + 1) * sub, :]
        first_map = lax.broadcasted_iota(jnp.int32, q.shape, 1) < B_QK_DIM
        zero = jnp.zeros_like(q)

        o = (_softmax_pv(jnp.where(first_map, q, zero), keys, values)
             - lam * _softmax_pv(jnp.where(first_map, zero, q), keys, values))
        ms = jnp.mean(o * o, axis=-1, keepdims=True)
        o = o * lax.rsqrt(ms + RMS_EPS) * gsub_ref[...] * (1.0 - lam_init)
        o_ref[r * sub:(r + 1) * sub, :] = o.astype(o_ref.dtype)


def _kv_specs(rg, col_k, col_v, latent, k_width=HEAD_DIM):
    c_blk0 = rg.nl // rg.c
    specs = [pl.BlockSpec((rg.c, k_width), lambda b, hh, qi: (c_blk0 + b, col_k(hh))),
             pl.BlockSpec((rg.c, HEAD_DIM), lambda b, hh, qi: (c_blk0 + b, col_v(hh)))]
    if latent:
        specs += [pl.BlockSpec((rg.t, k_width), lambda b, hh, qi: (b, col_k(hh))),
                  pl.BlockSpec((rg.t, HEAD_DIM), lambda b, hh, qi: (b, col_v(hh)))]
    return specs


def _q_geometry(rg, latent, tq_lat):
    if latent:
        tq = min(tq_lat, rg.t)
        return tq, rg.t // tq, 0
    return rg.c, 1, rg.nl // rg.c


def _attn_diff(h, b_lambda, b_subln_g, layer, rg, *, latent, lam_init, dst=None):
    tq, n_q, q0 = _q_geometry(rg, latent, 2048)
    in_specs = [pl.BlockSpec((None, 4, B_QK_DIM), lambda b, hh, qi: (layer, 0, 0)),
                pl.BlockSpec((None, 1, HEAD_DIM), lambda b, hh, qi: (layer, 0, 0)),
                pl.BlockSpec((tq, HEAD_DIM), lambda b, hh, qi: (q0 + b * n_q + qi, H_BQ // HEAD_DIM + hh))]
    in_specs += _kv_specs(rg, lambda hh: H_BK // HEAD_DIM + hh, lambda hh: H_BV // HEAD_DIM + hh, latent)
    n_kv = 4 if latent else 2
    args = [b_lambda, b_subln_g.reshape(b_subln_g.shape[0], 1, HEAD_DIM), h] + [h] * n_kv
    return _mixer_call(functools.partial(_attn_diff_kernel, has_lat=latent, lam_init=lam_init), rg,
                       latent=latent, dst=dst, grid=(rg.b, N_HEADS, n_q), in_specs=in_specs, args=args,
                       tq=tq, n_q=n_q, out_block_w=HEAD_DIM, name="attn_diff_lat" if latent else "attn_diff_ctx")


def _attn_mla_kernel(q_ref, knc_ref, vc_ref, krc_ref, *rest, has_lat):
    if has_lat:
        knl_ref, vl_ref, krl_ref, o_ref = rest
    else:
        _, o_ref = rest
    segs = [(knc_ref, krc_ref, vc_ref)] + ([(knl_ref, krl_ref, vl_ref)] if has_lat else [])
    keys = [jnp.concatenate([kn[...], kr[...]], axis=1) for kn, kr, _ in segs]
    values = [v[...] for _, _, v in segs]
    tq = q_ref.shape[0]
    sub = min(tq, Q_SUB)
    for r in range(tq // sub):
        rows = slice(r * sub, (r + 1) * sub)
        o_ref[rows, :] = _softmax_pv(q_ref[rows, :], keys, values).astype(o_ref.dtype)


def _attn_mla(qc, kv2, h, rg, *, latent, dst=None):
    tq, n_q, q0 = _q_geometry(rg, latent, 2048)
    c_blk0 = rg.nl // rg.c
    kr_col = H_CKR // HEAD_DIM
    kv = _kv_specs(rg, lambda hh: 2 * hh, lambda hh: 2 * hh + 1, latent)
    in_specs = [pl.BlockSpec((tq, 2 * HEAD_DIM), lambda b, hh, qi: (q0 + b * n_q + qi, hh)),
                kv[0], kv[1],
                pl.BlockSpec((rg.c, HEAD_DIM), lambda b, hh, qi: (c_blk0 + b, kr_col))]
    args = [qc, kv2, kv2, h]
    if latent:
        in_specs += [kv[2], kv[3], pl.BlockSpec((rg.t, HEAD_DIM), lambda b, hh, qi: (b, kr_col))]
        args += [kv2, kv2, h]
    return _mixer_call(functools.partial(_attn_mla_kernel, has_lat=latent), rg,
                       latent=latent, dst=dst, grid=(rg.b, N_HEADS, n_q), in_specs=in_specs, args=args,
                       tq=tq, n_q=n_q, out_block_w=HEAD_DIM, name="attn_mla_lat" if latent else "attn_mla_ctx")


def _dft_tables(n):
    idx = np.arange(n, dtype=np.int64)
    ang = (np.outer(idx, idx) % n).astype(np.float64) * (2.0 * np.pi / n)
    return jnp.asarray(np.cos(ang), F32), jnp.asarray(np.sin(ang), F32)


def _fourier_chan_kernel(z_ref, cc_ref, sc_ref, dw_ref, zc_ref, zs_ref):
    dw = dw_ref[...].astype(BF16)
    z = z_ref[...]
    zc_ref[...] = _dot(z, _dot(cc_ref[...], dw).astype(BF16)).astype(zc_ref.dtype)
    zs_ref[...] = _dot(z, _dot(sc_ref[...], dw).astype(BF16)).astype(zs_ref.dtype)


def _fourier_chan(h, cos_c, sin_c, d_w, layer, rg):
    tm = rg.tm
    col0 = H_DX // HEAD_DIM
    tile = pl.BlockSpec((tm, HEAD_DIM), lambda i, g: (i, g))
    const = pl.BlockSpec((HEAD_DIM, HEAD_DIM), lambda i, g: (0, 0))
    out = jax.ShapeDtypeStruct((rg.n, N_HEADS * HEAD_DIM), BF16)
    return pl.pallas_call(
        _fourier_chan_kernel,
        out_shape=[out, out],
        grid=(rg.n // tm, N_HEADS),
        in_specs=[pl.BlockSpec((tm, HEAD_DIM), lambda i, g: (i, col0 + g)), const, const,
                  pl.BlockSpec((None, None, HEAD_DIM, HEAD_DIM), lambda i, g: (layer, g, 0, 0))],
        out_specs=[tile, tile],
        compiler_params=_cparams(2),
        name="fourier_chan",
    )(h, cos_c, sin_c, d_w)


def _fourier_seq_kernel(ct_ref, st_ref, zc_ref, zs_ref, *rest, norm):
    o_ref = rest[-1]
    o = _dot(ct_ref[...], zc_ref[...]) - _dot(st_ref[...], zs_ref[...])
    o_ref[...] = (o * norm).astype(o_ref.dtype)


def _fourier_seq(cos_t, sin_t, zc, zs, rg, *, latent, dst=None):
    length = rg.t if latent else rg.c
    tm = min(512, length)
    n_i = length // tm
    z_blk0 = 0 if latent else rg.nl // rg.c
    row0 = 0 if latent else rg.nl // tm
    width = zc.shape[1]
    dft_spec = pl.BlockSpec((tm, length), lambda b, i: (i, 0))
    z_spec = pl.BlockSpec((length, width), lambda b, i: (z_blk0 + b, 0))
    in_specs, args, aliases = [dft_spec, dft_spec, z_spec, z_spec], [cos_t, sin_t, zc, zs], {}
    if not latent:
        in_specs.append(pl.BlockSpec(memory_space=pl.ANY))
        args.append(dst)
        aliases = {len(args) - 1: 0}
    return pl.pallas_call(
        functools.partial(_fourier_seq_kernel, norm=(length * HEAD_DIM) ** -0.5),
        out_shape=jax.ShapeDtypeStruct((rg.n, width), BF16),
        grid=(rg.b, n_i),
        in_specs=in_specs,
        out_specs=pl.BlockSpec((tm, width), lambda b, i: (row0 + b * n_i + i, 0)),
        input_output_aliases=aliases,
        compiler_params=_cparams(2),
        name="fourier_seq_lat" if latent else "fourier_seq_ctx",
    )(*args)


def _prep_w_in(w_in):
    return jnp.pad(w_in, ((0, 0), (0, 0), (0, H_COLS - w_in.shape[2]))).astype(BF16)


def _prep_w_uq(w_uq):
    depth, k, _ = w_uq.shape
    w = w_uq.reshape(depth, k, N_HEADS, C_NOPE + C_ROPE)
    w = jnp.pad(w, ((0, 0), (0, 0), (0, 0), (0, 2 * HEAD_DIM - C_NOPE - C_ROPE)))
    return w.reshape(depth, k, N_HEADS * 2 * HEAD_DIM).astype(BF16)


def _in_proj_tiles():
    step = PROJ_HALVES * ROPE_TN
    src_cols = {"aq": 0, "bq": 1024, "cq": 2048, "dx": 3584, "ak": 4608, "bk": 5120, "bv": 6144,
                "ckv": 7168, "ckr": 7680}
    layout = [("cq", H_CQ, H_AQ, (ROPE_NONE, ROPE_NONE), 1.0),
              ("aq", H_AQ, H_BQ, (ROPE_A, ROPE_A), HEAD_DIM ** -0.5 * LOG2E),
              ("bq", H_BQ, H_DX, (ROPE_B, ROPE_B), B_QK_DIM ** -0.5 * LOG2E),
              ("dx", H_DX, H_AK, (ROPE_NONE, ROPE_NONE), 1.0),
              ("ak", H_AK, H_BK, (ROPE_A, ROPE_NONE), 1.0),
              ("bk", H_BK, H_BV, (ROPE_B, ROPE_B), 1.0),
              ("bv", H_BV, H_CKV, (ROPE_NONE, ROPE_NONE), 1.0),
              ("ckv", H_CKV, H_CKR, (ROPE_NONE, ROPE_NONE), 1.0),
              ("ckr", H_CKR, H_COLS, (ROPE_KR, ROPE_NONE), 1.0)]
    kinds = np.zeros((H_COLS // ROPE_TN,), np.int32)
    scales = np.ones((H_COLS // ROPE_TN,), np.float32)
    src = np.zeros((H_COLS // step,), np.int32)
    for name, lo, hi, kind, scale in layout:
        assert lo % step == 0 and hi % step == 0 and src_cols[name] % step == 0
        n = (hi - lo) // step
        kinds[lo // ROPE_TN:hi // ROPE_TN] = np.tile(np.asarray(kind, np.int32), n)
        scales[lo // ROPE_TN:hi // ROPE_TN] = scale
        src[lo // step:hi // step] = src_cols[name] // step + np.arange(n)
    return jnp.asarray(kinds), jnp.asarray(scales), jnp.asarray(src)


def _ffn(u, w_gu, w_d, layer, *, rows, tm):
    hid = _ffn_up(u, w_gu, layer, rows=rows, tm=tm, tn=512)
    return _mm(hid, w_d, layer, rows=rows, tm=tm, tn=1024, tk=2048, name="ffn_down")


def _mixing(u, p, layer, consts, rg, *, lam_init, need_ctx):
    cos_tab, sin_tab, (kinds_in, scales_in, src_in), dft = consts
    h = _proj(u, p["w_in"], layer, kinds_in, scales_in, src_in, cos_tab, sin_tab, rg,
              rope_kinds=(ROPE_A, ROPE_B, ROPE_KR), name="in_proj")
    n_tiles = N_HEADS * 2 * HEAD_DIM // ROPE_TN
    ident = jnp.arange(n_tiles // PROJ_HALVES, dtype=jnp.int32)
    q_scale = jnp.full((n_tiles,), (C_NOPE + C_ROPE) ** -0.5 * LOG2E, F32)
    qc = _proj(h, p["c_w_uq"], layer, jnp.full((n_tiles,), ROPE_QR, jnp.int32), q_scale, ident, cos_tab, sin_tab,
               rg, rope_kinds=(ROPE_QR,), a_col0=H_CQ, gain=p["c_q_norm_g"], name="mla_q_up")
    kv2 = _proj(h, p["c_w_ukv"], layer, jnp.zeros((n_tiles,), jnp.int32), jnp.ones((n_tiles,), F32), ident,
                cos_tab, sin_tab, rg, rope_kinds=(), a_col0=H_CKV, gain=p["c_kv_norm_g"], name="mla_kv_up")
    zc, zs = _fourier_chan(h, dft["cos_c"], dft["sin_c"], p["d_w"], layer, rg)

    ya = _attn_win(h, p["a_sink"], layer, rg, latent=True)
    yb = _attn_diff(h, p["b_lambda"], p["b_subln_g"], layer, rg, latent=True, lam_init=lam_init)
    yc = _attn_mla(qc, kv2, h, rg, latent=True)
    yd = _fourier_seq(dft["cos_t"], dft["sin_t"], zc, zs, rg, latent=True)
    if need_ctx:
        ya = _attn_win(h, p["a_sink"], layer, rg, latent=False, dst=ya)
        yb = _attn_diff(h, p["b_lambda"], p["b_subln_g"], layer, rg, latent=False, lam_init=lam_init, dst=yb)
        yc = _attn_mla(qc, kv2, h, rg, latent=False, dst=yc)
        yd = _fourier_seq(dft["cos_x"], dft["sin_x"], zc, zs, rg, latent=False, dst=yd)
    rows = rg.n if need_ctx else rg.nl
    return _out_proj([ya, yb, yc, yd], p["w_out"], layer, rows=rows, tm=rg.tm, tn=512)


def kernel(x, c, ctx, c_ctx, w_mod, b_mod, ffn1_w_gu, ffn1_w_d, ffn2_w_gu, ffn2_w_d, ln_g, ln_b, w_in, w_out,
           a_sink, b_lambda, b_subln_g, c_q_norm_g, c_kv_norm_g, c_w_uq, c_w_ukv, d_w):
    n_batch, seq, d = x.shape
    ctx_len = ctx.shape[1]
    depth = w_mod.shape[0]
    rg = _Rows(n_batch, seq, ctx_len)
    alpha = (2.0 * depth) ** 0.25

    cos_tab, sin_tab = _rope_tables(seq)
    cos_t, sin_t = _dft_tables(seq)
    cos_x, sin_x = _dft_tables(ctx_len)
    cos_c, sin_c = _dft_tables(HEAD_DIM)
    dft = {"cos_t": cos_t.astype(BF16), "sin_t": sin_t.astype(BF16),
           "cos_x": cos_x.astype(BF16), "sin_x": sin_x.astype(BF16),
           "cos_c": cos_c.astype(BF16), "sin_c": sin_c.astype(BF16)}
    consts = (cos_tab, sin_tab, _in_proj_tiles(), dft)

    n_c_rows = 8
    c_rows = jnp.concatenate([c, c_ctx[None, :], jnp.zeros((n_c_rows - n_batch - 1, d), c.dtype)], axis=0)

    p = {"w_in": _prep_w_in(w_in), "w_out": w_out, "a_sink": a_sink, "b_lambda": b_lambda,
         "b_subln_g": b_subln_g, "c_q_norm_g": c_q_norm_g, "c_kv_norm_g": c_kv_norm_g,
         "c_w_uq": _prep_w_uq(c_w_uq), "c_w_ukv": c_w_ukv, "d_w": d_w}
    mod_all = _mod_vectors(c_rows, w_mod, b_mod).reshape(depth, n_c_rows, N_MOD, d)
    mods = [mod_all[l] for l in range(depth)]
    xs, u = _modulate(x.reshape(n_batch * seq, d), ctx.reshape(n_batch * ctx_len, d), mods[0], rg, shift_idx=0)
    for l in range(depth):
        last = l == depth - 1
        lam_init = 0.8 - 0.6 * math.exp(-0.3 * l)
        mod = mods[l]
        y = _ffn(u, ffn1_w_gu, ffn1_w_d, l, rows=rg.n, tm=rg.tm)
        xs, u = _ln_res(xs, y, mod, ln_g[l, 0], ln_b[l, 0], rg, rows=rg.n, alpha=alpha,
                        gate_idx=2, gate_mul=0.5, next_mod=mod, next_shift_idx=3, name="ln_ffn1")
        y = _mixing(u, p, l, consts, rg, lam_init=lam_init, need_ctx=not last)
        rows = rg.nl if last else rg.n
        xs, u = _ln_res(xs, y, mod, ln_g[l, 1], ln_b[l, 1], rg, rows=rows, alpha=alpha,
                        gate_idx=5, gate_mul=1.0, next_mod=mod, next_shift_idx=6, name="ln_mix")
        y = _ffn(u, ffn2_w_gu, ffn2_w_d, l, rows=rows, tm=rg.tm)
        xs, u = _ln_res(xs, y, mod, ln_g[l, 2], ln_b[l, 2], rg, rows=rows, alpha=alpha,
                        gate_idx=8, gate_mul=0.5, next_mod=None if last else mods[l + 1],
                        next_shift_idx=None if last else 0, name="ln_ffn2")
    return xs[:rg.nl].reshape(n_batch, seq, d)
```

```python
import functools
import math

import numpy as np
import jax
import jax.numpy as jnp
from jax import lax
from jax.experimental import pallas as pl
from jax.experimental.pallas import tpu as pltpu

F32 = jnp.float32
BF16 = jnp.bfloat16

GRID_W = 64
HEAD_DIM = 128
WINDOW = 128
N_HEADS = 8
A_KV_HEADS = 2
A_GROUP = N_HEADS // A_KV_HEADS
B_QK_DIM = 64
C_Q_LORA = 1536
C_KV_LORA = 512
C_NOPE = 128
C_ROPE = 64
N_MOD = 9
ROPE_BASE = 10000.0
LN_EPS = 1e-5
RMS_EPS = 1e-6
NEG_INF = -1e30

H_CQ, H_AQ, H_BQ, H_DX = 0, 1536, 2560, 3584
H_AK, H_AV, H_BK, H_BV, H_CKV, H_CKR = 4608, 4864, 5120, 6144, 7168, 7680
H_COLS = 8192

ROPE_NONE, ROPE_A, ROPE_B, ROPE_KR, ROPE_QR = 0, 1, 2, 3, 4
ROPE_HALF_LANES = {ROPE_A: HEAD_DIM // 4, ROPE_B: B_QK_DIM // 4, ROPE_KR: C_ROPE // 4, ROPE_QR: C_ROPE // 4}

VMEM_LIMIT_BYTES = 56 * 1024 * 1024
ROPE_TN = 256
PROJ_HALVES = 2
Q_SUB = 256
LOG2E = math.log2(math.e)


def _cparams(n_axes):
    return pltpu.CompilerParams(dimension_semantics=("arbitrary",) * n_axes,
                                vmem_limit_bytes=VMEM_LIMIT_BYTES)


def _dot(a, b):
    return jnp.dot(a, b, preferred_element_type=F32)


def _dot_nt(a, b):
    return lax.dot_general(a, b, (((1,), (1,)), ((), ())), preferred_element_type=F32)


def _mm_kernel(a_ref, b_ref, o_ref, acc_ref, *, nk):
    k, j = pl.program_id(1), pl.program_id(2)

    def part():
        return _dot(a_ref[...], b_ref[...].astype(BF16))

    @pl.when(k == 0)
    def _():
        acc_ref[j] = part()

    if nk > 2:
        @pl.when((k > 0) & (k < nk - 1))
        def _():
            acc_ref[j] += part()

    @pl.when(k == nk - 1)
    def _():
        o_ref[...] = (acc_ref[j] + part()).astype(o_ref.dtype)


def _mm(a, b, layer, *, rows, tm, tn, tk, name):
    _, kdim, n = b.shape
    assert rows % tm == 0 and n % tn == 0 and kdim % tk == 0 and kdim // tk >= 2
    nk = kdim // tk
    return pl.pallas_call(
        functools.partial(_mm_kernel, nk=nk),
        out_shape=jax.ShapeDtypeStruct((rows, n), BF16),
        grid=(rows // tm, nk, n // tn),
        in_specs=[pl.BlockSpec((tm, tk), lambda i, k, j: (i, k)),
                  pl.BlockSpec((None, tk, tn), lambda i, k, j: (layer, k, j))],
        out_specs=pl.BlockSpec((tm, tn), lambda i, k, j: (i, jnp.where(k == nk - 1, j, 0))),
        scratch_shapes=[pltpu.VMEM((n // tn, tm, tn), F32)],
        compiler_params=_cparams(3),
        name=name,
    )(a, b)


def _out_proj_kernel(*refs):
    *y_refs, w_ref, o_ref = refs
    width = y_refs[0].shape[1]
    acc = None
    for m, y_ref in enumerate(y_refs):
        part = _dot(y_ref[...], w_ref[m * width:(m + 1) * width, :].astype(BF16))
        acc = part if acc is None else acc + part
    o_ref[...] = acc.astype(o_ref.dtype)


def _out_proj(ys, w_out, layer, *, rows, tm, tn):
    _, kdim, n = w_out.shape
    width = ys[0].shape[1]
    assert kdim == width * len(ys)
    y_spec = pl.BlockSpec((tm, width), lambda i, j: (i, 0))
    return pl.pallas_call(
        _out_proj_kernel,
        out_shape=jax.ShapeDtypeStruct((rows, n), BF16),
        grid=(rows // tm, n // tn),
        in_specs=[y_spec] * len(ys) + [pl.BlockSpec((None, kdim, tn), lambda i, j: (layer, 0, j))],
        out_specs=pl.BlockSpec((tm, tn), lambda i, j: (i, j)),
        compiler_params=_cparams(2),
        name="out_proj",
    )(*ys, w_out)


def _mod_kernel(c_ref, w_ref, b_ref, o_ref):
    c = c_ref[...]
    sc = (c * jax.nn.sigmoid(c)).astype(BF16)
    o_ref[...] = _dot(sc, w_ref[...].astype(BF16)) + b_ref[...]


def _mod_vectors(c_rows, w_mod, b_mod):
    depth, d, n = w_mod.shape
    r = c_rows.shape[0]
    tn = 1024
    return pl.pallas_call(
        _mod_kernel,
        out_shape=jax.ShapeDtypeStruct((depth, r, n), F32),
        grid=(depth, n // tn),
        in_specs=[pl.BlockSpec((r, d), lambda l, j: (0, 0)),
                  pl.BlockSpec((None, d, tn), lambda l, j: (l, 0, j)),
                  pl.BlockSpec((None, 1, tn), lambda l, j: (l, 0, j))],
        out_specs=pl.BlockSpec((None, r, tn), lambda l, j: (l, 0, j)),
        compiler_params=_cparams(2),
        name="mod_vectors",
    )(c_rows, w_mod, b_mod.reshape(depth, 1, n))


def _seg_index(i, n_lat_tiles, tiles_per_batch, n_batch):
    return jnp.where(i < n_lat_tiles, i // tiles_per_batch, n_batch)


def _modulate_kernel(x_ref, c_ref, mod_ref, xs_ref, u_ref, *, shift_idx, n_lat_tiles):
    shift = mod_ref[shift_idx:shift_idx + 1, :]
    scale = mod_ref[shift_idx + 1:shift_idx + 2, :]

    def emit(src_ref):
        x = src_ref[...]
        xs_ref[...] = x
        u_ref[...] = (x * (1.0 + scale) + shift).astype(u_ref.dtype)

    @pl.when(pl.program_id(0) < n_lat_tiles)
    def _():
        emit(x_ref)

    @pl.when(pl.program_id(0) >= n_lat_tiles)
    def _():
        emit(c_ref)


def _ln_res_kernel(x_ref, y_ref, mod_ref, g_ref, b_ref, *rest, alpha, gate_idx, gate_mul, next_shift_idx):
    gate = mod_ref[gate_idx:gate_idx + 1, :]
    z = alpha * x_ref[...] + (gate_mul * gate) * y_ref[...].astype(F32)
    mu = jnp.mean(z, axis=-1, keepdims=True)
    zc = z - mu
    var = jnp.mean(zc * zc, axis=-1, keepdims=True)
    xn = zc * lax.rsqrt(var + LN_EPS) * g_ref[...] + b_ref[...]
    if next_shift_idx is None:
        xo_ref, = rest
        xo_ref[...] = xn
    else:
        nmod_ref, xo_ref, u_ref = rest
        xo_ref[...] = xn
        shift = nmod_ref[next_shift_idx:next_shift_idx + 1, :]
        scale = nmod_ref[next_shift_idx + 1:next_shift_idx + 2, :]
        u_ref[...] = (xn * (1.0 + scale) + shift).astype(u_ref.dtype)


class _Rows:
    def __init__(self, n_batch, seq, ctx_len):
        self.b, self.t, self.c = n_batch, seq, ctx_len
        self.nl, self.nc = n_batch * seq, n_batch * ctx_len
        self.n = self.nl + self.nc
        self.tr = math.gcd(256, math.gcd(seq, ctx_len))
        self.tm = math.gcd(1024, math.gcd(seq, self.nc))

    def mod_spec(self, tile, d):
        n_lat_tiles, per_batch, nb = self.nl // tile, self.t // tile, self.b
        return pl.BlockSpec((None, N_MOD, d),
                            lambda i, *_: (_seg_index(i, n_lat_tiles, per_batch, nb), 0, 0))


def _modulate(x, ctx, mod, rg, *, shift_idx):
    d = x.shape[1]
    tr = rg.tr
    n_lat_tiles, n_ctx_tiles = rg.nl // tr, rg.nc // tr
    row_spec = pl.BlockSpec((tr, d), lambda i: (i, 0))
    x_spec = pl.BlockSpec((tr, d), lambda i: (jnp.minimum(i, n_lat_tiles - 1), 0))
    c_spec = pl.BlockSpec((tr, d), lambda i: (jnp.clip(i - n_lat_tiles, 0, n_ctx_tiles - 1), 0))
    return pl.pallas_call(
        functools.partial(_modulate_kernel, shift_idx=shift_idx, n_lat_tiles=n_lat_tiles),
        out_shape=[jax.ShapeDtypeStruct((rg.n, d), F32), jax.ShapeDtypeStruct((rg.n, d), BF16)],
        grid=(rg.n // tr,),
        in_specs=[x_spec, c_spec, rg.mod_spec(tr, d)],
        out_specs=[row_spec, row_spec],
        compiler_params=_cparams(1),
        name="modulate",
    )(x, ctx, mod)


def _ln_res(x, y, mod, ln_g, ln_b, rg, *, rows, alpha, gate_idx, gate_mul, next_mod, next_shift_idx, name):
    d = x.shape[1]
    tr = rg.tr
    row_spec = pl.BlockSpec((tr, d), lambda i: (i, 0))
    vec_spec = pl.BlockSpec((1, d), lambda i: (0, 0))
    out_shape = [jax.ShapeDtypeStruct((rows, d), F32)]
    out_specs = [row_spec]
    in_specs = [row_spec, row_spec, rg.mod_spec(tr, d), vec_spec, vec_spec]
    args = [x, y, mod, ln_g.reshape(1, d), ln_b.reshape(1, d)]
    if next_shift_idx is not None:
        in_specs.append(rg.mod_spec(tr, d))
        args.append(next_mod)
        out_shape.append(jax.ShapeDtypeStruct((rows, d), BF16))
        out_specs.append(row_spec)
    res = pl.pallas_call(
        functools.partial(_ln_res_kernel, alpha=alpha, gate_idx=gate_idx, gate_mul=gate_mul,
                          next_shift_idx=next_shift_idx),
        out_shape=out_shape,
        grid=(rows // tr,),
        in_specs=in_specs,
        out_specs=out_specs,
        compiler_params=_cparams(1),
        name=name,
    )(*args)
    return (res[0], res[1]) if next_shift_idx is not None else (res[0], None)


def _ffn_up_kernel(a_ref, wg_ref, wu_ref, o_ref):
    a = a_ref[...]
    g = _dot(a, wg_ref[...].astype(BF16))
    up = _dot(a, wu_ref[...].astype(BF16))
    o_ref[...] = (g * jax.nn.sigmoid(g) * up).astype(o_ref.dtype)


def _ffn_up(u, w_gu, layer, *, rows, tm, tn):
    _, d, two_ff = w_gu.shape
    ff = two_ff // 2
    up0 = ff // tn
    return pl.pallas_call(
        _ffn_up_kernel,
        out_shape=jax.ShapeDtypeStruct((rows, ff), BF16),
        grid=(rows // tm, ff // tn),
        in_specs=[pl.BlockSpec((tm, d), lambda i, j: (i, 0)),
                  pl.BlockSpec((None, d, tn), lambda i, j: (layer, 0, j)),
                  pl.BlockSpec((None, d, tn), lambda i, j: (layer, 0, up0 + j))],
        out_specs=pl.BlockSpec((tm, tn), lambda i, j: (i, j)),
        compiler_params=_cparams(2),
        name="ffn_up",
    )(u, w_gu, w_gu)


def _rope_tile(x, cos, sin, half):
    lane = lax.broadcasted_iota(jnp.int32, x.shape, 1)
    first = (lane % (2 * half)) < half
    width = x.shape[1]
    partner = jnp.where(first, pltpu.roll(x, width - half, 1), pltpu.roll(x, half, 1))
    return x * cos + partner * sin


def _proj_kernel(kind_ref, src_ref, scale_ref, a_ref, b_ref, *rest, has_gain, n_lat_tiles, rope_kinds):
    if has_gain:
        gain_ref, *tab_refs, o_ref, an_ref = rest
    else:
        *tab_refs, o_ref = rest
    tables = {kind: (tab_refs[2 * n], tab_refs[2 * n + 1]) for n, kind in enumerate(rope_kinds)}
    i, j = pl.program_id(0), pl.program_id(1)

    if has_gain:
        @pl.when(j == 0)
        def _():
            x = a_ref[...].astype(F32)
            ms = jnp.mean(x * x, axis=-1, keepdims=True)
            an_ref[...] = (x * lax.rsqrt(ms + RMS_EPS) * gain_ref[...]).astype(BF16)
    lhs_ref = an_ref if has_gain else a_ref

    for half in range(PROJ_HALVES):
        cols = slice(half * ROPE_TN, (half + 1) * ROPE_TN)
        tile = PROJ_HALVES * j + half
        kind = jnp.where(i < n_lat_tiles, kind_ref[tile], ROPE_NONE)
        out_scale = scale_ref[tile]

        def acc():
            return _dot(lhs_ref[...], b_ref[:, cols].astype(BF16)) * out_scale

        @pl.when(kind == ROPE_NONE)
        def _():
            o_ref[:, cols] = acc().astype(o_ref.dtype)

        for rope_kind, (cos_ref, sin_ref) in tables.items():
            @pl.when(kind == rope_kind)
            def _(cos_ref=cos_ref, sin_ref=sin_ref, rope_kind=rope_kind):
                roped = _rope_tile(acc(), cos_ref[...], sin_ref[...], ROPE_HALF_LANES[rope_kind])
                o_ref[:, cols] = roped.astype(o_ref.dtype)


def _proj(a, w, layer, kinds, scales, src_tiles, cos_tab, sin_tab, rg, *, rope_kinds, a_col0=0, gain=None, name):
    _, kdim, n = w.shape
    tm, tn = rg.tm, PROJ_HALVES * ROPE_TN
    assert a_col0 % kdim == 0 and n % tn == 0
    a_blk = a_col0 // kdim
    n_lat_tiles = rg.nl // tm
    t_tiles = rg.t // tm
    has_gain = gain is not None

    def tab_spec(kind):
        return pl.BlockSpec((None, tm, ROPE_TN), lambda i, j, kr, sr: (kind, i % t_tiles, 0))

    in_specs = [pl.BlockSpec(memory_space=pltpu.SMEM),
                pl.BlockSpec((tm, kdim), lambda i, j, kr, sr: (i, a_blk)),
                pl.BlockSpec((None, kdim, tn), lambda i, j, kr, sr: (layer, 0, sr[j]))]
    args = [scales, a, w]
    scratch = []
    if has_gain:
        in_specs.append(pl.BlockSpec((None, 1, kdim), lambda i, j, kr, sr: (layer, 0, 0)))
        args.append(gain.reshape(gain.shape[0], 1, kdim))
        scratch.append(pltpu.VMEM((tm, kdim), BF16))
    for kind in rope_kinds:
        in_specs += [tab_spec(kind), tab_spec(kind)]
        args += [cos_tab, sin_tab]
    return pl.pallas_call(
        functools.partial(_proj_kernel, has_gain=has_gain, n_lat_tiles=n_lat_tiles, rope_kinds=tuple(rope_kinds)),
        out_shape=jax.ShapeDtypeStruct((rg.n, n), BF16),
        grid_spec=pltpu.PrefetchScalarGridSpec(
            num_scalar_prefetch=2,
            grid=(rg.n // tm, n // tn),
            in_specs=in_specs,
            out_specs=pl.BlockSpec((tm, tn), lambda i, j, kr, sr: (i, j)),
            scratch_shapes=scratch),
        compiler_params=_cparams(2),
        name=name,
    )(kinds, src_tiles, *args)


def _rope_tables(seq):
    t = np.arange(seq)
    row, col = (t // GRID_W).astype(np.float64), (t % GRID_W).astype(np.float64)

    def pattern(rot_dim):
        axis_dim = rot_dim // 2
        inv = ROPE_BASE ** (-np.arange(0, axis_dim, 2, dtype=np.float64) / axis_dim)
        ar, ac = row[:, None] * inv[None, :], col[:, None] * inv[None, :]
        cos = np.concatenate([np.cos(ar), np.cos(ar), np.cos(ac), np.cos(ac)], axis=1)
        sin = np.concatenate([-np.sin(ar), np.sin(ar), -np.sin(ac), np.sin(ac)], axis=1)
        return cos, sin

    cos = np.ones((5, seq, ROPE_TN), np.float64)
    sin = np.zeros((5, seq, ROPE_TN), np.float64)
    c128, s128 = pattern(HEAD_DIM)
    c64, s64 = pattern(2 * 32)
    cos[ROPE_A], sin[ROPE_A] = np.tile(c128, (1, 2)), np.tile(s128, (1, 2))
    cos[ROPE_B], sin[ROPE_B] = np.tile(c64, (1, 4)), np.tile(s64, (1, 4))
    cos[ROPE_KR, :, :64], sin[ROPE_KR, :, :64] = c64, s64
    cos[ROPE_QR, :, 128:192], sin[ROPE_QR, :, 128:192] = c64, s64
    return jnp.asarray(cos, F32), jnp.asarray(sin, F32)


def _mixer_call(body, rg, *, latent, dst, grid, in_specs, args, tq, n_q, out_block_w, name):
    row0 = 0 if latent else rg.nl // tq
    aliases = {}
    if not latent:
        in_specs = in_specs + [pl.BlockSpec(memory_space=pl.ANY)]
        args = args + [dst]
        aliases = {len(args) - 1: 0}
    return pl.pallas_call(
        body,
        out_shape=jax.ShapeDtypeStruct((rg.n, N_HEADS * HEAD_DIM), BF16),
        grid=grid,
        in_specs=in_specs,
        out_specs=pl.BlockSpec((tq, out_block_w), lambda b, hh, qi: (row0 + b * n_q + qi, hh)),
        input_output_aliases=aliases,
        compiler_params=_cparams(3),
        name=name,
    )(*args)


def _attn_win_kernel(sink_ref, q_ref, kc_ref, vc_ref, *rest, has_lat, tq, seq, layer):
    if has_lat:
        kl_ref, vl_ref, o_ref = rest
    else:
        _, o_ref = rest
    g, qi = pl.program_id(1), pl.program_id(2)
    kc, vc = kc_ref[...], vc_ref[...]
    if has_lat:
        win = min(seq, tq + 2 * WINDOW)
        ws = pl.multiple_of(jnp.clip(qi * tq - WINDOW, 0, seq - win), WINDOW)
        kw, vw = kl_ref[pl.ds(ws, win), :], vl_ref[pl.ds(ws, win), :]
        qpos = qi * tq + lax.broadcasted_iota(jnp.int32, (tq, win), 0)
        kpos = ws + lax.broadcasted_iota(jnp.int32, (tq, win), 1)
        valid = jnp.abs(qpos - kpos) <= WINDOW
    for j in range(A_GROUP):
        q = q_ref[:, j * HEAD_DIM:(j + 1) * HEAD_DIM]
        sink = sink_ref[layer, g * A_GROUP + j] * LOG2E
        s_c = _dot_nt(q, kc)
        m = jnp.maximum(jnp.max(s_c, axis=-1, keepdims=True), sink)
        if has_lat:
            s_l = jnp.where(valid, _dot_nt(q, kw), NEG_INF)
            m = jnp.maximum(m, jnp.max(s_l, axis=-1, keepdims=True))
        p_c = jnp.exp2(s_c - m)
        den = jnp.sum(p_c, axis=-1, keepdims=True) + jnp.exp2(sink - m)
        o = _dot(p_c.astype(BF16), vc)
        if has_lat:
            p_l = jnp.exp2(s_l - m)
            den = den + jnp.sum(p_l, axis=-1, keepdims=True)
            o = o + _dot(p_l.astype(BF16), vw)
        o_ref[:, j * HEAD_DIM:(j + 1) * HEAD_DIM] = (o / den).astype(o_ref.dtype)


def _attn_win(h, a_sink, layer, rg, *, latent, dst=None):
    gw = A_GROUP * HEAD_DIM
    tq, n_q, q0 = _q_geometry(rg, latent, 256)
    c_blk0 = rg.nl // rg.c
    in_specs = [pl.BlockSpec(memory_space=pltpu.SMEM),
                pl.BlockSpec((tq, gw), lambda b, g, qi: (q0 + b * n_q + qi, H_AQ // gw + g)),
                pl.BlockSpec((rg.c, HEAD_DIM), lambda b, g, qi: (c_blk0 + b, H_AK // HEAD_DIM + g)),
                pl.BlockSpec((rg.c, HEAD_DIM), lambda b, g, qi: (c_blk0 + b, H_AV // HEAD_DIM + g))]
    args = [a_sink, h, h, h]
    if latent:
        in_specs += [pl.BlockSpec((rg.t, HEAD_DIM), lambda b, g, qi: (b, H_AK // HEAD_DIM + g)),
                     pl.BlockSpec((rg.t, HEAD_DIM), lambda b, g, qi: (b, H_AV // HEAD_DIM + g))]
        args += [h, h]
    return _mixer_call(functools.partial(_attn_win_kernel, has_lat=latent, tq=tq, seq=rg.t, layer=layer), rg,
                       latent=latent, dst=dst, grid=(rg.b, A_KV_HEADS, n_q), in_specs=in_specs, args=args,
                       tq=tq, n_q=n_q, out_block_w=gw, name="attn_win_lat" if latent else "attn_win_ctx")


def _softmax_pv(q, keys, values):
    scores = [_dot_nt(q, k) for k in keys]
    m = functools.reduce(jnp.maximum, [jnp.max(s, axis=-1, keepdims=True) for s in scores])
    den, out = None, None
    for s, v in zip(scores, values):
        p = jnp.exp2(s - m)
        d = jnp.sum(p, axis=-1, keepdims=True)
        o = _dot(p.astype(BF16), v)
        den = d if den is None else den + d
        out = o if out is None else out + o
    return out / den


def _attn_diff_kernel(lam_ref, gsub_ref, q_ref, kc_ref, vc_ref, *rest, has_lat, lam_init):
    if has_lat:
        kl_ref, vl_ref, o_ref = rest
    else:
        _, o_ref = rest
    lf = lam_ref[...]
    lam = (jnp.exp(jnp.sum(lf[0:1] * lf[1:2], axis=-1, keepdims=True))
           - jnp.exp(jnp.sum(lf[2:3] * lf[3:4], axis=-1, keepdims=True)) + lam_init)
    keys = [kc_ref[...]] + ([kl_ref[...]] if has_lat else [])
    values = [vc_ref[...]] + ([vl_ref[...]] if has_lat else [])
    tq = q_ref.shape[0]
    sub = min(tq, Q_SUB)
    for r in range(tq // sub):
        q = q_ref[r * sub:(r + 1) * sub, :]
        first_map = lax.broadcasted_iota(jnp.int32, q.shape, 1) < B_QK_DIM
        zero = jnp.zeros_like(q)

        o = (_softmax_pv(jnp.where(first_map, q, zero), keys, values)
             - lam * _softmax_pv(jnp.where(first_map, zero, q), keys, values))
        ms = jnp.mean(o * o, axis=-1, keepdims=True)
        o = o * lax.rsqrt(ms + RMS_EPS) * gsub_ref[...] * (1.0 - lam_init)
        o_ref[r * sub:(r + 1) * sub, :] = o.astype(o_ref.dtype)


def _kv_specs(rg, col_k, col_v, latent, k_width=HEAD_DIM):
    c_blk0 = rg.nl // rg.c
    specs = [pl.BlockSpec((rg.c, k_width), lambda b, hh, qi: (c_blk0 + b, col_k(hh))),
             pl.BlockSpec((rg.c, HEAD_DIM), lambda b, hh, qi: (c_blk0 + b, col_v(hh)))]
    if latent:
        specs += [pl.BlockSpec((rg.t, k_width), lambda b, hh, qi: (b, col_k(hh))),
                  pl.BlockSpec((rg.t, HEAD_DIM), lambda b, hh, qi: (b, col_v(hh)))]
    return specs


def _q_geometry(rg, latent, tq_lat):
    if latent:
        tq = min(tq_lat, rg.t)
        return tq, rg.t // tq, 0
    return rg.c, 1, rg.nl // rg.c


def _attn_diff(h, b_lambda, b_subln_g, layer, rg, *, latent, lam_init, dst=None):
    tq, n_q, q0 = _q_geometry(rg, latent, 2048)
    in_specs = [pl.BlockSpec((None, 4, B_QK_DIM), lambda b, hh, qi: (layer, 0, 0)),
                pl.BlockSpec((None, 1, HEAD_DIM), lambda b, hh, qi: (layer, 0, 0)),
                pl.BlockSpec((tq, HEAD_DIM), lambda b, hh, qi: (q0 + b * n_q + qi, H_BQ // HEAD_DIM + hh))]
    in_specs += _kv_specs(rg, lambda hh: H_BK // HEAD_DIM + hh, lambda hh: H_BV // HEAD_DIM + hh, latent)
    n_kv = 4 if latent else 2
    args = [b_lambda, b_subln_g.reshape(b_subln_g.shape[0], 1, HEAD_DIM), h] + [h] * n_kv
    return _mixer_call(functools.partial(_attn_diff_kernel, has_lat=latent, lam_init=lam_init), rg,
                       latent=latent, dst=dst, grid=(rg.b, N_HEADS, n_q), in_specs=in_specs, args=args,
                       tq=tq, n_q=n_q, out_block_w=HEAD_DIM, name="attn_diff_lat" if latent else "attn_diff_ctx")


def _attn_mla_kernel(q_ref, knc_ref, vc_ref, krc_ref, *rest, has_lat):
    if has_lat:
        knl_ref, vl_ref, krl_ref, o_ref = rest
    else:
        _, o_ref = rest
    segs = [(knc_ref, krc_ref, vc_ref)] + ([(knl_ref, krl_ref, vl_ref)] if has_lat else [])
    keys = [jnp.concatenate([kn[...], kr[...]], axis=1) for kn, kr, _ in segs]
    values = [v[...] for _, _, v in segs]
    tq = q_ref.shape[0]
    sub = min(tq, Q_SUB)
    for r in range(tq // sub):
        rows = slice(r * sub, (r + 1) * sub)
        o_ref[rows, :] = _softmax_pv(q_ref[rows, :], keys, values).astype(o_ref.dtype)


def _attn_mla(qc, kv2, h, rg, *, latent, dst=None):
    tq, n_q, q0 = _q_geometry(rg, latent, 2048)
    c_blk0 = rg.nl // rg.c
    kr_col = H_CKR // HEAD_DIM
    kv = _kv_specs(rg, lambda hh: 2 * hh, lambda hh: 2 * hh + 1, latent)
    in_specs = [pl.BlockSpec((tq, 2 * HEAD_DIM), lambda b, hh, qi: (q0 + b * n_q + qi, hh)),
                kv[0], kv[1],
                pl.BlockSpec((rg.c, HEAD_DIM), lambda b, hh, qi: (c_blk0 + b, kr_col))]
    args = [qc, kv2, kv2, h]
    if latent:
        in_specs += [kv[2], kv[3], pl.BlockSpec((rg.t, HEAD_DIM), lambda b, hh, qi: (b, kr_col))]
        args += [kv2, kv2, h]
    return _mixer_call(functools.partial(_attn_mla_kernel, has_lat=latent), rg,
                       latent=latent, dst=dst, grid=(rg.b, N_HEADS, n_q), in_specs=in_specs, args=args,
                       tq=tq, n_q=n_q, out_block_w=HEAD_DIM, name="attn_mla_lat" if latent else "attn_mla_ctx")


def _dft_tables(n):
    idx = np.arange(n, dtype=np.int64)
    ang = (np.outer(idx, idx) % n).astype(np.float64) * (2.0 * np.pi / n)
    return jnp.asarray(np.cos(ang), F32), jnp.asarray(np.sin(ang), F32)


def _fourier_chan_kernel(z_ref, cc_ref, sc_ref, dw_ref, zc_ref, zs_ref):
    for g in range(dw_ref.shape[0]):
        cols = slice(g * HEAD_DIM, (g + 1) * HEAD_DIM)
        dw = dw_ref[g].astype(BF16)
        z = z_ref[:, cols]
        zc_ref[:, cols] = _dot(z, _dot(cc_ref[...], dw).astype(BF16)).astype(zc_ref.dtype)
        zs_ref[:, cols] = _dot(z, _dot(sc_ref[...], dw).astype(BF16)).astype(zs_ref.dtype)


def _fourier_chan(h, cos_c, sin_c, d_w, layer, rg):
    tm = rg.tm
    groups = N_HEADS // 2
    width = groups * HEAD_DIM
    assert H_DX % width == 0
    tile = pl.BlockSpec((tm, width), lambda i, s: (i, s))
    const = pl.BlockSpec((HEAD_DIM, HEAD_DIM), lambda i, s: (0, 0))
    out = jax.ShapeDtypeStruct((rg.n, N_HEADS * HEAD_DIM), BF16)
    return pl.pallas_call(
        _fourier_chan_kernel,
        out_shape=[out, out],
        grid=(rg.n // tm, N_HEADS // groups),
        in_specs=[pl.BlockSpec((tm, width), lambda i, s: (i, H_DX // width + s)), const, const,
                  pl.BlockSpec((None, groups, HEAD_DIM, HEAD_DIM), lambda i, s: (layer, s, 0, 0))],
        out_specs=[tile, tile],
        compiler_params=_cparams(2),
        name="fourier_chan",
    )(h, cos_c, sin_c, d_w)


def _fourier_seq_kernel(ct_ref, st_ref, zc_ref, zs_ref, *rest, norm):
    o_ref = rest[-1]
    o = _dot(ct_ref[...], zc_ref[...]) - _dot(st_ref[...], zs_ref[...])
    o_ref[...] = (o * norm).astype(o_ref.dtype)


def _fourier_seq(cos_t, sin_t, zc, zs, rg, *, latent, dst=None):
    length = rg.t if latent else rg.c
    tm = min(512, length)
    n_i = length // tm
    z_blk0 = 0 if latent else rg.nl // rg.c
    row0 = 0 if latent else rg.nl // tm
    width = zc.shape[1]
    dft_spec = pl.BlockSpec((tm, length), lambda b, i: (i, 0))
    z_spec = pl.BlockSpec((length, width), lambda b, i: (z_blk0 + b, 0))
    in_specs, args, aliases = [dft_spec, dft_spec, z_spec, z_spec], [cos_t, sin_t, zc, zs], {}
    if not latent:
        in_specs.append(pl.BlockSpec(memory_space=pl.ANY))
        args.append(dst)
        aliases = {len(args) - 1: 0}
    return pl.pallas_call(
        functools.partial(_fourier_seq_kernel, norm=(length * HEAD_DIM) ** -0.5),
        out_shape=jax.ShapeDtypeStruct((rg.n, width), BF16),
        grid=(rg.b, n_i),
        in_specs=in_specs,
        out_specs=pl.BlockSpec((tm, width), lambda b, i: (row0 + b * n_i + i, 0)),
        input_output_aliases=aliases,
        compiler_params=_cparams(2),
        name="fourier_seq_lat" if latent else "fourier_seq_ctx",
    )(*args)


def _prep_w_in(w_in):
    return jnp.pad(w_in, ((0, 0), (0, 0), (0, H_COLS - w_in.shape[2]))).astype(BF16)


def _prep_w_uq(w_uq):
    depth, k, _ = w_uq.shape
    w = w_uq.reshape(depth, k, N_HEADS, C_NOPE + C_ROPE)
    w = jnp.pad(w, ((0, 0), (0, 0), (0, 0), (0, 2 * HEAD_DIM - C_NOPE - C_ROPE)))
    return w.reshape(depth, k, N_HEADS * 2 * HEAD_DIM).astype(BF16)


def _in_proj_tiles():
    step = PROJ_HALVES * ROPE_TN
    src_cols = {"aq": 0, "bq": 1024, "cq": 2048, "dx": 3584, "ak": 4608, "bk": 5120, "bv": 6144,
                "ckv": 7168, "ckr": 7680}
    layout = [("cq", H_CQ, H_AQ, (ROPE_NONE, ROPE_NONE), 1.0),
              ("aq", H_AQ, H_BQ, (ROPE_A, ROPE_A), HEAD_DIM ** -0.5 * LOG2E),
              ("bq", H_BQ, H_DX, (ROPE_B, ROPE_B), B_QK_DIM ** -0.5 * LOG2E),
              ("dx", H_DX, H_AK, (ROPE_NONE, ROPE_NONE), 1.0),
              ("ak", H_AK, H_BK, (ROPE_A, ROPE_NONE), 1.0),
              ("bk", H_BK, H_BV, (ROPE_B, ROPE_B), 1.0),
              ("bv", H_BV, H_CKV, (ROPE_NONE, ROPE_NONE), 1.0),
              ("ckv", H_CKV, H_CKR, (ROPE_NONE, ROPE_NONE), 1.0),
              ("ckr", H_CKR, H_COLS, (ROPE_KR, ROPE_NONE), 1.0)]
    kinds = np.zeros((H_COLS // ROPE_TN,), np.int32)
    scales = np.ones((H_COLS // ROPE_TN,), np.float32)
    src = np.zeros((H_COLS // step,), np.int32)
    for name, lo, hi, kind, scale in layout:
        assert lo % step == 0 and hi % step == 0 and src_cols[name] % step == 0
        n = (hi - lo) // step
        kinds[lo // ROPE_TN:hi // ROPE_TN] = np.tile(np.asarray(kind, np.int32), n)
        scales[lo // ROPE_TN:hi // ROPE_TN] = scale
        src[lo // step:hi // step] = src_cols[name] // step + np.arange(n)
    return jnp.asarray(kinds), jnp.asarray(scales), jnp.asarray(src)


def _ffn(u, w_gu, w_d, layer, *, rows, tm):
    tm_up = next(t for t in (3 * tm // 2, tm) if rows % t == 0)
    hid = _ffn_up(u, w_gu, layer, rows=rows, tm=tm_up, tn=256)
    return _mm(hid, w_d, layer, rows=rows, tm=tm, tn=512, tk=w_d.shape[1] // 2, name="ffn_down")


def _mixing(u, p, layer, consts, rg, *, lam_init, need_ctx):
    cos_tab, sin_tab, (kinds_in, scales_in, src_in), dft = consts
    h = _proj(u, p["w_in"], layer, kinds_in, scales_in, src_in, cos_tab, sin_tab, rg,
              rope_kinds=(ROPE_A, ROPE_B, ROPE_KR), name="in_proj")
    n_tiles = N_HEADS * 2 * HEAD_DIM // ROPE_TN
    ident = jnp.arange(n_tiles // PROJ_HALVES, dtype=jnp.int32)
    q_scale = jnp.full((n_tiles,), (C_NOPE + C_ROPE) ** -0.5 * LOG2E, F32)
    qc = _proj(h, p["c_w_uq"], layer, jnp.full((n_tiles,), ROPE_QR, jnp.int32), q_scale, ident, cos_tab, sin_tab,
               rg, rope_kinds=(ROPE_QR,), a_col0=H_CQ, gain=p["c_q_norm_g"], name="mla_q_up")
    kv2 = _proj(h, p["c_w_ukv"], layer, jnp.zeros((n_tiles,), jnp.int32), jnp.ones((n_tiles,), F32), ident,
                cos_tab, sin_tab, rg, rope_kinds=(), a_col0=H_CKV, gain=p["c_kv_norm_g"], name="mla_kv_up")
    zc, zs = _fourier_chan(h, dft["cos_c"], dft["sin_c"], p["d_w"], layer, rg)

    ya = _attn_win(h, p["a_sink"], layer, rg, latent=True)
    yb = _attn_diff(h, p["b_lambda"], p["b_subln_g"], layer, rg, latent=True, lam_init=lam_init)
    yc = _attn_mla(qc, kv2, h, rg, latent=True)
    yd = _fourier_seq(dft["cos_t"], dft["sin_t"], zc, zs, rg, latent=True)
    if need_ctx:
        ya = _attn_win(h, p["a_sink"], layer, rg, latent=False, dst=ya)
        yb = _attn_diff(h, p["b_lambda"], p["b_subln_g"], layer, rg, latent=False, lam_init=lam_init, dst=yb)
        yc = _attn_mla(qc, kv2, h, rg, latent=False, dst=yc)
        yd = _fourier_seq(dft["cos_x"], dft["sin_x"], zc, zs, rg, latent=False, dst=yd)
    rows = rg.n if need_ctx else rg.nl
    return _out_proj([ya, yb, yc, yd], p["w_out"], layer, rows=rows, tm=rg.tm, tn=512)


def kernel(x, c, ctx, c_ctx, w_mod, b_mod, ffn1_w_gu, ffn1_w_d, ffn2_w_gu, ffn2_w_d, ln_g, ln_b, w_in, w_out,
           a_sink, b_lambda, b_subln_g, c_q_norm_g, c_kv_norm_g, c_w_uq, c_w_ukv, d_w):
    n_batch, seq, d = x.shape
    ctx_len = ctx.shape[1]
    depth = w_mod.shape[0]
    rg = _Rows(n_batch, seq, ctx_len)
    alpha = (2.0 * depth) ** 0.25

    cos_tab, sin_tab = _rope_tables(seq)
    cos_t, sin_t = _dft_tables(seq)
    cos_x, sin_x = _dft_tables(ctx_len)
    cos_c, sin_c = _dft_tables(HEAD_DIM)
    dft = {"cos_t": cos_t.astype(BF16), "sin_t": sin_t.astype(BF16),
           "cos_x": cos_x.astype(BF16), "sin_x": sin_x.astype(BF16),
           "cos_c": cos_c.astype(BF16), "sin_c": sin_c.astype(BF16)}
    consts = (cos_tab, sin_tab, _in_proj_tiles(), dft)

    n_c_rows = 8
    c_rows = jnp.concatenate([c, c_ctx[None, :], jnp.zeros((n_c_rows - n_batch - 1, d), c.dtype)], axis=0)

    p = {"w_in": _prep_w_in(w_in), "w_out": w_out, "a_sink": a_sink, "b_lambda": b_lambda,
         "b_subln_g": b_subln_g, "c_q_norm_g": c_q_norm_g, "c_kv_norm_g": c_kv_norm_g,
         "c_w_uq": _prep_w_uq(c_w_uq), "c_w_ukv": c_w_ukv, "d_w": d_w}
    mod_all = _mod_vectors(c_rows, w_mod, b_mod).reshape(depth, n_c_rows, N_MOD, d)
    mods = [mod_all[l] for l in range(depth)]
    xs, u = _modulate(x.reshape(n_batch * seq, d), ctx.reshape(n_batch * ctx_len, d), mods[0], rg, shift_idx=0)
    for l in range(depth):
        last = l == depth - 1
        lam_init = 0.8 - 0.6 * math.exp(-0.3 * l)
        mod = mods[l]
        y = _ffn(u, ffn1_w_gu, ffn1_w_d, l, rows=rg.n, tm=rg.tm)
        xs, u = _ln_res(xs, y, mod, ln_g[l, 0], ln_b[l, 0], rg, rows=rg.n, alpha=alpha,
                        gate_idx=2, gate_mul=0.5, next_mod=mod, next_shift_idx=3, name="ln_ffn1")
        y = _mixing(u, p, l, consts, rg, lam_init=lam_init, need_ctx=not last)
        rows = rg.nl if last else rg.n
        xs, u = _ln_res(xs, y, mod, ln_g[l, 1], ln_b[l, 1], rg, rows=rows, alpha=alpha,
                        gate_idx=5, gate_mul=1.0, next_mod=mod, next_shift_idx=6, name="ln_mix")
        y = _ffn(u, ffn2_w_gu, ffn2_w_d, l, rows=rows, tm=rg.tm)
        xs, u = _ln_res(xs, y, mod, ln_g[l, 2], ln_b[l, 2], rg, rows=rows, alpha=alpha,
                        gate_idx=8, gate_mul=0.5, next_mod=None if last else mods[l + 1],
                        next_shift_idx=None if last else 0, name="ln_ffn2")
    return xs[:rg.nl].reshape(n_batch, seq, d)
```

```python
import functools
import math

import numpy as np
import jax
import jax.numpy as jnp
from jax import lax
from jax.experimental import pallas as pl
from jax.experimental.pallas import tpu as pltpu

F32 = jnp.float32
BF16 = jnp.bfloat16

GRID_W = 64
HEAD_DIM = 128
WINDOW = 128
N_HEADS = 8
A_KV_HEADS = 2
A_GROUP = N_HEADS // A_KV_HEADS
B_QK_DIM = 64
C_Q_LORA = 1536
C_KV_LORA = 512
C_NOPE = 128
C_ROPE = 64
N_MOD = 9
ROPE_BASE = 10000.0
LN_EPS = 1e-5
RMS_EPS = 1e-6
NEG_INF = -1e30

H_CQ, H_AQ, H_BQ, H_DX = 0, 1536, 2560, 3584
H_AK, H_AV, H_BK, H_BV, H_CKV, H_CKR = 4608, 4864, 5120, 6144, 7168, 7680
H_COLS = 8192

ROPE_NONE, ROPE_A, ROPE_B, ROPE_KR, ROPE_QR = 0, 1, 2, 3, 4
ROPE_HALF_LANES = {ROPE_A: HEAD_DIM // 4, ROPE_B: B_QK_DIM // 4, ROPE_KR: C_ROPE // 4, ROPE_QR: C_ROPE // 4}

VMEM_LIMIT_BYTES = 56 * 1024 * 1024
ROPE_TN = 256
PROJ_HALVES = 2
Q_SUB = 256
LOG2E = math.log2(math.e)


def _cparams(n_axes):
    return pltpu.CompilerParams(dimension_semantics=("arbitrary",) * n_axes,
                                vmem_limit_bytes=VMEM_LIMIT_BYTES)


def _dot(a, b):
    return jnp.dot(a, b, preferred_element_type=F32)


def _dot_nt(a, b):
    return lax.dot_general(a, b, (((1,), (1,)), ((), ())), preferred_element_type=F32)


def _mm_kernel(a_ref, b_ref, o_ref, acc_ref, *, nk):
    k, j = pl.program_id(1), pl.program_id(2)

    def part():
        return _dot(a_ref[...], b_ref[...].astype(BF16))

    @pl.when(k == 0)
    def _():
        acc_ref[j] = part()

    if nk > 2:
        @pl.when((k > 0) & (k < nk - 1))
        def _():
            acc_ref[j] += part()

    @pl.when(k == nk - 1)
    def _():
        o_ref[...] = (acc_ref[j] + part()).astype(o_ref.dtype)


def _mm(a, b, layer, *, rows, tm, tn, tk, name):
    _, kdim, n = b.shape
    assert rows % tm == 0 and n % tn == 0 and kdim % tk == 0 and kdim // tk >= 2
    nk = kdim // tk
    return pl.pallas_call(
        functools.partial(_mm_kernel, nk=nk),
        out_shape=jax.ShapeDtypeStruct((rows, n), BF16),
        grid=(rows // tm, nk, n // tn),
        in_specs=[pl.BlockSpec((tm, tk), lambda i, k, j: (i, k)),
                  pl.BlockSpec((None, tk, tn), lambda i, k, j: (layer, k, j))],
        out_specs=pl.BlockSpec((tm, tn), lambda i, k, j: (i, jnp.where(k == nk - 1, j, 0))),
        scratch_shapes=[pltpu.VMEM((n // tn, tm, tn), F32)],
        compiler_params=_cparams(3),
        name=name,
    )(a, b)


def _out_proj_kernel(*refs):
    *y_refs, w_ref, o_ref = refs
    width = y_refs[0].shape[1]
    acc = None
    for m, y_ref in enumerate(y_refs):
        part = _dot(y_ref[...], w_ref[m * width:(m + 1) * width, :].astype(BF16))
        acc = part if acc is None else acc + part
    o_ref[...] = acc.astype(o_ref.dtype)


def _out_proj(ys, w_out, layer, *, rows, tm, tn):
    _, kdim, n = w_out.shape
    width = ys[0].shape[1]
    assert kdim == width * len(ys)
    y_spec = pl.BlockSpec((tm, width), lambda i, j: (i, 0))
    return pl.pallas_call(
        _out_proj_kernel,
        out_shape=jax.ShapeDtypeStruct((rows, n), BF16),
        grid=(rows // tm, n // tn),
        in_specs=[y_spec] * len(ys) + [pl.BlockSpec((None, kdim, tn), lambda i, j: (layer, 0, j))],
        out_specs=pl.BlockSpec((tm, tn), lambda i, j: (i, j)),
        compiler_params=_cparams(2),
        name="out_proj",
    )(*ys, w_out)


def _mod_kernel(c_ref, w_ref, b_ref, o_ref):
    c = c_ref[...]
    sc = (c * jax.nn.sigmoid(c)).astype(BF16)
    o_ref[...] = _dot(sc, w_ref[...].astype(BF16)) + b_ref[...]


def _mod_vectors(c_rows, w_mod, b_mod):
    depth, d, n = w_mod.shape
    r = c_rows.shape[0]
    tn = 1024
    return pl.pallas_call(
        _mod_kernel,
        out_shape=jax.ShapeDtypeStruct((depth, r, n), F32),
        grid=(depth, n // tn),
        in_specs=[pl.BlockSpec((r, d), lambda l, j: (0, 0)),
                  pl.BlockSpec((None, d, tn), lambda l, j: (l, 0, j)),
                  pl.BlockSpec((None, 1, tn), lambda l, j: (l, 0, j))],
        out_specs=pl.BlockSpec((None, r, tn), lambda l, j: (l, 0, j)),
        compiler_params=_cparams(2),
        name="mod_vectors",
    )(c_rows, w_mod, b_mod.reshape(depth, 1, n))


def _seg_index(i, n_lat_tiles, tiles_per_batch, n_batch):
    return jnp.where(i < n_lat_tiles, i // tiles_per_batch, n_batch)


def _modulate_kernel(x_ref, c_ref, mod_ref, xs_ref, u_ref, *, shift_idx, n_lat_tiles):
    shift = mod_ref[shift_idx:shift_idx + 1, :]
    scale = mod_ref[shift_idx + 1:shift_idx + 2, :]

    def emit(src_ref):
        x = src_ref[...]
        xs_ref[...] = x
        u_ref[...] = (x * (1.0 + scale) + shift).astype(u_ref.dtype)

    @pl.when(pl.program_id(0) < n_lat_tiles)
    def _():
        emit(x_ref)

    @pl.when(pl.program_id(0) >= n_lat_tiles)
    def _():
        emit(c_ref)


def _res_ln(x, y, mod_ref, g_ref, b_ref, *, alpha, gate_idx, gate_mul):
    gate = mod_ref[gate_idx:gate_idx + 1, :]
    z = alpha * x + (gate_mul * gate) * y.astype(F32)
    mu = jnp.mean(z, axis=-1, keepdims=True)
    zc = z - mu
    var = jnp.mean(zc * zc, axis=-1, keepdims=True)
    return zc * lax.rsqrt(var + LN_EPS) * g_ref[...] + b_ref[...]


def _modulated(xn, nmod_ref, shift_idx):
    shift = nmod_ref[shift_idx:shift_idx + 1, :]
    scale = nmod_ref[shift_idx + 1:shift_idx + 2, :]
    return (xn * (1.0 + scale) + shift).astype(BF16)


def _ln_res_kernel(x_ref, y_ref, mod_ref, g_ref, b_ref, *rest, alpha, gate_idx, gate_mul, next_shift_idx):
    xn = _res_ln(x_ref[...], y_ref[...], mod_ref, g_ref, b_ref, alpha=alpha, gate_idx=gate_idx, gate_mul=gate_mul)
    if next_shift_idx is None:
        xo_ref, = rest
        xo_ref[...] = xn
    else:
        nmod_ref, xo_ref, u_ref = rest
        xo_ref[...] = xn
        u_ref[...] = _modulated(xn, nmod_ref, next_shift_idx)


class _Rows:
    def __init__(self, n_batch, seq, ctx_len):
        self.b, self.t, self.c = n_batch, seq, ctx_len
        self.nl, self.nc = n_batch * seq, n_batch * ctx_len
        self.n = self.nl + self.nc
        self.tr = math.gcd(256, math.gcd(seq, ctx_len))
        self.tm = math.gcd(1024, math.gcd(seq, self.nc))

    def mod_spec(self, tile, d):
        n_lat_tiles, per_batch, nb = self.nl // tile, self.t // tile, self.b
        return pl.BlockSpec((None, N_MOD, d),
                            lambda i, *_: (_seg_index(i, n_lat_tiles, per_batch, nb), 0, 0))


def _modulate(x, ctx, mod, rg, *, shift_idx):
    d = x.shape[1]
    tr = rg.tr
    n_lat_tiles, n_ctx_tiles = rg.nl // tr, rg.nc // tr
    row_spec = pl.BlockSpec((tr, d), lambda i: (i, 0))
    x_spec = pl.BlockSpec((tr, d), lambda i: (jnp.minimum(i, n_lat_tiles - 1), 0))
    c_spec = pl.BlockSpec((tr, d), lambda i: (jnp.clip(i - n_lat_tiles, 0, n_ctx_tiles - 1), 0))
    return pl.pallas_call(
        functools.partial(_modulate_kernel, shift_idx=shift_idx, n_lat_tiles=n_lat_tiles),
        out_shape=[jax.ShapeDtypeStruct((rg.n, d), F32), jax.ShapeDtypeStruct((rg.n, d), BF16)],
        grid=(rg.n // tr,),
        in_specs=[x_spec, c_spec, rg.mod_spec(tr, d)],
        out_specs=[row_spec, row_spec],
        compiler_params=_cparams(1),
        name="modulate",
    )(x, ctx, mod)


def _ln_res(x, y, mod, ln_g, ln_b, rg, *, rows, alpha, gate_idx, gate_mul, next_mod, next_shift_idx, name,
            out_rows=None):
    d = x.shape[1]
    tr = rg.tr
    row_spec = pl.BlockSpec((tr, d), lambda i: (i, 0))
    vec_spec = pl.BlockSpec((1, d), lambda i: (0, 0))
    out_shape = [jax.ShapeDtypeStruct((out_rows or rows, d), F32)]
    out_specs = [row_spec]
    in_specs = [row_spec, row_spec, rg.mod_spec(tr, d), vec_spec, vec_spec]
    args = [x, y, mod, ln_g.reshape(1, d), ln_b.reshape(1, d)]
    if next_shift_idx is not None:
        in_specs.append(rg.mod_spec(tr, d))
        args.append(next_mod)
        out_shape.append(jax.ShapeDtypeStruct((rows, d), BF16))
        out_specs.append(row_spec)
    res = pl.pallas_call(
        functools.partial(_ln_res_kernel, alpha=alpha, gate_idx=gate_idx, gate_mul=gate_mul,
                          next_shift_idx=next_shift_idx),
        out_shape=out_shape,
        grid=(rows // tr,),
        in_specs=in_specs,
        out_specs=out_specs,
        compiler_params=_cparams(1),
        name=name,
    )(*args)
    return (res[0], res[1]) if next_shift_idx is not None else (res[0], None)


def _ffn_up_kernel(a_ref, wg_ref, wu_ref, o_ref):
    a = a_ref[...]
    g = _dot(a, wg_ref[...].astype(BF16))
    up = _dot(a, wu_ref[...].astype(BF16))
    o_ref[...] = (g * jax.nn.sigmoid(g) * up).astype(o_ref.dtype)


def _ffn_up(u, w_gu, layer, *, rows, tm, tn):
    _, d, two_ff = w_gu.shape
    ff = two_ff // 2
    up0 = ff // tn
    return pl.pallas_call(
        _ffn_up_kernel,
        out_shape=jax.ShapeDtypeStruct((rows, ff), BF16),
        grid=(rows // tm, ff // tn),
        in_specs=[pl.BlockSpec((tm, d), lambda i, j: (i, 0)),
                  pl.BlockSpec((None, d, tn), lambda i, j: (layer, 0, j)),
                  pl.BlockSpec((None, d, tn), lambda i, j: (layer, 0, up0 + j))],
        out_specs=pl.BlockSpec((tm, tn), lambda i, j: (i, j)),
        compiler_params=_cparams(2),
        name="ffn_up",
    )(u, w_gu, w_gu)


def _ffn_up_ln_kernel(x_ref, y_ref, mod_ref, nmod_ref, g_ref, b_ref, u0_ref, _xs_head, wg_ref, wu_ref,
                      hid_ref, xo_ref, a_even, a_odd, *, alpha, gate_idx, gate_mul, shift_idx, chunk):
    i, j = pl.program_id(0), pl.program_id(1)

    @pl.when((i == 0) & (j == 0))
    def _():
        a_even[...] = u0_ref[...]

    def step(a_cur, a_next):
        xn = _res_ln(x_ref[...], y_ref[...], mod_ref, g_ref, b_ref,
                     alpha=alpha, gate_idx=gate_idx, gate_mul=gate_mul)
        xo_ref[...] = xn
        a_next[pl.ds(pl.multiple_of(j * chunk, chunk), chunk), :] = _modulated(xn, nmod_ref, shift_idx)
        a = a_cur[...]
        g = _dot(a, wg_ref[...].astype(BF16))
        up = _dot(a, wu_ref[...].astype(BF16))
        hid_ref[...] = (g * jax.nn.sigmoid(g) * up).astype(hid_ref.dtype)

    @pl.when(i % 2 == 0)
    def _():
        step(a_even, a_odd)

    @pl.when(i % 2 == 1)
    def _():
        step(a_odd, a_even)


def _ffn_up_ln(ln, w_gu, layer, rg, *, rows, tm, tn):
    _, d, two_ff = w_gu.shape
    ff = two_ff // 2
    n_i, n_j = rows // tm, ff // tn
    chunk = tm // n_j
    assert rows % tm == 0 and tm % n_j == 0 and chunk % 16 == 0 and rg.t % chunk == 0 and rg.nl % tm == 0
    xs_head, u0 = _ln_res(ln["x"], ln["y"], ln["mod"], ln["g"], ln["b"], rg, rows=tm, out_rows=rows,
                          alpha=ln["alpha"], gate_idx=ln["gate_idx"], gate_mul=ln["gate_mul"],
                          next_mod=ln["next_mod"], next_shift_idx=ln["shift_idx"], name="ln_head")

    def chunk_block(i, j):
        return jnp.minimum(i + 1, n_i - 1) * n_j + j

    def seg(i, j):
        row0 = chunk_block(i, j) * chunk
        return jnp.where(row0 < rg.nl, row0 // rg.t, rg.b)

    chunk_spec = pl.BlockSpec((chunk, d), lambda i, j: (chunk_block(i, j), 0))
    mod_spec = pl.BlockSpec((None, N_MOD, d), lambda i, j: (seg(i, j), 0, 0))
    vec_spec = pl.BlockSpec((1, d), lambda i, j: (0, 0))
    up0 = ff // tn
    hid, xs_new = pl.pallas_call(
        functools.partial(_ffn_up_ln_kernel, alpha=ln["alpha"], gate_idx=ln["gate_idx"], gate_mul=ln["gate_mul"],
                          shift_idx=ln["shift_idx"], chunk=chunk),
        out_shape=[jax.ShapeDtypeStruct((rows, ff), BF16), jax.ShapeDtypeStruct((rows, d), F32)],
        grid=(n_i, n_j),
        in_specs=[chunk_spec, chunk_spec, mod_spec, mod_spec, vec_spec, vec_spec,
                  pl.BlockSpec((tm, d), lambda i, j: (0, 0), pipeline_mode=pl.Buffered(1)),
                  pl.BlockSpec(memory_space=pl.ANY),
                  pl.BlockSpec((None, d, tn), lambda i, j: (layer, 0, j)),
                  pl.BlockSpec((None, d, tn), lambda i, j: (layer, 0, up0 + j))],
        out_specs=[pl.BlockSpec((tm, tn), lambda i, j: (i, j)), chunk_spec],
        scratch_shapes=[pltpu.VMEM((tm, d), BF16), pltpu.VMEM((tm, d), BF16)],
        input_output_aliases={7: 1},
        compiler_params=_cparams(2),
        name="ffn_up_ln",
    )(ln["x"], ln["y"], ln["mod"], ln["next_mod"], ln["g"].reshape(1, d), ln["b"].reshape(1, d), u0, xs_head,
      w_gu, w_gu)
    return hid, xs_new


def _rope_tile(x, cos, sin, half):
    lane = lax.broadcasted_iota(jnp.int32, x.shape, 1)
    first = (lane % (2 * half)) < half
    width = x.shape[1]
    partner = jnp.where(first, pltpu.roll(x, width - half, 1), pltpu.roll(x, half, 1))
    return x * cos + partner * sin


def _proj_kernel(kind_ref, src_ref, scale_ref, a_ref, b_ref, *rest, has_gain, n_lat_tiles, rope_kinds):
    if has_gain:
        gain_ref, *tab_refs, o_ref, an_ref = rest
    else:
        *tab_refs, o_ref = rest
    tables = {kind: (tab_refs[2 * n], tab_refs[2 * n + 1]) for n, kind in enumerate(rope_kinds)}
    i, j = pl.program_id(0), pl.program_id(1)

    if has_gain:
        @pl.when(j == 0)
        def _():
            x = a_ref[...].astype(F32)
            ms = jnp.mean(x * x, axis=-1, keepdims=True)
            an_ref[...] = (x * lax.rsqrt(ms + RMS_EPS) * gain_ref[...]).astype(BF16)
    lhs_ref = an_ref if has_gain else a_ref

    for half in range(PROJ_HALVES):
        cols = slice(half * ROPE_TN, (half + 1) * ROPE_TN)
        tile = PROJ_HALVES * j + half
        kind = jnp.where(i < n_lat_tiles, kind_ref[tile], ROPE_NONE)
        out_scale = scale_ref[tile]

        def acc():
            return _dot(lhs_ref[...], b_ref[:, cols].astype(BF16)) * out_scale

        @pl.when(kind == ROPE_NONE)
        def _():
            o_ref[:, cols] = acc().astype(o_ref.dtype)

        for rope_kind, (cos_ref, sin_ref) in tables.items():
            @pl.when(kind == rope_kind)
            def _(cos_ref=cos_ref, sin_ref=sin_ref, rope_kind=rope_kind):
                roped = _rope_tile(acc(), cos_ref[...], sin_ref[...], ROPE_HALF_LANES[rope_kind])
                o_ref[:, cols] = roped.astype(o_ref.dtype)


def _proj(a, w, layer, kinds, scales, src_tiles, cos_tab, sin_tab, rg, *, rows, rope_kinds, a_col0=0, gain=None,
          name):
    _, kdim, n = w.shape
    tm, tn = rg.tm, PROJ_HALVES * ROPE_TN
    assert a_col0 % kdim == 0 and n % tn == 0
    a_blk = a_col0 // kdim
    n_lat_tiles = rg.nl // tm
    t_tiles = rg.t // tm
    has_gain = gain is not None

    def tab_spec(kind):
        return pl.BlockSpec((None, tm, ROPE_TN), lambda i, j, kr, sr: (kind, i % t_tiles, 0))

    in_specs = [pl.BlockSpec(memory_space=pltpu.SMEM),
                pl.BlockSpec((tm, kdim), lambda i, j, kr, sr: (i, a_blk)),
                pl.BlockSpec((None, kdim, tn), lambda i, j, kr, sr: (layer, 0, sr[j]))]
    args = [scales, a, w]
    scratch = []
    if has_gain:
        in_specs.append(pl.BlockSpec((None, 1, kdim), lambda i, j, kr, sr: (layer, 0, 0)))
        args.append(gain.reshape(gain.shape[0], 1, kdim))
        scratch.append(pltpu.VMEM((tm, kdim), BF16))
    for kind in rope_kinds:
        in_specs += [tab_spec(kind), tab_spec(kind)]
        args += [cos_tab, sin_tab]
    return pl.pallas_call(
        functools.partial(_proj_kernel, has_gain=has_gain, n_lat_tiles=n_lat_tiles, rope_kinds=tuple(rope_kinds)),
        out_shape=jax.ShapeDtypeStruct((rows, n), BF16),
        grid_spec=pltpu.PrefetchScalarGridSpec(
            num_scalar_prefetch=2,
            grid=(rows // tm, n // tn),
            in_specs=in_specs,
            out_specs=pl.BlockSpec((tm, tn), lambda i, j, kr, sr: (i, j)),
            scratch_shapes=scratch),
        compiler_params=_cparams(2),
        name=name,
    )(kinds, src_tiles, *args)


def _rope_tables(seq):
    t = np.arange(seq)
    row, col = (t // GRID_W).astype(np.float64), (t % GRID_W).astype(np.float64)

    def pattern(rot_dim):
        axis_dim = rot_dim // 2
        inv = ROPE_BASE ** (-np.arange(0, axis_dim, 2, dtype=np.float64) / axis_dim)
        ar, ac = row[:, None] * inv[None, :], col[:, None] * inv[None, :]
        cos = np.concatenate([np.cos(ar), np.cos(ar), np.cos(ac), np.cos(ac)], axis=1)
        sin = np.concatenate([-np.sin(ar), np.sin(ar), -np.sin(ac), np.sin(ac)], axis=1)
        return cos, sin

    cos = np.ones((5, seq, ROPE_TN), np.float64)
    sin = np.zeros((5, seq, ROPE_TN), np.float64)
    c128, s128 = pattern(HEAD_DIM)
    c64, s64 = pattern(2 * 32)
    cos[ROPE_A], sin[ROPE_A] = np.tile(c128, (1, 2)), np.tile(s128, (1, 2))
    cos[ROPE_B], sin[ROPE_B] = np.tile(c64, (1, 4)), np.tile(s64, (1, 4))
    cos[ROPE_KR, :, :64], sin[ROPE_KR, :, :64] = c64, s64
    cos[ROPE_QR, :, 128:192], sin[ROPE_QR, :, 128:192] = c64, s64
    return jnp.asarray(cos, F32), jnp.asarray(sin, F32)


def _mixer_call(body, rg, *, latent, dst, out_rows, grid, in_specs, args, tq, n_q, out_block_w, name):
    row0 = 0 if latent else rg.nl // tq
    aliases = {}
    if not latent:
        in_specs = in_specs + [pl.BlockSpec(memory_space=pl.ANY)]
        args = args + [dst]
        aliases = {len(args) - 1: 0}
    return pl.pallas_call(
        body,
        out_shape=jax.ShapeDtypeStruct((out_rows, N_HEADS * HEAD_DIM), BF16),
        grid=grid,
        in_specs=in_specs,
        out_specs=pl.BlockSpec((tq, out_block_w), lambda b, hh, qi: (row0 + b * n_q + qi, hh)),
        input_output_aliases=aliases,
        compiler_params=_cparams(3),
        name=name,
    )(*args)


def _attn_win_kernel(sink_ref, q_ref, kc_ref, vc_ref, *rest, has_lat, tq, seq, layer):
    if has_lat:
        kl_ref, vl_ref, o_ref = rest
    else:
        _, o_ref = rest
    g, qi = pl.program_id(1), pl.program_id(2)
    kc, vc = kc_ref[...], vc_ref[...]
    if has_lat:
        win = min(seq, tq + 2 * WINDOW)
        ws = pl.multiple_of(jnp.clip(qi * tq - WINDOW, 0, seq - win), WINDOW)
        kw, vw = kl_ref[pl.ds(ws, win), :], vl_ref[pl.ds(ws, win), :]
        qpos = qi * tq + lax.broadcasted_iota(jnp.int32, (tq, win), 0)
        kpos = ws + lax.broadcasted_iota(jnp.int32, (tq, win), 1)
        valid = jnp.abs(qpos - kpos) <= WINDOW
    for j in range(A_GROUP):
        q = q_ref[:, j * HEAD_DIM:(j + 1) * HEAD_DIM]
        sink = sink_ref[layer, g * A_GROUP + j] * LOG2E
        s_c = _dot_nt(q, kc)
        m = jnp.maximum(jnp.max(s_c, axis=-1, keepdims=True), sink)
        if has_lat:
            s_l = jnp.where(valid, _dot_nt(q, kw), NEG_INF)
            m = jnp.maximum(m, jnp.max(s_l, axis=-1, keepdims=True))
        p_c = jnp.exp2(s_c - m)
        den = jnp.sum(p_c, axis=-1, keepdims=True) + jnp.exp2(sink - m)
        o = _dot(p_c.astype(BF16), vc)
        if has_lat:
            p_l = jnp.exp2(s_l - m)
            den = den + jnp.sum(p_l, axis=-1, keepdims=True)
            o = o + _dot(p_l.astype(BF16), vw)
        o_ref[:, j * HEAD_DIM:(j + 1) * HEAD_DIM] = (o / den).astype(o_ref.dtype)


def _attn_win(h, a_sink, layer, rg, *, latent, out_rows, dst=None):
    gw = A_GROUP * HEAD_DIM
    tq, n_q, q0 = _q_geometry(rg, latent, 256)
    c_blk0 = rg.nl // rg.c
    in_specs = [pl.BlockSpec(memory_space=pltpu.SMEM),
                pl.BlockSpec((tq, gw), lambda b, g, qi: (q0 + b * n_q + qi, H_AQ // gw + g)),
                pl.BlockSpec((rg.c, HEAD_DIM), lambda b, g, qi: (c_blk0 + b, H_AK // HEAD_DIM + g)),
                pl.BlockSpec((rg.c, HEAD_DIM), lambda b, g, qi: (c_blk0 + b, H_AV // HEAD_DIM + g))]
    args = [a_sink, h, h, h]
    if latent:
        in_specs += [pl.BlockSpec((rg.t, HEAD_DIM), lambda b, g, qi: (b, H_AK // HEAD_DIM + g)),
                     pl.BlockSpec((rg.t, HEAD_DIM), lambda b, g, qi: (b, H_AV // HEAD_DIM + g))]
        args += [h, h]
    return _mixer_call(functools.partial(_attn_win_kernel, has_lat=latent, tq=tq, seq=rg.t, layer=layer), rg,
                       latent=latent, dst=dst, out_rows=out_rows, grid=(rg.b, A_KV_HEADS, n_q), in_specs=in_specs, args=args,
                       tq=tq, n_q=n_q, out_block_w=gw, name="attn_win_lat" if latent else "attn_win_ctx")


def _softmax_pv(q, keys, values):
    scores = [_dot_nt(q, k) for k in keys]
    m = functools.reduce(jnp.maximum, [jnp.max(s, axis=-1, keepdims=True) for s in scores])
    den, out = None, None
    for s, v in zip(scores, values):
        p = jnp.exp2(s - m)
        d = jnp.sum(p, axis=-1, keepdims=True)
        o = _dot(p.astype(BF16), v)
        den = d if den is None else den + d
        out = o if out is None else out + o
    return out / den


def _attn_diff_kernel(lam_ref, gsub_ref, q_ref, kc_ref, vc_ref, *rest, has_lat, lam_init):
    if has_lat:
        kl_ref, vl_ref, o_ref = rest
    else:
        _, o_ref = rest
    lf = lam_ref[...]
    lam = (jnp.exp(jnp.sum(lf[0:1] * lf[1:2], axis=-1, keepdims=True))
           - jnp.exp(jnp.sum(lf[2:3] * lf[3:4], axis=-1, keepdims=True)) + lam_init)
    keys = [kc_ref[...]] + ([kl_ref[...]] if has_lat else [])
    values = [vc_ref[...]] + ([vl_ref[...]] if has_lat else [])
    tq = q_ref.shape[0]
    sub = min(tq, Q_SUB)
    for r in range(tq // sub):
        q = q_ref[r * sub:(r + 1) * sub, :]
        first_map = lax.broadcasted_iota(jnp.int32, q.shape, 1) < B_QK_DIM
        zero = jnp.zeros_like(q)

        o = (_softmax_pv(jnp.where(first_map, q, zero), keys, values)
             - lam * _softmax_pv(jnp.where(first_map, zero, q), keys, values))
        ms = jnp.mean(o * o, axis=-1, keepdims=True)
        o = o * lax.rsqrt(ms + RMS_EPS) * gsub_ref[...] * (1.0 - lam_init)
        o_ref[r * sub:(r + 1) * sub, :] = o.astype(o_ref.dtype)


def _kv_specs(rg, col_k, col_v, latent, k_width=HEAD_DIM):
    c_blk0 = rg.nl // rg.c
    specs = [pl.BlockSpec((rg.c, k_width), lambda b, hh, qi: (c_blk0 + b, col_k(hh))),
             pl.BlockSpec((rg.c, HEAD_DIM), lambda b, hh, qi: (c_blk0 + b, col_v(hh)))]
    if latent:
        specs += [pl.BlockSpec((rg.t, k_width), lambda b, hh, qi: (b, col_k(hh))),
                  pl.BlockSpec((rg.t, HEAD_DIM), lambda b, hh, qi: (b, col_v(hh)))]
    return specs


def _q_geometry(rg, latent, tq_lat):
    if latent:
        tq = min(tq_lat, rg.t)
        return tq, rg.t // tq, 0
    return rg.c, 1, rg.nl // rg.c


def _attn_diff(h, b_lambda, b_subln_g, layer, rg, *, latent, lam_init, out_rows, dst=None):
    tq, n_q, q0 = _q_geometry(rg, latent, 2048)
    in_specs = [pl.BlockSpec((None, 4, B_QK_DIM), lambda b, hh, qi: (layer, 0, 0)),
                pl.BlockSpec((None, 1, HEAD_DIM), lambda b, hh, qi: (layer, 0, 0)),
                pl.BlockSpec((tq, HEAD_DIM), lambda b, hh, qi: (q0 + b * n_q + qi, H_BQ // HEAD_DIM + hh))]
    in_specs += _kv_specs(rg, lambda hh: H_BK // HEAD_DIM + hh, lambda hh: H_BV // HEAD_DIM + hh, latent)
    n_kv = 4 if latent else 2
    args = [b_lambda, b_subln_g.reshape(b_subln_g.shape[0], 1, HEAD_DIM), h] + [h] * n_kv
    return _mixer_call(functools.partial(_attn_diff_kernel, has_lat=latent, lam_init=lam_init), rg,
                       latent=latent, dst=dst, out_rows=out_rows, grid=(rg.b, N_HEADS, n_q), in_specs=in_specs, args=args,
                       tq=tq, n_q=n_q, out_block_w=HEAD_DIM, name="attn_diff_lat" if latent else "attn_diff_ctx")


def _attn_mla_kernel(q_ref, knc_ref, vc_ref, krc_ref, *rest, has_lat):
    if has_lat:
        knl_ref, vl_ref, krl_ref, o_ref = rest
    else:
        _, o_ref = rest
    segs = [(knc_ref, krc_ref, vc_ref)] + ([(knl_ref, krl_ref, vl_ref)] if has_lat else [])
    keys = [jnp.concatenate([kn[...], kr[...]], axis=1) for kn, kr, _ in segs]
    values = [v[...] for _, _, v in segs]
    tq = q_ref.shape[0]
    sub = min(tq, Q_SUB)
    for r in range(tq // sub):
        rows = slice(r * sub, (r + 1) * sub)
        o_ref[rows, :] = _softmax_pv(q_ref[rows, :], keys, values).astype(o_ref.dtype)


def _attn_mla(qc, kv2, h, rg, *, latent, out_rows, dst=None):
    tq, n_q, q0 = _q_geometry(rg, latent, 2048)
    c_blk0 = rg.nl // rg.c
    kr_col = H_CKR // HEAD_DIM
    kv = _kv_specs(rg, lambda hh: 2 * hh, lambda hh: 2 * hh + 1, latent)
    in_specs = [pl.BlockSpec((tq, 2 * HEAD_DIM), lambda b, hh, qi: (q0 + b * n_q + qi, hh)),
                kv[0], kv[1],
                pl.BlockSpec((rg.c, HEAD_DIM), lambda b, hh, qi: (c_blk0 + b, kr_col))]
    args = [qc, kv2, kv2, h]
    if latent:
        in_specs += [kv[2], kv[3], pl.BlockSpec((rg.t, HEAD_DIM), lambda b, hh, qi: (b, kr_col))]
        args += [kv2, kv2, h]
    return _mixer_call(functools.partial(_attn_mla_kernel, has_lat=latent), rg,
                       latent=latent, dst=dst, out_rows=out_rows, grid=(rg.b, N_HEADS, n_q), in_specs=in_specs, args=args,
                       tq=tq, n_q=n_q, out_block_w=HEAD_DIM, name="attn_mla_lat" if latent else "attn_mla_ctx")


def _dft_tables(n):
    idx = np.arange(n, dtype=np.int64)
    ang = (np.outer(idx, idx) % n).astype(np.float64) * (2.0 * np.pi / n)
    return jnp.asarray(np.cos(ang), F32), jnp.asarray(np.sin(ang), F32)


def _fourier_chan_kernel(z_ref, cc_ref, sc_ref, dw_ref, zc_ref, zs_ref):
    for g in range(dw_ref.shape[0]):
        cols = slice(g * HEAD_DIM, (g + 1) * HEAD_DIM)
        dw = dw_ref[g].astype(BF16)
        z = z_ref[:, cols]
        zc_ref[:, cols] = _dot(z, _dot(cc_ref[...], dw).astype(BF16)).astype(zc_ref.dtype)
        zs_ref[:, cols] = _dot(z, _dot(sc_ref[...], dw).astype(BF16)).astype(zs_ref.dtype)


def _fourier_chan(h, cos_c, sin_c, d_w, layer, rg, *, rows):
    tm = rg.tm
    groups = N_HEADS // 2
    width = groups * HEAD_DIM
    assert H_DX % width == 0
    tile = pl.BlockSpec((tm, width), lambda i, s: (i, s))
    const = pl.BlockSpec((HEAD_DIM, HEAD_DIM), lambda i, s: (0, 0))
    out = jax.ShapeDtypeStruct((rows, N_HEADS * HEAD_DIM), BF16)
    return pl.pallas_call(
        _fourier_chan_kernel,
        out_shape=[out, out],
        grid=(rows // tm, N_HEADS // groups),
        in_specs=[pl.BlockSpec((tm, width), lambda i, s: (i, H_DX // width + s)), const, const,
                  pl.BlockSpec((None, groups, HEAD_DIM, HEAD_DIM), lambda i, s: (layer, s, 0, 0))],
        out_specs=[tile, tile],
        compiler_params=_cparams(2),
        name="fourier_chan",
    )(h, cos_c, sin_c, d_w)


def _fourier_seq_kernel(ct_ref, st_ref, zc_ref, zs_ref, *rest, norm):
    o_ref = rest[-1]
    o = _dot(ct_ref[...], zc_ref[...]) - _dot(st_ref[...], zs_ref[...])
    o_ref[...] = (o * norm).astype(o_ref.dtype)


def _fourier_seq(cos_t, sin_t, zc, zs, rg, *, latent, out_rows, dst=None):
    length = rg.t if latent else rg.c
    tm = min(512, length)
    n_i = length // tm
    z_blk0 = 0 if latent else rg.nl // rg.c
    row0 = 0 if latent else rg.nl // tm
    width = zc.shape[1]
    dft_spec = pl.BlockSpec((tm, length), lambda b, i: (i, 0))
    z_spec = pl.BlockSpec((length, width), lambda b, i: (z_blk0 + b, 0))
    in_specs, args, aliases = [dft_spec, dft_spec, z_spec, z_spec], [cos_t, sin_t, zc, zs], {}
    if not latent:
        in_specs.append(pl.BlockSpec(memory_space=pl.ANY))
        args.append(dst)
        aliases = {len(args) - 1: 0}
    return pl.pallas_call(
        functools.partial(_fourier_seq_kernel, norm=(length * HEAD_DIM) ** -0.5),
        out_shape=jax.ShapeDtypeStruct((out_rows, width), BF16),
        grid=(rg.b, n_i),
        in_specs=in_specs,
        out_specs=pl.BlockSpec((tm, width), lambda b, i: (row0 + b * n_i + i, 0)),
        input_output_aliases=aliases,
        compiler_params=_cparams(2),
        name="fourier_seq_lat" if latent else "fourier_seq_ctx",
    )(*args)


def _prep_w_in(w_in):
    return jnp.pad(w_in, ((0, 0), (0, 0), (0, H_COLS - w_in.shape[2]))).astype(BF16)


def _prep_w_uq(w_uq):
    depth, k, _ = w_uq.shape
    w = w_uq.reshape(depth, k, N_HEADS, C_NOPE + C_ROPE)
    w = jnp.pad(w, ((0, 0), (0, 0), (0, 0), (0, 2 * HEAD_DIM - C_NOPE - C_ROPE)))
    return w.reshape(depth, k, N_HEADS * 2 * HEAD_DIM).astype(BF16)


def _in_proj_tiles():
    step = PROJ_HALVES * ROPE_TN
    src_cols = {"aq": 0, "bq": 1024, "cq": 2048, "dx": 3584, "ak": 4608, "bk": 5120, "bv": 6144,
                "ckv": 7168, "ckr": 7680}
    layout = [("cq", H_CQ, H_AQ, (ROPE_NONE, ROPE_NONE), 1.0),
              ("aq", H_AQ, H_BQ, (ROPE_A, ROPE_A), HEAD_DIM ** -0.5 * LOG2E),
              ("bq", H_BQ, H_DX, (ROPE_B, ROPE_B), B_QK_DIM ** -0.5 * LOG2E),
              ("dx", H_DX, H_AK, (ROPE_NONE, ROPE_NONE), 1.0),
              ("ak", H_AK, H_BK, (ROPE_A, ROPE_NONE), 1.0),
              ("bk", H_BK, H_BV, (ROPE_B, ROPE_B), 1.0),
              ("bv", H_BV, H_CKV, (ROPE_NONE, ROPE_NONE), 1.0),
              ("ckv", H_CKV, H_CKR, (ROPE_NONE, ROPE_NONE), 1.0),
              ("ckr", H_CKR, H_COLS, (ROPE_KR, ROPE_NONE), 1.0)]
    kinds = np.zeros((H_COLS // ROPE_TN,), np.int32)
    scales = np.ones((H_COLS // ROPE_TN,), np.float32)
    src = np.zeros((H_COLS // step,), np.int32)
    for name, lo, hi, kind, scale in layout:
        assert lo % step == 0 and hi % step == 0 and src_cols[name] % step == 0
        n = (hi - lo) // step
        kinds[lo // ROPE_TN:hi // ROPE_TN] = np.tile(np.asarray(kind, np.int32), n)
        scales[lo // ROPE_TN:hi // ROPE_TN] = scale
        src[lo // step:hi // step] = src_cols[name] // step + np.arange(n)
    return jnp.asarray(kinds), jnp.asarray(scales), jnp.asarray(src)


def _ffn(u, w_gu, w_d, layer, *, rows, tm):
    tm_up = next(t for t in (3 * tm // 2, tm) if rows % t == 0)
    hid = _ffn_up(u, w_gu, layer, rows=rows, tm=tm_up, tn=256)
    return _mm(hid, w_d, layer, rows=rows, tm=tm, tn=512, tk=w_d.shape[1] // 2, name="ffn_down")


def _ffn_after_ln(ln, w_gu, w_d, layer, rg, *, rows):
    hid, xs_new = _ffn_up_ln(ln, w_gu, layer, rg, rows=rows, tm=rg.tm, tn=256)
    y = _mm(hid, w_d, layer, rows=rows, tm=rg.tm, tn=512, tk=w_d.shape[1] // 2, name="ffn_down")
    return y, xs_new


def _mixing(u, p, layer, consts, rg, *, lam_init, need_ctx):
    cos_tab, sin_tab, (kinds_in, scales_in, src_in), dft = consts
    rows = rg.n if need_ctx else rg.nl
    h = _proj(u, p["w_in"], layer, kinds_in, scales_in, src_in, cos_tab, sin_tab, rg, rows=rg.n,
              rope_kinds=(ROPE_A, ROPE_B, ROPE_KR), name="in_proj")
    n_tiles = N_HEADS * 2 * HEAD_DIM // ROPE_TN
    ident = jnp.arange(n_tiles // PROJ_HALVES, dtype=jnp.int32)
    q_scale = jnp.full((n_tiles,), (C_NOPE + C_ROPE) ** -0.5 * LOG2E, F32)
    qc = _proj(h, p["c_w_uq"], layer, jnp.full((n_tiles,), ROPE_QR, jnp.int32), q_scale, ident, cos_tab, sin_tab,
               rg, rows=rows, rope_kinds=(ROPE_QR,), a_col0=H_CQ, gain=p["c_q_norm_g"], name="mla_q_up")
    kv2 = _proj(h, p["c_w_ukv"], layer, jnp.zeros((n_tiles,), jnp.int32), jnp.ones((n_tiles,), F32), ident,
                cos_tab, sin_tab, rg, rows=rg.n, rope_kinds=(), a_col0=H_CKV, gain=p["c_kv_norm_g"],
                name="mla_kv_up")
    zc, zs = _fourier_chan(h, dft["cos_c"], dft["sin_c"], p["d_w"], layer, rg, rows=rows)

    ya = _attn_win(h, p["a_sink"], layer, rg, latent=True, out_rows=rows)
    yb = _attn_diff(h, p["b_lambda"], p["b_subln_g"], layer, rg, latent=True, lam_init=lam_init, out_rows=rows)
    yc = _attn_mla(qc, kv2, h, rg, latent=True, out_rows=rows)
    yd = _fourier_seq(dft["cos_t"], dft["sin_t"], zc, zs, rg, latent=True, out_rows=rows)
    if need_ctx:
        ya = _attn_win(h, p["a_sink"], layer, rg, latent=False, out_rows=rows, dst=ya)
        yb = _attn_diff(h, p["b_lambda"], p["b_subln_g"], layer, rg, latent=False, lam_init=lam_init,
                        out_rows=rows, dst=yb)
        yc = _attn_mla(qc, kv2, h, rg, latent=False, out_rows=rows, dst=yc)
        yd = _fourier_seq(dft["cos_x"], dft["sin_x"], zc, zs, rg, latent=False, out_rows=rows, dst=yd)
    return _out_proj([ya, yb, yc, yd], p["w_out"], layer, rows=rows, tm=rg.tm, tn=512)


def kernel(x, c, ctx, c_ctx, w_mod, b_mod, ffn1_w_gu, ffn1_w_d, ffn2_w_gu, ffn2_w_d, ln_g, ln_b, w_in, w_out,
           a_sink, b_lambda, b_subln_g, c_q_norm_g, c_kv_norm_g, c_w_uq, c_w_ukv, d_w):
    n_batch, seq, d = x.shape
    ctx_len = ctx.shape[1]
    depth = w_mod.shape[0]
    rg = _Rows(n_batch, seq, ctx_len)
    alpha = (2.0 * depth) ** 0.25

    cos_tab, sin_tab = _rope_tables(seq)
    cos_t, sin_t = _dft_tables(seq)
    cos_x, sin_x = _dft_tables(ctx_len)
    cos_c, sin_c = _dft_tables(HEAD_DIM)
    dft = {"cos_t": cos_t.astype(BF16), "sin_t": sin_t.astype(BF16),
           "cos_x": cos_x.astype(BF16), "sin_x": sin_x.astype(BF16),
           "cos_c": cos_c.astype(BF16), "sin_c": sin_c.astype(BF16)}
    consts = (cos_tab, sin_tab, _in_proj_tiles(), dft)

    n_c_rows = 8
    c_rows = jnp.concatenate([c, c_ctx[None, :], jnp.zeros((n_c_rows - n_batch - 1, d), c.dtype)], axis=0)

    p = {"w_in": _prep_w_in(w_in), "w_out": w_out, "a_sink": a_sink, "b_lambda": b_lambda,
         "b_subln_g": b_subln_g, "c_q_norm_g": c_q_norm_g, "c_kv_norm_g": c_kv_norm_g,
         "c_w_uq": _prep_w_uq(c_w_uq), "c_w_ukv": c_w_ukv, "d_w": d_w}
    mod_all = _mod_vectors(c_rows, w_mod, b_mod).reshape(depth, n_c_rows, N_MOD, d)
    mods = [mod_all[l] for l in range(depth)]
    xs, u = _modulate(x.reshape(n_batch * seq, d), ctx.reshape(n_batch * ctx_len, d), mods[0], rg, shift_idx=0)
    pending = None
    for l in range(depth):
        last = l == depth - 1
        lam_init = 0.8 - 0.6 * math.exp(-0.3 * l)
        mod = mods[l]
        if pending is None:
            y = _ffn(u, ffn1_w_gu, ffn1_w_d, l, rows=rg.n, tm=rg.tm)
        else:
            y, xs = _ffn_after_ln(pending, ffn1_w_gu, ffn1_w_d, l, rg, rows=rg.n)
        xs, u = _ln_res(xs, y, mod, ln_g[l, 0], ln_b[l, 0], rg, rows=rg.n, alpha=alpha,
                        gate_idx=2, gate_mul=0.5, next_mod=mod, next_shift_idx=3, name="ln_ffn1")
        y = _mixing(u, p, l, consts, rg, lam_init=lam_init, need_ctx=not last)
        rows = rg.nl if last else rg.n
        ln_mix = {"x": xs, "y": y, "mod": mod, "next_mod": mod, "g": ln_g[l, 1], "b": ln_b[l, 1], "alpha": alpha,
                  "gate_idx": 5, "gate_mul": 1.0, "shift_idx": 6}
        y, xs = _ffn_after_ln(ln_mix, ffn2_w_gu, ffn2_w_d, l, rg, rows=rows)
        if last:
            xs, _ = _ln_res(xs, y, mod, ln_g[l, 2], ln_b[l, 2], rg, rows=rows, alpha=alpha,
                            gate_idx=8, gate_mul=0.5, next_mod=None, next_shift_idx=None, name="ln_ffn2")
        else:
            pending = {"x": xs, "y": y, "mod": mod, "next_mod": mods[l + 1], "g": ln_g[l, 2], "b": ln_b[l, 2],
                       "alpha": alpha, "gate_idx": 8, "gate_mul": 0.5, "shift_idx": 0}
    return xs[:rg.nl].reshape(n_batch, seq, d)
```

```python
import functools
import math

import numpy as np
import jax
import jax.numpy as jnp
from jax import lax
from jax.experimental import pallas as pl
from jax.experimental.pallas import tpu as pltpu

F32 = jnp.float32
BF16 = jnp.bfloat16

GRID_W = 64
HEAD_DIM = 128
WINDOW = 128
N_HEADS = 8
A_KV_HEADS = 2
A_GROUP = N_HEADS // A_KV_HEADS
B_QK_DIM = 64
C_Q_LORA = 1536
C_KV_LORA = 512
C_NOPE = 128
C_ROPE = 64
N_MOD = 9
ROPE_BASE = 10000.0
LN_EPS = 1e-5
RMS_EPS = 1e-6
NEG_INF = -1e30

H_CQ, H_AQ, H_BQ, H_DX = 0, 1536, 2560, 3584
H_AK, H_AV, H_BK, H_BV, H_CKV, H_CKR = 4608, 4864, 5120, 6144, 7168, 7680
H_COLS = 8192

ROPE_NONE, ROPE_A, ROPE_B, ROPE_KR, ROPE_QR = 0, 1, 2, 3, 4
ROPE_HALF_LANES = {ROPE_A: HEAD_DIM // 4, ROPE_B: B_QK_DIM // 4, ROPE_KR: C_ROPE // 4, ROPE_QR: C_ROPE // 4}

VMEM_LIMIT_BYTES = 56 * 1024 * 1024
ROPE_TN = 256
PROJ_HALVES = 2
Q_SUB = 256
LOG2E = math.log2(math.e)


def _cparams(n_axes):
    return pltpu.CompilerParams(dimension_semantics=("arbitrary",) * n_axes,
                                vmem_limit_bytes=VMEM_LIMIT_BYTES)


def _dot(a, b):
    return jnp.dot(a, b, preferred_element_type=F32)


def _dot_nt(a, b):
    return lax.dot_general(a, b, (((1,), (1,)), ((), ())), preferred_element_type=F32)


def _mm_kernel(a_ref, b_ref, o_ref, acc_ref, *, nk):
    k, j = pl.program_id(1), pl.program_id(2)

    def part():
        return _dot(a_ref[...], b_ref[...].astype(BF16))

    @pl.when(k == 0)
    def _():
        acc_ref[j] = part()

    if nk > 2:
        @pl.when((k > 0) & (k < nk - 1))
        def _():
            acc_ref[j] += part()

    @pl.when(k == nk - 1)
    def _():
        o_ref[...] = (acc_ref[j] + part()).astype(o_ref.dtype)


def _mm(a, b, layer, *, rows, tm, tn, tk, name):
    _, kdim, n = b.shape
    assert rows % tm == 0 and n % tn == 0 and kdim % tk == 0 and kdim // tk >= 2
    nk = kdim // tk
    return pl.pallas_call(
        functools.partial(_mm_kernel, nk=nk),
        out_shape=jax.ShapeDtypeStruct((rows, n), BF16),
        grid=(rows // tm, nk, n // tn),
        in_specs=[pl.BlockSpec((tm, tk), lambda i, k, j: (i, k)),
                  pl.BlockSpec((None, tk, tn), lambda i, k, j: (layer, k, j))],
        out_specs=pl.BlockSpec((tm, tn), lambda i, k, j: (i, jnp.where(k == nk - 1, j, 0))),
        scratch_shapes=[pltpu.VMEM((n // tn, tm, tn), F32)],
        compiler_params=_cparams(3),
        name=name,
    )(a, b)


def _out_proj_kernel(*refs):
    *y_refs, w_ref, o_ref = refs
    width = y_refs[0].shape[1]
    acc = None
    for m, y_ref in enumerate(y_refs):
        part = _dot(y_ref[...], w_ref[m * width:(m + 1) * width, :].astype(BF16))
        acc = part if acc is None else acc + part
    o_ref[...] = acc.astype(o_ref.dtype)


def _out_proj(ys, w_out, layer, *, rows, tm, tn):
    _, kdim, n = w_out.shape
    width = ys[0].shape[1]
    assert kdim == width * len(ys)
    y_spec = pl.BlockSpec((tm, width), lambda i, j: (i, 0))
    return pl.pallas_call(
        _out_proj_kernel,
        out_shape=jax.ShapeDtypeStruct((rows, n), BF16),
        grid=(rows // tm, n // tn),
        in_specs=[y_spec] * len(ys) + [pl.BlockSpec((None, kdim, tn), lambda i, j: (layer, 0, j))],
        out_specs=pl.BlockSpec((tm, tn), lambda i, j: (i, j)),
        compiler_params=_cparams(2),
        name="out_proj",
    )(*ys, w_out)


def _mod_kernel(c_ref, w_ref, b_ref, o_ref):
    c = c_ref[...]
    sc = (c * jax.nn.sigmoid(c)).astype(BF16)
    o_ref[...] = _dot(sc, w_ref[...].astype(BF16)) + b_ref[...]


def _mod_vectors(c_rows, w_mod, b_mod):
    depth, d, n = w_mod.shape
    r = c_rows.shape[0]
    tn = 1024
    return pl.pallas_call(
        _mod_kernel,
        out_shape=jax.ShapeDtypeStruct((depth, r, n), F32),
        grid=(depth, n // tn),
        in_specs=[pl.BlockSpec((r, d), lambda l, j: (0, 0)),
                  pl.BlockSpec((None, d, tn), lambda l, j: (l, 0, j)),
                  pl.BlockSpec((None, 1, tn), lambda l, j: (l, 0, j))],
        out_specs=pl.BlockSpec((None, r, tn), lambda l, j: (l, 0, j)),
        compiler_params=_cparams(2),
        name="mod_vectors",
    )(c_rows, w_mod, b_mod.reshape(depth, 1, n))


def _seg_index(i, n_lat_tiles, tiles_per_batch, n_batch):
    return jnp.where(i < n_lat_tiles, i // tiles_per_batch, n_batch)


def _modulate_kernel(x_ref, c_ref, mod_ref, xs_ref, u_ref, *, shift_idx, n_lat_tiles):
    shift = mod_ref[shift_idx:shift_idx + 1, :]
    scale = mod_ref[shift_idx + 1:shift_idx + 2, :]

    def emit(src_ref):
        x = src_ref[...]
        xs_ref[...] = x
        u_ref[...] = (x * (1.0 + scale) + shift).astype(u_ref.dtype)

    @pl.when(pl.program_id(0) < n_lat_tiles)
    def _():
        emit(x_ref)

    @pl.when(pl.program_id(0) >= n_lat_tiles)
    def _():
        emit(c_ref)


def _res_ln(x, y, mod_ref, g_ref, b_ref, *, alpha, gate_idx, gate_mul):
    gate = mod_ref[gate_idx:gate_idx + 1, :]
    z = alpha * x + (gate_mul * gate) * y.astype(F32)
    mu = jnp.mean(z, axis=-1, keepdims=True)
    zc = z - mu
    var = jnp.mean(zc * zc, axis=-1, keepdims=True)
    return zc * lax.rsqrt(var + LN_EPS) * g_ref[...] + b_ref[...]


def _modulated(xn, nmod_ref, shift_idx):
    shift = nmod_ref[shift_idx:shift_idx + 1, :]
    scale = nmod_ref[shift_idx + 1:shift_idx + 2, :]
    return (xn * (1.0 + scale) + shift).astype(BF16)


def _ln_res_kernel(x_ref, y_ref, mod_ref, g_ref, b_ref, *rest, alpha, gate_idx, gate_mul, next_shift_idx):
    xn = _res_ln(x_ref[...], y_ref[...], mod_ref, g_ref, b_ref, alpha=alpha, gate_idx=gate_idx, gate_mul=gate_mul)
    if next_shift_idx is None:
        xo_ref, = rest
        xo_ref[...] = xn
    else:
        nmod_ref, xo_ref, u_ref = rest
        xo_ref[...] = xn
        u_ref[...] = _modulated(xn, nmod_ref, next_shift_idx)


class _Rows:
    def __init__(self, n_batch, seq, ctx_len):
        self.b, self.t, self.c = n_batch, seq, ctx_len
        self.nl, self.nc = n_batch * seq, n_batch * ctx_len
        self.n = self.nl + self.nc
        self.tr = math.gcd(256, math.gcd(seq, ctx_len))
        self.tm = math.gcd(1024, math.gcd(seq, self.nc))

    def mod_spec(self, tile, d):
        n_lat_tiles, per_batch, nb = self.nl // tile, self.t // tile, self.b
        return pl.BlockSpec((None, N_MOD, d),
                            lambda i, *_: (_seg_index(i, n_lat_tiles, per_batch, nb), 0, 0))


def _modulate(x, ctx, mod, rg, *, shift_idx):
    d = x.shape[1]
    tr = rg.tr
    n_lat_tiles, n_ctx_tiles = rg.nl // tr, rg.nc // tr
    row_spec = pl.BlockSpec((tr, d), lambda i: (i, 0))
    x_spec = pl.BlockSpec((tr, d), lambda i: (jnp.minimum(i, n_lat_tiles - 1), 0))
    c_spec = pl.BlockSpec((tr, d), lambda i: (jnp.clip(i - n_lat_tiles, 0, n_ctx_tiles - 1), 0))
    return pl.pallas_call(
        functools.partial(_modulate_kernel, shift_idx=shift_idx, n_lat_tiles=n_lat_tiles),
        out_shape=[jax.ShapeDtypeStruct((rg.n, d), F32), jax.ShapeDtypeStruct((rg.n, d), BF16)],
        grid=(rg.n // tr,),
        in_specs=[x_spec, c_spec, rg.mod_spec(tr, d)],
        out_specs=[row_spec, row_spec],
        compiler_params=_cparams(1),
        name="modulate",
    )(x, ctx, mod)


def _ln_res(x, y, mod, ln_g, ln_b, rg, *, rows, alpha, gate_idx, gate_mul, next_mod, next_shift_idx, name,
            out_rows=None):
    d = x.shape[1]
    tr = rg.tr
    row_spec = pl.BlockSpec((tr, d), lambda i: (i, 0))
    vec_spec = pl.BlockSpec((1, d), lambda i: (0, 0))
    out_shape = [jax.ShapeDtypeStruct((out_rows or rows, d), F32)]
    out_specs = [row_spec]
    in_specs = [row_spec, row_spec, rg.mod_spec(tr, d), vec_spec, vec_spec]
    args = [x, y, mod, ln_g.reshape(1, d), ln_b.reshape(1, d)]
    if next_shift_idx is not None:
        in_specs.append(rg.mod_spec(tr, d))
        args.append(next_mod)
        out_shape.append(jax.ShapeDtypeStruct((rows, d), BF16))
        out_specs.append(row_spec)
    res = pl.pallas_call(
        functools.partial(_ln_res_kernel, alpha=alpha, gate_idx=gate_idx, gate_mul=gate_mul,
                          next_shift_idx=next_shift_idx),
        out_shape=out_shape,
        grid=(rows // tr,),
        in_specs=in_specs,
        out_specs=out_specs,
        compiler_params=_cparams(1),
        name=name,
    )(*args)
    return (res[0], res[1]) if next_shift_idx is not None else (res[0], None)


def _ffn_up_kernel(a_ref, wg_ref, wu_ref, o_ref):
    a = a_ref[...]
    g = _dot(a, wg_ref[...].astype(BF16))
    up = _dot(a, wu_ref[...].astype(BF16))
    o_ref[...] = (g * jax.nn.sigmoid(g) * up).astype(o_ref.dtype)


def _ffn_up(u, w_gu, layer, *, rows, tm, tn):
    _, d, two_ff = w_gu.shape
    ff = two_ff // 2
    up0 = ff // tn
    return pl.pallas_call(
        _ffn_up_kernel,
        out_shape=jax.ShapeDtypeStruct((rows, ff), BF16),
        grid=(rows // tm, ff // tn),
        in_specs=[pl.BlockSpec((tm, d), lambda i, j: (i, 0)),
                  pl.BlockSpec((None, d, tn), lambda i, j: (layer, 0, j)),
                  pl.BlockSpec((None, d, tn), lambda i, j: (layer, 0, up0 + j))],
        out_specs=pl.BlockSpec((tm, tn), lambda i, j: (i, j)),
        compiler_params=_cparams(2),
        name="ffn_up",
    )(u, w_gu, w_gu)


def _ffn_up_ln_kernel(x_ref, y_ref, mod_ref, nmod_ref, g_ref, b_ref, u0_ref, _xs_head, wg_ref, wu_ref,
                      hid_ref, xo_ref, a_even, a_odd, *, alpha, gate_idx, gate_mul, shift_idx, chunk):
    i, j = pl.program_id(0), pl.program_id(1)

    @pl.when((i == 0) & (j == 0))
    def _():
        a_even[...] = u0_ref[...]

    def step(a_cur, a_next):
        xn = _res_ln(x_ref[...], y_ref[...], mod_ref, g_ref, b_ref,
                     alpha=alpha, gate_idx=gate_idx, gate_mul=gate_mul)
        xo_ref[...] = xn
        a_next[pl.ds(pl.multiple_of(j * chunk, chunk), chunk), :] = _modulated(xn, nmod_ref, shift_idx)
        a = a_cur[...]
        g = _dot(a, wg_ref[...].astype(BF16))
        up = _dot(a, wu_ref[...].astype(BF16))
        hid_ref[...] = (g * jax.nn.sigmoid(g) * up).astype(hid_ref.dtype)

    @pl.when(i % 2 == 0)
    def _():
        step(a_even, a_odd)

    @pl.when(i % 2 == 1)
    def _():
        step(a_odd, a_even)


def _ffn_up_ln(ln, w_gu, layer, rg, *, rows, tm, tn):
    _, d, two_ff = w_gu.shape
    ff = two_ff // 2
    n_i, n_j = rows // tm, ff // tn
    chunk = tm // n_j
    assert rows % tm == 0 and tm % n_j == 0 and chunk % 16 == 0 and rg.t % chunk == 0 and rg.nl % tm == 0
    xs_head, u0 = _ln_res(ln["x"], ln["y"], ln["mod"], ln["g"], ln["b"], rg, rows=tm, out_rows=rows,
                          alpha=ln["alpha"], gate_idx=ln["gate_idx"], gate_mul=ln["gate_mul"],
                          next_mod=ln["next_mod"], next_shift_idx=ln["shift_idx"], name="ln_head")

    def chunk_block(i, j):
        return jnp.where(i < n_i - 1, (i + 1) * n_j + j, n_i * n_j - 1)

    def seg(i, j):
        row0 = chunk_block(i, j) * chunk
        return jnp.where(row0 < rg.nl, row0 // rg.t, rg.b)

    chunk_spec = pl.BlockSpec((chunk, d), lambda i, j: (chunk_block(i, j), 0))
    mod_spec = pl.BlockSpec((None, N_MOD, d), lambda i, j: (seg(i, j), 0, 0))
    vec_spec = pl.BlockSpec((1, d), lambda i, j: (0, 0))
    up0 = ff // tn
    hid, xs_new = pl.pallas_call(
        functools.partial(_ffn_up_ln_kernel, alpha=ln["alpha"], gate_idx=ln["gate_idx"], gate_mul=ln["gate_mul"],
                          shift_idx=ln["shift_idx"], chunk=chunk),
        out_shape=[jax.ShapeDtypeStruct((rows, ff), BF16), jax.ShapeDtypeStruct((rows, d), F32)],
        grid=(n_i, n_j),
        in_specs=[chunk_spec, chunk_spec, mod_spec, mod_spec, vec_spec, vec_spec,
                  pl.BlockSpec((tm, d), lambda i, j: (0, 0), pipeline_mode=pl.Buffered(1)),
                  pl.BlockSpec(memory_space=pl.ANY),
                  pl.BlockSpec((None, d, tn), lambda i, j: (layer, 0, j)),
                  pl.BlockSpec((None, d, tn), lambda i, j: (layer, 0, up0 + j))],
        out_specs=[pl.BlockSpec((tm, tn), lambda i, j: (i, j)), chunk_spec],
        scratch_shapes=[pltpu.VMEM((tm, d), BF16), pltpu.VMEM((tm, d), BF16)],
        input_output_aliases={7: 1},
        compiler_params=_cparams(2),
        name="ffn_up_ln",
    )(ln["x"], ln["y"], ln["mod"], ln["next_mod"], ln["g"].reshape(1, d), ln["b"].reshape(1, d), u0, xs_head,
      w_gu, w_gu)
    return hid, xs_new


def _rope_tile(x, cos, sin, half):
    lane = lax.broadcasted_iota(jnp.int32, x.shape, 1)
    first = (lane % (2 * half)) < half
    width = x.shape[1]
    partner = jnp.where(first, pltpu.roll(x, width - half, 1), pltpu.roll(x, half, 1))
    return x * cos + partner * sin


def _proj_kernel(kind_ref, src_ref, scale_ref, a_ref, b_ref, *rest, has_gain, n_lat_tiles, rope_kinds):
    if has_gain:
        gain_ref, *tab_refs, o_ref, an_ref = rest
    else:
        *tab_refs, o_ref = rest
    tables = {kind: (tab_refs[2 * n], tab_refs[2 * n + 1]) for n, kind in enumerate(rope_kinds)}
    i, j = pl.program_id(0), pl.program_id(1)

    if has_gain:
        @pl.when(j == 0)
        def _():
            x = a_ref[...].astype(F32)
            ms = jnp.mean(x * x, axis=-1, keepdims=True)
            an_ref[...] = (x * lax.rsqrt(ms + RMS_EPS) * gain_ref[...]).astype(BF16)
    lhs_ref = an_ref if has_gain else a_ref

    for half in range(PROJ_HALVES):
        cols = slice(half * ROPE_TN, (half + 1) * ROPE_TN)
        tile = PROJ_HALVES * j + half
        kind = jnp.where(i < n_lat_tiles, kind_ref[tile], ROPE_NONE)
        out_scale = scale_ref[tile]

        def acc():
            return _dot(lhs_ref[...], b_ref[:, cols].astype(BF16)) * out_scale

        @pl.when(kind == ROPE_NONE)
        def _():
            o_ref[:, cols] = acc().astype(o_ref.dtype)

        for rope_kind, (cos_ref, sin_ref) in tables.items():
            @pl.when(kind == rope_kind)
            def _(cos_ref=cos_ref, sin_ref=sin_ref, rope_kind=rope_kind):
                roped = _rope_tile(acc(), cos_ref[...], sin_ref[...], ROPE_HALF_LANES[rope_kind])
                o_ref[:, cols] = roped.astype(o_ref.dtype)


def _proj(a, w, layer, kinds, scales, src_tiles, cos_tab, sin_tab, rg, *, rows, rope_kinds, a_col0=0, gain=None,
          name):
    _, kdim, n = w.shape
    tm, tn = rg.tm, PROJ_HALVES * ROPE_TN
    assert a_col0 % kdim == 0 and n % tn == 0
    a_blk = a_col0 // kdim
    n_lat_tiles = rg.nl // tm
    t_tiles = rg.t // tm
    has_gain = gain is not None

    def tab_spec(kind):
        return pl.BlockSpec((None, tm, ROPE_TN), lambda i, j, kr, sr: (kind, i % t_tiles, 0))

    in_specs = [pl.BlockSpec(memory_space=pltpu.SMEM),
                pl.BlockSpec((tm, kdim), lambda i, j, kr, sr: (i, a_blk)),
                pl.BlockSpec((None, kdim, tn), lambda i, j, kr, sr: (layer, 0, sr[j]))]
    args = [scales, a, w]
    scratch = []
    if has_gain:
        in_specs.append(pl.BlockSpec((None, 1, kdim), lambda i, j, kr, sr: (layer, 0, 0)))
        args.append(gain.reshape(gain.shape[0], 1, kdim))
        scratch.append(pltpu.VMEM((tm, kdim), BF16))
    for kind in rope_kinds:
        in_specs += [tab_spec(kind), tab_spec(kind)]
        args += [cos_tab, sin_tab]
    return pl.pallas_call(
        functools.partial(_proj_kernel, has_gain=has_gain, n_lat_tiles=n_lat_tiles, rope_kinds=tuple(rope_kinds)),
        out_shape=jax.ShapeDtypeStruct((rows, n), BF16),
        grid_spec=pltpu.PrefetchScalarGridSpec(
            num_scalar_prefetch=2,
            grid=(rows // tm, n // tn),
            in_specs=in_specs,
            out_specs=pl.BlockSpec((tm, tn), lambda i, j, kr, sr: (i, j)),
            scratch_shapes=scratch),
        compiler_params=_cparams(2),
        name=name,
    )(kinds, src_tiles, *args)


def _rope_tables(seq):
    t = np.arange(seq)
    row, col = (t // GRID_W).astype(np.float64), (t % GRID_W).astype(np.float64)

    def pattern(rot_dim):
        axis_dim = rot_dim // 2
        inv = ROPE_BASE ** (-np.arange(0, axis_dim, 2, dtype=np.float64) / axis_dim)
        ar, ac = row[:, None] * inv[None, :], col[:, None] * inv[None, :]
        cos = np.concatenate([np.cos(ar), np.cos(ar), np.cos(ac), np.cos(ac)], axis=1)
        sin = np.concatenate([-np.sin(ar), np.sin(ar), -np.sin(ac), np.sin(ac)], axis=1)
        return cos, sin

    cos = np.ones((5, seq, ROPE_TN), np.float64)
    sin = np.zeros((5, seq, ROPE_TN), np.float64)
    c128, s128 = pattern(HEAD_DIM)
    c64, s64 = pattern(2 * 32)
    cos[ROPE_A], sin[ROPE_A] = np.tile(c128, (1, 2)), np.tile(s128, (1, 2))
    cos[ROPE_B], sin[ROPE_B] = np.tile(c64, (1, 4)), np.tile(s64, (1, 4))
    cos[ROPE_KR, :, :64], sin[ROPE_KR, :, :64] = c64, s64
    cos[ROPE_QR, :, 128:192], sin[ROPE_QR, :, 128:192] = c64, s64
    return jnp.asarray(cos, F32), jnp.asarray(sin, F32)


def _mixer_call(body, rg, *, latent, dst, out_rows, grid, in_specs, args, tq, n_q, out_block_w, name):
    row0 = 0 if latent else rg.nl // tq
    aliases = {}
    if not latent:
        in_specs = in_specs + [pl.BlockSpec(memory_space=pl.ANY)]
        args = args + [dst]
        aliases = {len(args) - 1: 0}
    return pl.pallas_call(
        body,
        out_shape=jax.ShapeDtypeStruct((out_rows, N_HEADS * HEAD_DIM), BF16),
        grid=grid,
        in_specs=in_specs,
        out_specs=pl.BlockSpec((tq, out_block_w), lambda b, hh, qi: (row0 + b * n_q + qi, hh)),
        input_output_aliases=aliases,
        compiler_params=_cparams(3),
        name=name,
    )(*args)


def _attn_win_kernel(sink_ref, q_ref, kc_ref, vc_ref, *rest, has_lat, tq, seq, layer):
    if has_lat:
        kl_ref, vl_ref, o_ref = rest
    else:
        _, o_ref = rest
    g, qi = pl.program_id(1), pl.program_id(2)
    kc, vc = kc_ref[...], vc_ref[...]
    if has_lat:
        win = min(seq, tq + 2 * WINDOW)
        ws = pl.multiple_of(jnp.clip(qi * tq - WINDOW, 0, seq - win), WINDOW)
        kw, vw = kl_ref[pl.ds(ws, win), :], vl_ref[pl.ds(ws, win), :]
        qpos = qi * tq + lax.broadcasted_iota(jnp.int32, (tq, win), 0)
        kpos = ws + lax.broadcasted_iota(jnp.int32, (tq, win), 1)
        valid = jnp.abs(qpos - kpos) <= WINDOW
    for j in range(A_GROUP):
        q = q_ref[:, j * HEAD_DIM:(j + 1) * HEAD_DIM]
        sink = sink_ref[layer, g * A_GROUP + j] * LOG2E
        s_c = _dot_nt(q, kc)
        m = jnp.maximum(jnp.max(s_c, axis=-1, keepdims=True), sink)
        if has_lat:
            s_l = jnp.where(valid, _dot_nt(q, kw), NEG_INF)
            m = jnp.maximum(m, jnp.max(s_l, axis=-1, keepdims=True))
        p_c = jnp.exp2(s_c - m)
        den = jnp.sum(p_c, axis=-1, keepdims=True) + jnp.exp2(sink - m)
        o = _dot(p_c.astype(BF16), vc)
        if has_lat:
            p_l = jnp.exp2(s_l - m)
            den = den + jnp.sum(p_l, axis=-1, keepdims=True)
            o = o + _dot(p_l.astype(BF16), vw)
        o_ref[:, j * HEAD_DIM:(j + 1) * HEAD_DIM] = (o / den).astype(o_ref.dtype)


def _attn_win(h, a_sink, layer, rg, *, latent, out_rows, dst=None):
    gw = A_GROUP * HEAD_DIM
    tq, n_q, q0 = _q_geometry(rg, latent, 256)
    c_blk0 = rg.nl // rg.c
    in_specs = [pl.BlockSpec(memory_space=pltpu.SMEM),
                pl.BlockSpec((tq, gw), lambda b, g, qi: (q0 + b * n_q + qi, H_AQ // gw + g)),
                pl.BlockSpec((rg.c, HEAD_DIM), lambda b, g, qi: (c_blk0 + b, H_AK // HEAD_DIM + g)),
                pl.BlockSpec((rg.c, HEAD_DIM), lambda b, g, qi: (c_blk0 + b, H_AV // HEAD_DIM + g))]
    args = [a_sink, h, h, h]
    if latent:
        in_specs += [pl.BlockSpec((rg.t, HEAD_DIM), lambda b, g, qi: (b, H_AK // HEAD_DIM + g)),
                     pl.BlockSpec((rg.t, HEAD_DIM), lambda b, g, qi: (b, H_AV // HEAD_DIM + g))]
        args += [h, h]
    return _mixer_call(functools.partial(_attn_win_kernel, has_lat=latent, tq=tq, seq=rg.t, layer=layer), rg,
                       latent=latent, dst=dst, out_rows=out_rows, grid=(rg.b, A_KV_HEADS, n_q), in_specs=in_specs, args=args,
                       tq=tq, n_q=n_q, out_block_w=gw, name="attn_win_lat" if latent else "attn_win_ctx")


def _softmax_pv(q, keys, values):
    scores = [_dot_nt(q, k) for k in keys]
    m = functools.reduce(jnp.maximum, [jnp.max(s, axis=-1, keepdims=True) for s in scores])
    den, out = None, None
    for s, v in zip(scores, values):
        p = jnp.exp2(s - m)
        d = jnp.sum(p, axis=-1, keepdims=True)
        o = _dot(p.astype(BF16), v)
        den = d if den is None else den + d
        out = o if out is None else out + o
    return out / den


def _attn_diff_kernel(lam_ref, gsub_ref, q_ref, kc_ref, vc_ref, *rest, has_lat, lam_init):
    if has_lat:
        kl_ref, vl_ref, o_ref = rest
    else:
        _, o_ref = rest
    lf = lam_ref[...]
    lam = (jnp.exp(jnp.sum(lf[0:1] * lf[1:2], axis=-1, keepdims=True))
           - jnp.exp(jnp.sum(lf[2:3] * lf[3:4], axis=-1, keepdims=True)) + lam_init)
    keys = [kc_ref[...]] + ([kl_ref[...]] if has_lat else [])
    values = [vc_ref[...]] + ([vl_ref[...]] if has_lat else [])
    tq = q_ref.shape[0]
    sub = min(tq, Q_SUB)
    for r in range(tq // sub):
        q = q_ref[r * sub:(r + 1) * sub, :]
        first_map = lax.broadcasted_iota(jnp.int32, q.shape, 1) < B_QK_DIM
        zero = jnp.zeros_like(q)

        o = (_softmax_pv(jnp.where(first_map, q, zero), keys, values)
             - lam * _softmax_pv(jnp.where(first_map, zero, q), keys, values))
        ms = jnp.mean(o * o, axis=-1, keepdims=True)
        o = o * lax.rsqrt(ms + RMS_EPS) * gsub_ref[...] * (1.0 - lam_init)
        o_ref[r * sub:(r + 1) * sub, :] = o.astype(o_ref.dtype)


def _kv_specs(rg, col_k, col_v, latent, k_width=HEAD_DIM):
    c_blk0 = rg.nl // rg.c
    specs = [pl.BlockSpec((rg.c, k_width), lambda b, hh, qi: (c_blk0 + b, col_k(hh))),
             pl.BlockSpec((rg.c, HEAD_DIM), lambda b, hh, qi: (c_blk0 + b, col_v(hh)))]
    if latent:
        specs += [pl.BlockSpec((rg.t, k_width), lambda b, hh, qi: (b, col_k(hh))),
                  pl.BlockSpec((rg.t, HEAD_DIM), lambda b, hh, qi: (b, col_v(hh)))]
    return specs


def _q_geometry(rg, latent, tq_lat):
    if latent:
        tq = min(tq_lat, rg.t)
        return tq, rg.t // tq, 0
    return rg.c, 1, rg.nl // rg.c


def _attn_diff(h, b_lambda, b_subln_g, layer, rg, *, latent, lam_init, out_rows, dst=None):
    tq, n_q, q0 = _q_geometry(rg, latent, 2048)
    in_specs = [pl.BlockSpec((None, 4, B_QK_DIM), lambda b, hh, qi: (layer, 0, 0)),
                pl.BlockSpec((None, 1, HEAD_DIM), lambda b, hh, qi: (layer, 0, 0)),
                pl.BlockSpec((tq, HEAD_DIM), lambda b, hh, qi: (q0 + b * n_q + qi, H_BQ // HEAD_DIM + hh))]
    in_specs += _kv_specs(rg, lambda hh: H_BK // HEAD_DIM + hh, lambda hh: H_BV // HEAD_DIM + hh, latent)
    n_kv = 4 if latent else 2
    args = [b_lambda, b_subln_g.reshape(b_subln_g.shape[0], 1, HEAD_DIM), h] + [h] * n_kv
    return _mixer_call(functools.partial(_attn_diff_kernel, has_lat=latent, lam_init=lam_init), rg,
                       latent=latent, dst=dst, out_rows=out_rows, grid=(rg.b, N_HEADS, n_q), in_specs=in_specs, args=args,
                       tq=tq, n_q=n_q, out_block_w=HEAD_DIM, name="attn_diff_lat" if latent else "attn_diff_ctx")


def _attn_mla_kernel(q_ref, knc_ref, vc_ref, krc_ref, *rest, has_lat):
    if has_lat:
        knl_ref, vl_ref, krl_ref, o_ref = rest
    else:
        _, o_ref = rest
    segs = [(knc_ref, krc_ref, vc_ref)] + ([(knl_ref, krl_ref, vl_ref)] if has_lat else [])
    keys = [jnp.concatenate([kn[...], kr[...]], axis=1) for kn, kr, _ in segs]
    values = [v[...] for _, _, v in segs]
    tq = q_ref.shape[0]
    sub = min(tq, Q_SUB)
    for r in range(tq // sub):
        rows = slice(r * sub, (r + 1) * sub)
        o_ref[rows, :] = _softmax_pv(q_ref[rows, :], keys, values).astype(o_ref.dtype)


def _attn_mla(qc, kv2, h, rg, *, latent, out_rows, dst=None):
    tq, n_q, q0 = _q_geometry(rg, latent, 2048)
    c_blk0 = rg.nl // rg.c
    kr_col = H_CKR // HEAD_DIM
    kv = _kv_specs(rg, lambda hh: 2 * hh, lambda hh: 2 * hh + 1, latent)
    in_specs = [pl.BlockSpec((tq, 2 * HEAD_DIM), lambda b, hh, qi: (q0 + b * n_q + qi, hh)),
                kv[0], kv[1],
                pl.BlockSpec((rg.c, HEAD_DIM), lambda b, hh, qi: (c_blk0 + b, kr_col))]
    args = [qc, kv2, kv2, h]
    if latent:
        in_specs += [kv[2], kv[3], pl.BlockSpec((rg.t, HEAD_DIM), lambda b, hh, qi: (b, kr_col))]
        args += [kv2, kv2, h]
    return _mixer_call(functools.partial(_attn_mla_kernel, has_lat=latent), rg,
                       latent=latent, dst=dst, out_rows=out_rows, grid=(rg.b, N_HEADS, n_q), in_specs=in_specs, args=args,
                       tq=tq, n_q=n_q, out_block_w=HEAD_DIM, name="attn_mla_lat" if latent else "attn_mla_ctx")


def _dft_tables(n):
    idx = np.arange(n, dtype=np.int64)
    ang = (np.outer(idx, idx) % n).astype(np.float64) * (2.0 * np.pi / n)
    return jnp.asarray(np.cos(ang), F32), jnp.asarray(np.sin(ang), F32)


def _fourier_chan_kernel(z_ref, cc_ref, sc_ref, dw_ref, zc_ref, zs_ref):
    for g in range(dw_ref.shape[0]):
        cols = slice(g * HEAD_DIM, (g + 1) * HEAD_DIM)
        dw = dw_ref[g].astype(BF16)
        z = z_ref[:, cols]
        zc_ref[:, cols] = _dot(z, _dot(cc_ref[...], dw).astype(BF16)).astype(zc_ref.dtype)
        zs_ref[:, cols] = _dot(z, _dot(sc_ref[...], dw).astype(BF16)).astype(zs_ref.dtype)


def _fourier_chan(h, cos_c, sin_c, d_w, layer, rg, *, rows):
    tm = rg.tm
    groups = N_HEADS // 2
    width = groups * HEAD_DIM
    assert H_DX % width == 0
    tile = pl.BlockSpec((tm, width), lambda i, s: (i, s))
    const = pl.BlockSpec((HEAD_DIM, HEAD_DIM), lambda i, s: (0, 0))
    out = jax.ShapeDtypeStruct((rows, N_HEADS * HEAD_DIM), BF16)
    return pl.pallas_call(
        _fourier_chan_kernel,
        out_shape=[out, out],
        grid=(rows // tm, N_HEADS // groups),
        in_specs=[pl.BlockSpec((tm, width), lambda i, s: (i, H_DX // width + s)), const, const,
                  pl.BlockSpec((None, groups, HEAD_DIM, HEAD_DIM), lambda i, s: (layer, s, 0, 0))],
        out_specs=[tile, tile],
        compiler_params=_cparams(2),
        name="fourier_chan",
    )(h, cos_c, sin_c, d_w)


def _fourier_seq_kernel(ct_ref, st_ref, zc_ref, zs_ref, *rest, norm):
    o_ref = rest[-1]
    o = _dot(ct_ref[...], zc_ref[...]) - _dot(st_ref[...], zs_ref[...])
    o_ref[...] = (o * norm).astype(o_ref.dtype)


def _fourier_seq(cos_t, sin_t, zc, zs, rg, *, latent, out_rows, dst=None):
    length = rg.t if latent else rg.c
    tm = min(512, length)
    n_i = length // tm
    z_blk0 = 0 if latent else rg.nl // rg.c
    row0 = 0 if latent else rg.nl // tm
    width = zc.shape[1]
    dft_spec = pl.BlockSpec((tm, length), lambda b, i: (i, 0))
    z_spec = pl.BlockSpec((length, width), lambda b, i: (z_blk0 + b, 0))
    in_specs, args, aliases = [dft_spec, dft_spec, z_spec, z_spec], [cos_t, sin_t, zc, zs], {}
    if not latent:
        in_specs.append(pl.BlockSpec(memory_space=pl.ANY))
        args.append(dst)
        aliases = {len(args) - 1: 0}
    return pl.pallas_call(
        functools.partial(_fourier_seq_kernel, norm=(length * HEAD_DIM) ** -0.5),
        out_shape=jax.ShapeDtypeStruct((out_rows, width), BF16),
        grid=(rg.b, n_i),
        in_specs=in_specs,
        out_specs=pl.BlockSpec((tm, width), lambda b, i: (row0 + b * n_i + i, 0)),
        input_output_aliases=aliases,
        compiler_params=_cparams(2),
        name="fourier_seq_lat" if latent else "fourier_seq_ctx",
    )(*args)


def _prep_w_in(w_in):
    return jnp.pad(w_in, ((0, 0), (0, 0), (0, H_COLS - w_in.shape[2]))).astype(BF16)


def _prep_w_uq(w_uq):
    depth, k, _ = w_uq.shape
    w = w_uq.reshape(depth, k, N_HEADS, C_NOPE + C_ROPE)
    w = jnp.pad(w, ((0, 0), (0, 0), (0, 0), (0, 2 * HEAD_DIM - C_NOPE - C_ROPE)))
    return w.reshape(depth, k, N_HEADS * 2 * HEAD_DIM).astype(BF16)


def _in_proj_tiles():
    step = PROJ_HALVES * ROPE_TN
    src_cols = {"aq": 0, "bq": 1024, "cq": 2048, "dx": 3584, "ak": 4608, "bk": 5120, "bv": 6144,
                "ckv": 7168, "ckr": 7680}
    layout = [("cq", H_CQ, H_AQ, (ROPE_NONE, ROPE_NONE), 1.0),
              ("aq", H_AQ, H_BQ, (ROPE_A, ROPE_A), HEAD_DIM ** -0.5 * LOG2E),
              ("bq", H_BQ, H_DX, (ROPE_B, ROPE_B), B_QK_DIM ** -0.5 * LOG2E),
              ("dx", H_DX, H_AK, (ROPE_NONE, ROPE_NONE), 1.0),
              ("ak", H_AK, H_BK, (ROPE_A, ROPE_NONE), 1.0),
              ("bk", H_BK, H_BV, (ROPE_B, ROPE_B), 1.0),
              ("bv", H_BV, H_CKV, (ROPE_NONE, ROPE_NONE), 1.0),
              ("ckv", H_CKV, H_CKR, (ROPE_NONE, ROPE_NONE), 1.0),
              ("ckr", H_CKR, H_COLS, (ROPE_KR, ROPE_NONE), 1.0)]
    kinds = np.zeros((H_COLS // ROPE_TN,), np.int32)
    scales = np.ones((H_COLS // ROPE_TN,), np.float32)
    src = np.zeros((H_COLS // step,), np.int32)
    for name, lo, hi, kind, scale in layout:
        assert lo % step == 0 and hi % step == 0 and src_cols[name] % step == 0
        n = (hi - lo) // step
        kinds[lo // ROPE_TN:hi // ROPE_TN] = np.tile(np.asarray(kind, np.int32), n)
        scales[lo // ROPE_TN:hi // ROPE_TN] = scale
        src[lo // step:hi // step] = src_cols[name] // step + np.arange(n)
    return jnp.asarray(kinds), jnp.asarray(scales), jnp.asarray(src)


def _ffn(u, w_gu, w_d, layer, *, rows, tm):
    tm_up = next(t for t in (3 * tm // 2, tm) if rows % t == 0)
    hid = _ffn_up(u, w_gu, layer, rows=rows, tm=tm_up, tn=256)
    return _mm(hid, w_d, layer, rows=rows, tm=tm, tn=512, tk=w_d.shape[1] // 2, name="ffn_down")


def _ffn_after_ln(ln, w_gu, w_d, layer, rg, *, rows):
    hid, xs_new = _ffn_up_ln(ln, w_gu, layer, rg, rows=rows, tm=rg.tm, tn=256)
    y = _mm(hid, w_d, layer, rows=rows, tm=rg.tm, tn=512, tk=w_d.shape[1] // 2, name="ffn_down")
    return y, xs_new


def _mixing(u, p, layer, consts, rg, *, lam_init, need_ctx):
    cos_tab, sin_tab, (kinds_in, scales_in, src_in), dft = consts
    rows = rg.n if need_ctx else rg.nl
    h = _proj(u, p["w_in"], layer, kinds_in, scales_in, src_in, cos_tab, sin_tab, rg, rows=rg.n,
              rope_kinds=(ROPE_A, ROPE_B, ROPE_KR), name="in_proj")
    n_tiles = N_HEADS * 2 * HEAD_DIM // ROPE_TN
    ident = jnp.arange(n_tiles // PROJ_HALVES, dtype=jnp.int32)
    q_scale = jnp.full((n_tiles,), (C_NOPE + C_ROPE) ** -0.5 * LOG2E, F32)
    qc = _proj(h, p["c_w_uq"], layer, jnp.full((n_tiles,), ROPE_QR, jnp.int32), q_scale, ident, cos_tab, sin_tab,
               rg, rows=rows, rope_kinds=(ROPE_QR,), a_col0=H_CQ, gain=p["c_q_norm_g"], name="mla_q_up")
    kv2 = _proj(h, p["c_w_ukv"], layer, jnp.zeros((n_tiles,), jnp.int32), jnp.ones((n_tiles,), F32), ident,
                cos_tab, sin_tab, rg, rows=rg.n, rope_kinds=(), a_col0=H_CKV, gain=p["c_kv_norm_g"],
                name="mla_kv_up")
    zc, zs = _fourier_chan(h, dft["cos_c"], dft["sin_c"], p["d_w"], layer, rg, rows=rows)

    ya = _attn_win(h, p["a_sink"], layer, rg, latent=True, out_rows=rows)
    yb = _attn_diff(h, p["b_lambda"], p["b_subln_g"], layer, rg, latent=True, lam_init=lam_init, out_rows=rows)
    yc = _attn_mla(qc, kv2, h, rg, latent=True, out_rows=rows)
    yd = _fourier_seq(dft["cos_t"], dft["sin_t"], zc, zs, rg, latent=True, out_rows=rows)
    if need_ctx:
        ya = _attn_win(h, p["a_sink"], layer, rg, latent=False, out_rows=rows, dst=ya)
        yb = _attn_diff(h, p["b_lambda"], p["b_subln_g"], layer, rg, latent=False, lam_init=lam_init,
                        out_rows=rows, dst=yb)
        yc = _attn_mla(qc, kv2, h, rg, latent=False, out_rows=rows, dst=yc)
        yd = _fourier_seq(dft["cos_x"], dft["sin_x"], zc, zs, rg, latent=False, out_rows=rows, dst=yd)
    return _out_proj([ya, yb, yc, yd], p["w_out"], layer, rows=rows, tm=rg.tm, tn=512)


def kernel(x, c, ctx, c_ctx, w_mod, b_mod, ffn1_w_gu, ffn1_w_d, ffn2_w_gu, ffn2_w_d, ln_g, ln_b, w_in, w_out,
           a_sink, b_lambda, b_subln_g, c_q_norm_g, c_kv_norm_g, c_w_uq, c_w_ukv, d_w):
    n_batch, seq, d = x.shape
    ctx_len = ctx.shape[1]
    depth = w_mod.shape[0]
    rg = _Rows(n_batch, seq, ctx_len)
    alpha = (2.0 * depth) ** 0.25

    cos_tab, sin_tab = _rope_tables(seq)
    cos_t, sin_t = _dft_tables(seq)
    cos_x, sin_x = _dft_tables(ctx_len)
    cos_c, sin_c = _dft_tables(HEAD_DIM)
    dft = {"cos_t": cos_t.astype(BF16), "sin_t": sin_t.astype(BF16),
           "cos_x": cos_x.astype(BF16), "sin_x": sin_x.astype(BF16),
           "cos_c": cos_c.astype(BF16), "sin_c": sin_c.astype(BF16)}
    consts = (cos_tab, sin_tab, _in_proj_tiles(), dft)

    n_c_rows = 8
    c_rows = jnp.concatenate([c, c_ctx[None, :], jnp.zeros((n_c_rows - n_batch - 1, d), c.dtype)], axis=0)

    p = {"w_in": _prep_w_in(w_in), "w_out": w_out, "a_sink": a_sink, "b_lambda": b_lambda,
         "b_subln_g": b_subln_g, "c_q_norm_g": c_q_norm_g, "c_kv_norm_g": c_kv_norm_g,
         "c_w_uq": _prep_w_uq(c_w_uq), "c_w_ukv": c_w_ukv, "d_w": d_w}
    mod_all = _mod_vectors(c_rows, w_mod, b_mod).reshape(depth, n_c_rows, N_MOD, d)
    mods = [mod_all[l] for l in range(depth)]
    xs, u = _modulate(x.reshape(n_batch * seq, d), ctx.reshape(n_batch * ctx_len, d), mods[0], rg, shift_idx=0)
    pending = None
    for l in range(depth):
        last = l == depth - 1
        lam_init = 0.8 - 0.6 * math.exp(-0.3 * l)
        mod = mods[l]
        if pending is None:
            y = _ffn(u, ffn1_w_gu, ffn1_w_d, l, rows=rg.n, tm=rg.tm)
        else:
            y, xs = _ffn_after_ln(pending, ffn1_w_gu, ffn1_w_d, l, rg, rows=rg.n)
        xs, u = _ln_res(xs, y, mod, ln_g[l, 0], ln_b[l, 0], rg, rows=rg.n, alpha=alpha,
                        gate_idx=2, gate_mul=0.5, next_mod=mod, next_shift_idx=3, name="ln_ffn1")
        y = _mixing(u, p, l, consts, rg, lam_init=lam_init, need_ctx=not last)
        rows = rg.nl if last else rg.n
        ln_mix = {"x": xs, "y": y, "mod": mod, "next_mod": mod, "g": ln_g[l, 1], "b": ln_b[l, 1], "alpha": alpha,
                  "gate_idx": 5, "gate_mul": 1.0, "shift_idx": 6}
        y, xs = _ffn_after_ln(ln_mix, ffn2_w_gu, ffn2_w_d, l, rg, rows=rows)
        if last:
            xs, _ = _ln_res(xs, y, mod, ln_g[l, 2], ln_b[l, 2], rg, rows=rows, alpha=alpha,
                            gate_idx=8, gate_mul=0.5, next_mod=None, next_shift_idx=None, name="ln_ffn2")
        else:
            pending = {"x": xs, "y": y, "mod": mod, "next_mod": mods[l + 1], "g": ln_g[l, 2], "b": ln_b[l, 2],
                       "alpha": alpha, "gate_idx": 8, "gate_mul": 0.5, "shift_idx": 0}
    return xs[:rg.nl].reshape(n_batch, seq, d)
```

```python
import functools
import math

import numpy as np
import jax
import jax.numpy as jnp
from jax import lax
from jax.experimental import pallas as pl
from jax.experimental.pallas import tpu as pltpu

F32 = jnp.float32
BF16 = jnp.bfloat16

GRID_W = 64
HEAD_DIM = 128
WINDOW = 128
N_HEADS = 8
A_KV_HEADS = 2
A_GROUP = N_HEADS // A_KV_HEADS
B_QK_DIM = 64
C_Q_LORA = 1536
C_KV_LORA = 512
C_NOPE = 128
C_ROPE = 64
N_MOD = 9
ROPE_BASE = 10000.0
LN_EPS = 1e-5
RMS_EPS = 1e-6
NEG_INF = -1e30

H_CQ, H_AQ, H_BQ, H_DX = 0, 1536, 2560, 3584
H_AK, H_AV, H_BK, H_BV, H_CKV, H_CKR = 4608, 4864, 5120, 6144, 7168, 7680
H_COLS = 8192

ROPE_NONE, ROPE_A, ROPE_B, ROPE_KR, ROPE_QR = 0, 1, 2, 3, 4
ROPE_HALF_LANES = {ROPE_A: HEAD_DIM // 4, ROPE_B: B_QK_DIM // 4, ROPE_KR: C_ROPE // 4, ROPE_QR: C_ROPE // 4}

VMEM_LIMIT_BYTES = 56 * 1024 * 1024
ROPE_TN = 256
PROJ_HALVES = 2
Q_SUB = 256
LOG2E = math.log2(math.e)


def _cparams(n_axes):
    return pltpu.CompilerParams(dimension_semantics=("arbitrary",) * n_axes,
                                vmem_limit_bytes=VMEM_LIMIT_BYTES)


def _dot(a, b):
    return jnp.dot(a, b, preferred_element_type=F32)


def _dot_nt(a, b):
    return lax.dot_general(a, b, (((1,), (1,)), ((), ())), preferred_element_type=F32)


def _mm_kernel(a_ref, b_ref, o_ref, acc_ref, *, nk):
    k, j = pl.program_id(1), pl.program_id(2)

    def part():
        return _dot(a_ref[...], b_ref[...].astype(BF16))

    @pl.when(k == 0)
    def _():
        acc_ref[j] = part()

    if nk > 2:
        @pl.when((k > 0) & (k < nk - 1))
        def _():
            acc_ref[j] += part()

    @pl.when(k == nk - 1)
    def _():
        o_ref[...] = (acc_ref[j] + part()).astype(o_ref.dtype)


def _mm(a, b, layer, *, rows, tm, tn, tk, name):
    _, kdim, n = b.shape
    assert rows % tm == 0 and n % tn == 0 and kdim % tk == 0 and kdim // tk >= 2
    nk = kdim // tk
    return pl.pallas_call(
        functools.partial(_mm_kernel, nk=nk),
        out_shape=jax.ShapeDtypeStruct((rows, n), BF16),
        grid=(rows // tm, nk, n // tn),
        in_specs=[pl.BlockSpec((tm, tk), lambda i, k, j: (i, k)),
                  pl.BlockSpec((None, tk, tn), lambda i, k, j: (layer, k, j))],
        out_specs=pl.BlockSpec((tm, tn), lambda i, k, j: (i, jnp.where(k == nk - 1, j, 0))),
        scratch_shapes=[pltpu.VMEM((n // tn, tm, tn), F32)],
        compiler_params=_cparams(3),
        name=name,
    )(a, b)


def _out_proj_kernel(*refs):
    *y_refs, w_ref, o_ref = refs
    width = y_refs[0].shape[1]
    acc = None
    for m, y_ref in enumerate(y_refs):
        part = _dot(y_ref[...], w_ref[m * width:(m + 1) * width, :].astype(BF16))
        acc = part if acc is None else acc + part
    o_ref[...] = acc.astype(o_ref.dtype)


def _out_proj(ys, w_out, layer, *, rows, tm, tn):
    _, kdim, n = w_out.shape
    width = ys[0].shape[1]
    assert kdim == width * len(ys)
    y_spec = pl.BlockSpec((tm, width), lambda i, j: (i, 0))
    return pl.pallas_call(
        _out_proj_kernel,
        out_shape=jax.ShapeDtypeStruct((rows, n), BF16),
        grid=(rows // tm, n // tn),
        in_specs=[y_spec] * len(ys) + [pl.BlockSpec((None, kdim, tn), lambda i, j: (layer, 0, j))],
        out_specs=pl.BlockSpec((tm, tn), lambda i, j: (i, j)),
        compiler_params=_cparams(2),
        name="out_proj",
    )(*ys, w_out)


def _mod_kernel(c_ref, w_ref, b_ref, o_ref):
    c = c_ref[...]
    sc = (c * jax.nn.sigmoid(c)).astype(BF16)
    o_ref[...] = _dot(sc, w_ref[...].astype(BF16)) + b_ref[...]


def _mod_vectors(c_rows, w_mod, b_mod):
    depth, d, n = w_mod.shape
    r = c_rows.shape[0]
    tn = 1024
    return pl.pallas_call(
        _mod_kernel,
        out_shape=jax.ShapeDtypeStruct((depth, r, n), F32),
        grid=(depth, n // tn),
        in_specs=[pl.BlockSpec((r, d), lambda l, j: (0, 0)),
                  pl.BlockSpec((None, d, tn), lambda l, j: (l, 0, j)),
                  pl.BlockSpec((None, 1, tn), lambda l, j: (l, 0, j))],
        out_specs=pl.BlockSpec((None, r, tn), lambda l, j: (l, 0, j)),
        compiler_params=_cparams(2),
        name="mod_vectors",
    )(c_rows, w_mod, b_mod.reshape(depth, 1, n))


def _seg_index(i, n_lat_tiles, tiles_per_batch, n_batch):
    return jnp.where(i < n_lat_tiles, i // tiles_per_batch, n_batch)


def _modulate_kernel(x_ref, c_ref, mod_ref, xs_ref, u_ref, *, shift_idx, n_lat_tiles):
    shift = mod_ref[shift_idx:shift_idx + 1, :]
    scale = mod_ref[shift_idx + 1:shift_idx + 2, :]

    def emit(src_ref):
        x = src_ref[...]
        xs_ref[...] = x
        u_ref[...] = (x * (1.0 + scale) + shift).astype(u_ref.dtype)

    @pl.when(pl.program_id(0) < n_lat_tiles)
    def _():
        emit(x_ref)

    @pl.when(pl.program_id(0) >= n_lat_tiles)
    def _():
        emit(c_ref)


def _res_ln(x, y, mod_ref, g_ref, b_ref, *, alpha, gate_idx, gate_mul):
    gate = mod_ref[gate_idx:gate_idx + 1, :]
    z = alpha * x + (gate_mul * gate) * y.astype(F32)
    mu = jnp.mean(z, axis=-1, keepdims=True)
    zc = z - mu
    var = jnp.mean(zc * zc, axis=-1, keepdims=True)
    return zc * lax.rsqrt(var + LN_EPS) * g_ref[...] + b_ref[...]


def _modulated(xn, nmod_ref, shift_idx):
    shift = nmod_ref[shift_idx:shift_idx + 1, :]
    scale = nmod_ref[shift_idx + 1:shift_idx + 2, :]
    return (xn * (1.0 + scale) + shift).astype(BF16)


def _ln_res_kernel(x_ref, y_ref, mod_ref, g_ref, b_ref, *rest, alpha, gate_idx, gate_mul, next_shift_idx):
    xn = _res_ln(x_ref[...], y_ref[...], mod_ref, g_ref, b_ref, alpha=alpha, gate_idx=gate_idx, gate_mul=gate_mul)
    if next_shift_idx is None:
        xo_ref, = rest
        xo_ref[...] = xn
    else:
        nmod_ref, xo_ref, u_ref = rest
        xo_ref[...] = xn
        u_ref[...] = _modulated(xn, nmod_ref, next_shift_idx)


class _Rows:
    def __init__(self, n_batch, seq, ctx_len):
        self.b, self.t, self.c = n_batch, seq, ctx_len
        self.nl, self.nc = n_batch * seq, n_batch * ctx_len
        self.n = self.nl + self.nc
        self.tr = math.gcd(256, math.gcd(seq, ctx_len))
        self.tm = math.gcd(1024, math.gcd(seq, self.nc))

    def mod_spec(self, tile, d):
        n_lat_tiles, per_batch, nb = self.nl // tile, self.t // tile, self.b
        return pl.BlockSpec((None, N_MOD, d),
                            lambda i, *_: (_seg_index(i, n_lat_tiles, per_batch, nb), 0, 0))


def _modulate(x, ctx, mod, rg, *, shift_idx):
    d = x.shape[1]
    tr = rg.tr
    n_lat_tiles, n_ctx_tiles = rg.nl // tr, rg.nc // tr
    row_spec = pl.BlockSpec((tr, d), lambda i: (i, 0))
    x_spec = pl.BlockSpec((tr, d), lambda i: (jnp.minimum(i, n_lat_tiles - 1), 0))
    c_spec = pl.BlockSpec((tr, d), lambda i: (jnp.clip(i - n_lat_tiles, 0, n_ctx_tiles - 1), 0))
    return pl.pallas_call(
        functools.partial(_modulate_kernel, shift_idx=shift_idx, n_lat_tiles=n_lat_tiles),
        out_shape=[jax.ShapeDtypeStruct((rg.n, d), F32), jax.ShapeDtypeStruct((rg.n, d), BF16)],
        grid=(rg.n // tr,),
        in_specs=[x_spec, c_spec, rg.mod_spec(tr, d)],
        out_specs=[row_spec, row_spec],
        compiler_params=_cparams(1),
        name="modulate",
    )(x, ctx, mod)


def _ln_res(x, y, mod, ln_g, ln_b, rg, *, rows, alpha, gate_idx, gate_mul, next_mod, next_shift_idx, name,
            out_rows=None):
    d = x.shape[1]
    tr = rg.tr
    row_spec = pl.BlockSpec((tr, d), lambda i: (i, 0))
    vec_spec = pl.BlockSpec((1, d), lambda i: (0, 0))
    out_shape = [jax.ShapeDtypeStruct((out_rows or rows, d), F32)]
    out_specs = [row_spec]
    in_specs = [row_spec, row_spec, rg.mod_spec(tr, d), vec_spec, vec_spec]
    args = [x, y, mod, ln_g.reshape(1, d), ln_b.reshape(1, d)]
    if next_shift_idx is not None:
        in_specs.append(rg.mod_spec(tr, d))
        args.append(next_mod)
        out_shape.append(jax.ShapeDtypeStruct((rows, d), BF16))
        out_specs.append(row_spec)
    res = pl.pallas_call(
        functools.partial(_ln_res_kernel, alpha=alpha, gate_idx=gate_idx, gate_mul=gate_mul,
                          next_shift_idx=next_shift_idx),
        out_shape=out_shape,
        grid=(rows // tr,),
        in_specs=in_specs,
        out_specs=out_specs,
        compiler_params=_cparams(1),
        name=name,
    )(*args)
    return (res[0], res[1]) if next_shift_idx is not None else (res[0], None)


def _ffn_up_kernel(a_ref, wg_ref, wu_ref, o_ref):
    a = a_ref[...]
    g = _dot(a, wg_ref[...].astype(BF16))
    up = _dot(a, wu_ref[...].astype(BF16))
    o_ref[...] = (g * jax.nn.sigmoid(g) * up).astype(o_ref.dtype)


def _ffn_up(u, w_gu, layer, *, rows, tm, tn):
    _, d, two_ff = w_gu.shape
    ff = two_ff // 2
    up0 = ff // tn
    return pl.pallas_call(
        _ffn_up_kernel,
        out_shape=jax.ShapeDtypeStruct((rows, ff), BF16),
        grid=(rows // tm, ff // tn),
        in_specs=[pl.BlockSpec((tm, d), lambda i, j: (i, 0)),
                  pl.BlockSpec((None, d, tn), lambda i, j: (layer, 0, j)),
                  pl.BlockSpec((None, d, tn), lambda i, j: (layer, 0, up0 + j))],
        out_specs=pl.BlockSpec((tm, tn), lambda i, j: (i, j)),
        compiler_params=_cparams(2),
        name="ffn_up",
    )(u, w_gu, w_gu)


def _ffn_up_ln_kernel(x_ref, y_ref, mod_ref, nmod_ref, g_ref, b_ref, u0_ref, _xs_head, wg_ref, wu_ref,
                      hid_ref, xo_ref, a_even, a_odd, *, alpha, gate_idx, gate_mul, shift_idx, chunk):
    i, j = pl.program_id(0), pl.program_id(1)

    @pl.when((i == 0) & (j == 0))
    def _():
        a_even[...] = u0_ref[...]

    def step(a_cur, a_next):
        xn = _res_ln(x_ref[...], y_ref[...], mod_ref, g_ref, b_ref,
                     alpha=alpha, gate_idx=gate_idx, gate_mul=gate_mul)
        xo_ref[...] = xn
        a_next[pl.ds(pl.multiple_of(j * chunk, chunk), chunk), :] = _modulated(xn, nmod_ref, shift_idx)
        a = a_cur[...]
        g = _dot(a, wg_ref[...].astype(BF16))
        up = _dot(a, wu_ref[...].astype(BF16))
        hid_ref[...] = (g * jax.nn.sigmoid(g) * up).astype(hid_ref.dtype)

    @pl.when(i % 2 == 0)
    def _():
        step(a_even, a_odd)

    @pl.when(i % 2 == 1)
    def _():
        step(a_odd, a_even)


def _ffn_up_ln(ln, w_gu, layer, rg, *, rows, tm, tn):
    _, d, two_ff = w_gu.shape
    ff = two_ff // 2
    n_i, n_j = rows // tm, ff // tn
    chunk = tm // n_j
    assert rows % tm == 0 and tm % n_j == 0 and chunk % 16 == 0 and rg.t % chunk == 0 and rg.nl % tm == 0
    xs_head, u0 = _ln_res(ln["x"], ln["y"], ln["mod"], ln["g"], ln["b"], rg, rows=tm, out_rows=rows,
                          alpha=ln["alpha"], gate_idx=ln["gate_idx"], gate_mul=ln["gate_mul"],
                          next_mod=ln["next_mod"], next_shift_idx=ln["shift_idx"], name="ln_head")

    def chunk_block(i, j):
        return jnp.where(i < n_i - 1, (i + 1) * n_j + j, n_i * n_j - 1)

    def seg(i, j):
        row0 = chunk_block(i, j) * chunk
        return jnp.where(row0 < rg.nl, row0 // rg.t, rg.b)

    chunk_spec = pl.BlockSpec((chunk, d), lambda i, j: (chunk_block(i, j), 0))
    mod_spec = pl.BlockSpec((None, N_MOD, d), lambda i, j: (seg(i, j), 0, 0))
    vec_spec = pl.BlockSpec((1, d), lambda i, j: (0, 0))
    up0 = ff // tn
    hid, xs_new = pl.pallas_call(
        functools.partial(_ffn_up_ln_kernel, alpha=ln["alpha"], gate_idx=ln["gate_idx"], gate_mul=ln["gate_mul"],
                          shift_idx=ln["shift_idx"], chunk=chunk),
        out_shape=[jax.ShapeDtypeStruct((rows, ff), BF16), jax.ShapeDtypeStruct((rows, d), F32)],
        grid=(n_i, n_j),
        in_specs=[chunk_spec, chunk_spec, mod_spec, mod_spec, vec_spec, vec_spec,
                  pl.BlockSpec((tm, d), lambda i, j: (0, 0), pipeline_mode=pl.Buffered(1)),
                  pl.BlockSpec(memory_space=pl.ANY),
                  pl.BlockSpec((None, d, tn), lambda i, j: (layer, 0, j)),
                  pl.BlockSpec((None, d, tn), lambda i, j: (layer, 0, up0 + j))],
        out_specs=[pl.BlockSpec((tm, tn), lambda i, j: (i, j)), chunk_spec],
        scratch_shapes=[pltpu.VMEM((tm, d), BF16), pltpu.VMEM((tm, d), BF16)],
        input_output_aliases={7: 1},
        compiler_params=_cparams(2),
        name="ffn_up_ln",
    )(ln["x"], ln["y"], ln["mod"], ln["next_mod"], ln["g"].reshape(1, d), ln["b"].reshape(1, d), u0, xs_head,
      w_gu, w_gu)
    return hid, xs_new


def _rope_tile(x, cos, sin, half):
    lane = lax.broadcasted_iota(jnp.int32, x.shape, 1)
    first = (lane % (2 * half)) < half
    width = x.shape[1]
    partner = jnp.where(first, pltpu.roll(x, width - half, 1), pltpu.roll(x, half, 1))
    return x * cos + partner * sin


def _proj_kernel(kind_ref, src_ref, scale_ref, a_ref, b_ref, *rest, has_gain, n_lat_tiles, rope_kinds):
    if has_gain:
        gain_ref, *tab_refs, o_ref, an_ref = rest
    else:
        *tab_refs, o_ref = rest
    tables = {kind: (tab_refs[2 * n], tab_refs[2 * n + 1]) for n, kind in enumerate(rope_kinds)}
    i, j = pl.program_id(0), pl.program_id(1)

    if has_gain:
        @pl.when(j == 0)
        def _():
            x = a_ref[...].astype(F32)
            ms = jnp.mean(x * x, axis=-1, keepdims=True)
            an_ref[...] = (x * lax.rsqrt(ms + RMS_EPS) * gain_ref[...]).astype(BF16)
    lhs_ref = an_ref if has_gain else a_ref

    for half in range(PROJ_HALVES):
        cols = slice(half * ROPE_TN, (half + 1) * ROPE_TN)
        tile = PROJ_HALVES * j + half
        kind = jnp.where(i < n_lat_tiles, kind_ref[tile], ROPE_NONE)
        out_scale = scale_ref[tile]

        def acc():
            return _dot(lhs_ref[...], b_ref[:, cols].astype(BF16)) * out_scale

        @pl.when(kind == ROPE_NONE)
        def _():
            o_ref[:, cols] = acc().astype(o_ref.dtype)

        for rope_kind, (cos_ref, sin_ref) in tables.items():
            @pl.when(kind == rope_kind)
            def _(cos_ref=cos_ref, sin_ref=sin_ref, rope_kind=rope_kind):
                roped = _rope_tile(acc(), cos_ref[...], sin_ref[...], ROPE_HALF_LANES[rope_kind])
                o_ref[:, cols] = roped.astype(o_ref.dtype)


def _proj(a, w, layer, kinds, scales, src_tiles, cos_tab, sin_tab, rg, *, rows, rope_kinds, a_col0=0, gain=None,
          name):
    _, kdim, n = w.shape
    tm, tn = rg.tm, PROJ_HALVES * ROPE_TN
    assert a_col0 % kdim == 0 and n % tn == 0
    a_blk = a_col0 // kdim
    n_lat_tiles = rg.nl // tm
    t_tiles = rg.t // tm
    has_gain = gain is not None

    def tab_spec(kind):
        return pl.BlockSpec((None, tm, ROPE_TN), lambda i, j, kr, sr: (kind, i % t_tiles, 0))

    in_specs = [pl.BlockSpec(memory_space=pltpu.SMEM),
                pl.BlockSpec((tm, kdim), lambda i, j, kr, sr: (i, a_blk)),
                pl.BlockSpec((None, kdim, tn), lambda i, j, kr, sr: (layer, 0, sr[j]))]
    args = [scales, a, w]
    scratch = []
    if has_gain:
        in_specs.append(pl.BlockSpec((None, 1, kdim), lambda i, j, kr, sr: (layer, 0, 0)))
        args.append(gain.reshape(gain.shape[0], 1, kdim))
        scratch.append(pltpu.VMEM((tm, kdim), BF16))
    for kind in rope_kinds:
        in_specs += [tab_spec(kind), tab_spec(kind)]
        args += [cos_tab, sin_tab]
    return pl.pallas_call(
        functools.partial(_proj_kernel, has_gain=has_gain, n_lat_tiles=n_lat_tiles, rope_kinds=tuple(rope_kinds)),
        out_shape=jax.ShapeDtypeStruct((rows, n), BF16),
        grid_spec=pltpu.PrefetchScalarGridSpec(
            num_scalar_prefetch=2,
            grid=(rows // tm, n // tn),
            in_specs=in_specs,
            out_specs=pl.BlockSpec((tm, tn), lambda i, j, kr, sr: (i, j)),
            scratch_shapes=scratch),
        compiler_params=_cparams(2),
        name=name,
    )(kinds, src_tiles, *args)


def _rope_tables(seq):
    t = np.arange(seq)
    row, col = (t // GRID_W).astype(np.float64), (t % GRID_W).astype(np.float64)

    def pattern(rot_dim):
        axis_dim = rot_dim // 2
        inv = ROPE_BASE ** (-np.arange(0, axis_dim, 2, dtype=np.float64) / axis_dim)
        ar, ac = row[:, None] * inv[None, :], col[:, None] * inv[None, :]
        cos = np.concatenate([np.cos(ar), np.cos(ar), np.cos(ac), np.cos(ac)], axis=1)
        sin = np.concatenate([-np.sin(ar), np.sin(ar), -np.sin(ac), np.sin(ac)], axis=1)
        return cos, sin

    cos = np.ones((5, seq, ROPE_TN), np.float64)
    sin = np.zeros((5, seq, ROPE_TN), np.float64)
    c128, s128 = pattern(HEAD_DIM)
    c64, s64 = pattern(2 * 32)
    cos[ROPE_A], sin[ROPE_A] = np.tile(c128, (1, 2)), np.tile(s128, (1, 2))
    cos[ROPE_B], sin[ROPE_B] = np.tile(c64, (1, 4)), np.tile(s64, (1, 4))
    cos[ROPE_KR, :, :64], sin[ROPE_KR, :, :64] = c64, s64
    cos[ROPE_QR, :, 128:192], sin[ROPE_QR, :, 128:192] = c64, s64
    return jnp.asarray(cos, F32), jnp.asarray(sin, F32)


def _mixer_call(body, rg, *, latent, dst, out_rows, grid, in_specs, args, tq, n_q, out_block_w, name):
    row0 = 0 if latent else rg.nl // tq
    aliases = {}
    if not latent:
        in_specs = in_specs + [pl.BlockSpec(memory_space=pl.ANY)]
        args = args + [dst]
        aliases = {len(args) - 1: 0}
    return pl.pallas_call(
        body,
        out_shape=jax.ShapeDtypeStruct((out_rows, N_HEADS * HEAD_DIM), BF16),
        grid=grid,
        in_specs=in_specs,
        out_specs=pl.BlockSpec((tq, out_block_w), lambda b, hh, qi: (row0 + b * n_q + qi, hh)),
        input_output_aliases=aliases,
        compiler_params=_cparams(3),
        name=name,
    )(*args)


def _attn_win_kernel(sink_ref, q_ref, kc_ref, vc_ref, *rest, has_lat, tq, seq, layer):
    if has_lat:
        kl_ref, vl_ref, o_ref = rest
    else:
        _, o_ref = rest
    g, qi = pl.program_id(1), pl.program_id(2)
    kc, vc = kc_ref[...], vc_ref[...]
    sub = min(tq, Q_SUB)
    for r in range(tq // sub):
        rows = slice(r * sub, (r + 1) * sub)
        row0 = qi * tq + r * sub
        if has_lat:
            win = min(seq, sub + 2 * WINDOW)
            ws = pl.multiple_of(jnp.clip(row0 - WINDOW, 0, seq - win), WINDOW)
            kw, vw = kl_ref[pl.ds(ws, win), :], vl_ref[pl.ds(ws, win), :]
            qpos = row0 + lax.broadcasted_iota(jnp.int32, (sub, win), 0)
            kpos = ws + lax.broadcasted_iota(jnp.int32, (sub, win), 1)
            valid = jnp.abs(qpos - kpos) <= WINDOW
        for j in range(A_GROUP):
            cols = slice(j * HEAD_DIM, (j + 1) * HEAD_DIM)
            q = q_ref[rows, cols]
            sink = sink_ref[layer, g * A_GROUP + j] * LOG2E
            s_c = _dot_nt(q, kc)
            m = jnp.maximum(jnp.max(s_c, axis=-1, keepdims=True), sink)
            if has_lat:
                s_l = jnp.where(valid, _dot_nt(q, kw), NEG_INF)
                m = jnp.maximum(m, jnp.max(s_l, axis=-1, keepdims=True))
            p_c = jnp.exp2(s_c - m)
            den = jnp.sum(p_c, axis=-1, keepdims=True) + jnp.exp2(sink - m)
            o = _dot(p_c.astype(BF16), vc)
            if has_lat:
                p_l = jnp.exp2(s_l - m)
                den = den + jnp.sum(p_l, axis=-1, keepdims=True)
                o = o + _dot(p_l.astype(BF16), vw)
            o_ref[rows, cols] = (o / den).astype(o_ref.dtype)


def _attn_win(h, a_sink, layer, rg, *, latent, out_rows, dst=None):
    gw = A_GROUP * HEAD_DIM
    tq, n_q, q0 = _q_geometry(rg, latent, 1024)
    c_blk0 = rg.nl // rg.c
    in_specs = [pl.BlockSpec(memory_space=pltpu.SMEM),
                pl.BlockSpec((tq, gw), lambda b, g, qi: (q0 + b * n_q + qi, H_AQ // gw + g)),
                pl.BlockSpec((rg.c, HEAD_DIM), lambda b, g, qi: (c_blk0 + b, H_AK // HEAD_DIM + g)),
                pl.BlockSpec((rg.c, HEAD_DIM), lambda b, g, qi: (c_blk0 + b, H_AV // HEAD_DIM + g))]
    args = [a_sink, h, h, h]
    if latent:
        in_specs += [pl.BlockSpec((rg.t, HEAD_DIM), lambda b, g, qi: (b, H_AK // HEAD_DIM + g)),
                     pl.BlockSpec((rg.t, HEAD_DIM), lambda b, g, qi: (b, H_AV // HEAD_DIM + g))]
        args += [h, h]
    return _mixer_call(functools.partial(_attn_win_kernel, has_lat=latent, tq=tq, seq=rg.t, layer=layer), rg,
                       latent=latent, dst=dst, out_rows=out_rows, grid=(rg.b, A_KV_HEADS, n_q), in_specs=in_specs, args=args,
                       tq=tq, n_q=n_q, out_block_w=gw, name="attn_win_lat" if latent else "attn_win_ctx")


def _softmax_pv(q, keys, values):
    scores = [_dot_nt(q, k) for k in keys]
    m = functools.reduce(jnp.maximum, [jnp.max(s, axis=-1, keepdims=True) for s in scores])
    den, out = None, None
    for s, v in zip(scores, values):
        p = jnp.exp2(s - m)
        d = jnp.sum(p, axis=-1, keepdims=True)
        o = _dot(p.astype(BF16), v)
        den = d if den is None else den + d
        out = o if out is None else out + o
    return out / den


def _attn_diff_kernel(lam_ref, gsub_ref, q_ref, kc_ref, vc_ref, *rest, has_lat, lam_init):
    if has_lat:
        kl_ref, vl_ref, o_ref = rest
    else:
        _, o_ref = rest
    lf = lam_ref[...]
    lam = (jnp.exp(jnp.sum(lf[0:1] * lf[1:2], axis=-1, keepdims=True))
           - jnp.exp(jnp.sum(lf[2:3] * lf[3:4], axis=-1, keepdims=True)) + lam_init)
    keys = [kc_ref[...]] + ([kl_ref[...]] if has_lat else [])
    values = [vc_ref[...]] + ([vl_ref[...]] if has_lat else [])
    tq = q_ref.shape[0]
    sub = min(tq, Q_SUB)
    for r in range(tq // sub):
        q = q_ref[r * sub:(r + 1) * sub, :]
        first_map = lax.broadcasted_iota(jnp.int32, q.shape, 1) < B_QK_DIM
        zero = jnp.zeros_like(q)

        o = (_softmax_pv(jnp.where(first_map, q, zero), keys, values)
             - lam * _softmax_pv(jnp.where(first_map, zero, q), keys, values))
        ms = jnp.mean(o * o, axis=-1, keepdims=True)
        o = o * lax.rsqrt(ms + RMS_EPS) * gsub_ref[...] * (1.0 - lam_init)
        o_ref[r * sub:(r + 1) * sub, :] = o.astype(o_ref.dtype)


def _kv_specs(rg, col_k, col_v, latent, k_width=HEAD_DIM):
    c_blk0 = rg.nl // rg.c
    specs = [pl.BlockSpec((rg.c, k_width), lambda b, hh, qi: (c_blk0 + b, col_k(hh))),
             pl.BlockSpec((rg.c, HEAD_DIM), lambda b, hh, qi: (c_blk0 + b, col_v(hh)))]
    if latent:
        specs += [pl.BlockSpec((rg.t, k_width), lambda b, hh, qi: (b, col_k(hh))),
                  pl.BlockSpec((rg.t, HEAD_DIM), lambda b, hh, qi: (b, col_v(hh)))]
    return specs


def _q_geometry(rg, latent, tq_lat):
    if latent:
        tq = min(tq_lat, rg.t)
        return tq, rg.t // tq, 0
    return rg.c, 1, rg.nl // rg.c


def _attn_diff(h, b_lambda, b_subln_g, layer, rg, *, latent, lam_init, out_rows, dst=None):
    tq, n_q, q0 = _q_geometry(rg, latent, 2048)
    in_specs = [pl.BlockSpec((None, 4, B_QK_DIM), lambda b, hh, qi: (layer, 0, 0)),
                pl.BlockSpec((None, 1, HEAD_DIM), lambda b, hh, qi: (layer, 0, 0)),
                pl.BlockSpec((tq, HEAD_DIM), lambda b, hh, qi: (q0 + b * n_q + qi, H_BQ // HEAD_DIM + hh))]
    in_specs += _kv_specs(rg, lambda hh: H_BK // HEAD_DIM + hh, lambda hh: H_BV // HEAD_DIM + hh, latent)
    n_kv = 4 if latent else 2
    args = [b_lambda, b_subln_g.reshape(b_subln_g.shape[0], 1, HEAD_DIM), h] + [h] * n_kv
    return _mixer_call(functools.partial(_attn_diff_kernel, has_lat=latent, lam_init=lam_init), rg,
                       latent=latent, dst=dst, out_rows=out_rows, grid=(rg.b, N_HEADS, n_q), in_specs=in_specs, args=args,
                       tq=tq, n_q=n_q, out_block_w=HEAD_DIM, name="attn_diff_lat" if latent else "attn_diff_ctx")


def _attn_mla_kernel(q_ref, knc_ref, vc_ref, krc_ref, *rest, has_lat):
    if has_lat:
        knl_ref, vl_ref, krl_ref, o_ref = rest
    else:
        _, o_ref = rest
    segs = [(knc_ref, krc_ref, vc_ref)] + ([(knl_ref, krl_ref, vl_ref)] if has_lat else [])
    keys = [jnp.concatenate([kn[...], kr[...]], axis=1) for kn, kr, _ in segs]
    values = [v[...] for _, _, v in segs]
    tq = q_ref.shape[0]
    sub = min(tq, Q_SUB)
    for r in range(tq // sub):
        rows = slice(r * sub, (r + 1) * sub)
        o_ref[rows, :] = _softmax_pv(q_ref[rows, :], keys, values).astype(o_ref.dtype)


def _attn_mla(qc, kv2, h, rg, *, latent, out_rows, dst=None):
    tq, n_q, q0 = _q_geometry(rg, latent, 2048)
    c_blk0 = rg.nl // rg.c
    kr_col = H_CKR // HEAD_DIM
    kv = _kv_specs(rg, lambda hh: 2 * hh, lambda hh: 2 * hh + 1, latent)
    in_specs = [pl.BlockSpec((tq, 2 * HEAD_DIM), lambda b, hh, qi: (q0 + b * n_q + qi, hh)),
                kv[0], kv[1],
                pl.BlockSpec((rg.c, HEAD_DIM), lambda b, hh, qi: (c_blk0 + b, kr_col))]
    args = [qc, kv2, kv2, h]
    if latent:
        in_specs += [kv[2], kv[3], pl.BlockSpec((rg.t, HEAD_DIM), lambda b, hh, qi: (b, kr_col))]
        args += [kv2, kv2, h]
    return _mixer_call(functools.partial(_attn_mla_kernel, has_lat=latent), rg,
                       latent=latent, dst=dst, out_rows=out_rows, grid=(rg.b, N_HEADS, n_q), in_specs=in_specs, args=args,
                       tq=tq, n_q=n_q, out_block_w=HEAD_DIM, name="attn_mla_lat" if latent else "attn_mla_ctx")


def _dft_tables(n):
    idx = np.arange(n, dtype=np.int64)
    ang = (np.outer(idx, idx) % n).astype(np.float64) * (2.0 * np.pi / n)
    return jnp.asarray(np.cos(ang), F32), jnp.asarray(np.sin(ang), F32)


def _fourier_chan_kernel(z_ref, cc_ref, sc_ref, dw_ref, zc_ref, zs_ref):
    for g in range(dw_ref.shape[0]):
        cols = slice(g * HEAD_DIM, (g + 1) * HEAD_DIM)
        dw = dw_ref[g].astype(BF16)
        z = z_ref[:, cols]
        zc_ref[:, cols] = _dot(z, _dot(cc_ref[...], dw).astype(BF16)).astype(zc_ref.dtype)
        zs_ref[:, cols] = _dot(z, _dot(sc_ref[...], dw).astype(BF16)).astype(zs_ref.dtype)


def _fourier_chan(h, cos_c, sin_c, d_w, layer, rg, *, rows):
    tm = rg.tm
    groups = N_HEADS // 2
    width = groups * HEAD_DIM
    assert H_DX % width == 0
    tile = pl.BlockSpec((tm, width), lambda i, s: (i, s))
    const = pl.BlockSpec((HEAD_DIM, HEAD_DIM), lambda i, s: (0, 0))
    out = jax.ShapeDtypeStruct((rows, N_HEADS * HEAD_DIM), BF16)
    return pl.pallas_call(
        _fourier_chan_kernel,
        out_shape=[out, out],
        grid=(rows // tm, N_HEADS // groups),
        in_specs=[pl.BlockSpec((tm, width), lambda i, s: (i, H_DX // width + s)), const, const,
                  pl.BlockSpec((None, groups, HEAD_DIM, HEAD_DIM), lambda i, s: (layer, s, 0, 0))],
        out_specs=[tile, tile],
        compiler_params=_cparams(2),
        name="fourier_chan",
    )(h, cos_c, sin_c, d_w)


def _fourier_seq_kernel(ct_ref, st_ref, zc_ref, zs_ref, *rest, norm):
    o_ref = rest[-1]
    o = _dot(ct_ref[...], zc_ref[...]) - _dot(st_ref[...], zs_ref[...])
    o_ref[...] = (o * norm).astype(o_ref.dtype)


def _fourier_seq(cos_t, sin_t, zc, zs, rg, *, latent, out_rows, dst=None):
    length = rg.t if latent else rg.c
    tm = min(512, length)
    n_i = length // tm
    z_blk0 = 0 if latent else rg.nl // rg.c
    row0 = 0 if latent else rg.nl // tm
    width = zc.shape[1]
    dft_spec = pl.BlockSpec((tm, length), lambda b, i: (i, 0))
    z_spec = pl.BlockSpec((length, width), lambda b, i: (z_blk0 + b, 0))
    in_specs, args, aliases = [dft_spec, dft_spec, z_spec, z_spec], [cos_t, sin_t, zc, zs], {}
    if not latent:
        in_specs.append(pl.BlockSpec(memory_space=pl.ANY))
        args.append(dst)
        aliases = {len(args) - 1: 0}
    return pl.pallas_call(
        functools.partial(_fourier_seq_kernel, norm=(length * HEAD_DIM) ** -0.5),
        out_shape=jax.ShapeDtypeStruct((out_rows, width), BF16),
        grid=(rg.b, n_i),
        in_specs=in_specs,
        out_specs=pl.BlockSpec((tm, width), lambda b, i: (row0 + b * n_i + i, 0)),
        input_output_aliases=aliases,
        compiler_params=_cparams(2),
        name="fourier_seq_lat" if latent else "fourier_seq_ctx",
    )(*args)


def _prep_w_in(w_in):
    return jnp.pad(w_in, ((0, 0), (0, 0), (0, H_COLS - w_in.shape[2]))).astype(BF16)


def _prep_w_uq(w_uq):
    depth, k, _ = w_uq.shape
    w = w_uq.reshape(depth, k, N_HEADS, C_NOPE + C_ROPE)
    w = jnp.pad(w, ((0, 0), (0, 0), (0, 0), (0, 2 * HEAD_DIM - C_NOPE - C_ROPE)))
    return w.reshape(depth, k, N_HEADS * 2 * HEAD_DIM).astype(BF16)


def _in_proj_tiles():
    step = PROJ_HALVES * ROPE_TN
    src_cols = {"aq": 0, "bq": 1024, "cq": 2048, "dx": 3584, "ak": 4608, "bk": 5120, "bv": 6144,
                "ckv": 7168, "ckr": 7680}
    layout = [("cq", H_CQ, H_AQ, (ROPE_NONE, ROPE_NONE), 1.0),
              ("aq", H_AQ, H_BQ, (ROPE_A, ROPE_A), HEAD_DIM ** -0.5 * LOG2E),
              ("bq", H_BQ, H_DX, (ROPE_B, ROPE_B), B_QK_DIM ** -0.5 * LOG2E),
              ("dx", H_DX, H_AK, (ROPE_NONE, ROPE_NONE), 1.0),
              ("ak", H_AK, H_BK, (ROPE_A, ROPE_NONE), 1.0),
              ("bk", H_BK, H_BV, (ROPE_B, ROPE_B), 1.0),
              ("bv", H_BV, H_CKV, (ROPE_NONE, ROPE_NONE), 1.0),
              ("ckv", H_CKV, H_CKR, (ROPE_NONE, ROPE_NONE), 1.0),
              ("ckr", H_CKR, H_COLS, (ROPE_KR, ROPE_NONE), 1.0)]
    kinds = np.zeros((H_COLS // ROPE_TN,), np.int32)
    scales = np.ones((H_COLS // ROPE_TN,), np.float32)
    src = np.zeros((H_COLS // step,), np.int32)
    for name, lo, hi, kind, scale in layout:
        assert lo % step == 0 and hi % step == 0 and src_cols[name] % step == 0
        n = (hi - lo) // step
        kinds[lo // ROPE_TN:hi // ROPE_TN] = np.tile(np.asarray(kind, np.int32), n)
        scales[lo // ROPE_TN:hi // ROPE_TN] = scale
        src[lo // step:hi // step] = src_cols[name] // step + np.arange(n)
    return jnp.asarray(kinds), jnp.asarray(scales), jnp.asarray(src)


def _ffn(u, w_gu, w_d, layer, *, rows, tm):
    tm_up = next(t for t in (3 * tm // 2, tm) if rows % t == 0)
    hid = _ffn_up(u, w_gu, layer, rows=rows, tm=tm_up, tn=256)
    return _mm(hid, w_d, layer, rows=rows, tm=tm, tn=512, tk=w_d.shape[1] // 2, name="ffn_down")


def _ffn_after_ln(ln, w_gu, w_d, layer, rg, *, rows):
    hid, xs_new = _ffn_up_ln(ln, w_gu, layer, rg, rows=rows, tm=rg.tm, tn=256)
    y = _mm(hid, w_d, layer, rows=rows, tm=rg.tm, tn=512, tk=w_d.shape[1] // 2, name="ffn_down")
    return y, xs_new


def _mixing(u, p, layer, consts, rg, *, lam_init, need_ctx):
    cos_tab, sin_tab, (kinds_in, scales_in, src_in), dft = consts
    rows = rg.n if need_ctx else rg.nl
    h = _proj(u, p["w_in"], layer, kinds_in, scales_in, src_in, cos_tab, sin_tab, rg, rows=rg.n,
              rope_kinds=(ROPE_A, ROPE_B, ROPE_KR), name="in_proj")
    n_tiles = N_HEADS * 2 * HEAD_DIM // ROPE_TN
    ident = jnp.arange(n_tiles // PROJ_HALVES, dtype=jnp.int32)
    q_scale = jnp.full((n_tiles,), (C_NOPE + C_ROPE) ** -0.5 * LOG2E, F32)
    qc = _proj(h, p["c_w_uq"], layer, jnp.full((n_tiles,), ROPE_QR, jnp.int32), q_scale, ident, cos_tab, sin_tab,
               rg, rows=rows, rope_kinds=(ROPE_QR,), a_col0=H_CQ, gain=p["c_q_norm_g"], name="mla_q_up")
    kv2 = _proj(h, p["c_w_ukv"], layer, jnp.zeros((n_tiles,), jnp.int32), jnp.ones((n_tiles,), F32), ident,
                cos_tab, sin_tab, rg, rows=rg.n, rope_kinds=(), a_col0=H_CKV, gain=p["c_kv_norm_g"],
                name="mla_kv_up")
    zc, zs = _fourier_chan(h, dft["cos_c"], dft["sin_c"], p["d_w"], layer, rg, rows=rows)

    ya = _attn_win(h, p["a_sink"], layer, rg, latent=True, out_rows=rows)
    yb = _attn_diff(h, p["b_lambda"], p["b_subln_g"], layer, rg, latent=True, lam_init=lam_init, out_rows=rows)
    yc = _attn_mla(qc, kv2, h, rg, latent=True, out_rows=rows)
    yd = _fourier_seq(dft["cos_t"], dft["sin_t"], zc, zs, rg, latent=True, out_rows=rows)
    if need_ctx:
        ya = _attn_win(h, p["a_sink"], layer, rg, latent=False, out_rows=rows, dst=ya)
        yb = _attn_diff(h, p["b_lambda"], p["b_subln_g"], layer, rg, latent=False, lam_init=lam_init,
                        out_rows=rows, dst=yb)
        yc = _attn_mla(qc, kv2, h, rg, latent=False, out_rows=rows, dst=yc)
        yd = _fourier_seq(dft["cos_x"], dft["sin_x"], zc, zs, rg, latent=False, out_rows=rows, dst=yd)
    return _out_proj([ya, yb, yc, yd], p["w_out"], layer, rows=rows, tm=rg.tm, tn=512)


def kernel(x, c, ctx, c_ctx, w_mod, b_mod, ffn1_w_gu, ffn1_w_d, ffn2_w_gu, ffn2_w_d, ln_g, ln_b, w_in, w_out,
           a_sink, b_lambda, b_subln_g, c_q_norm_g, c_kv_norm_g, c_w_uq, c_w_ukv, d_w):
    n_batch, seq, d = x.shape
    ctx_len = ctx.shape[1]
    depth = w_mod.shape[0]
    rg = _Rows(n_batch, seq, ctx_len)
    alpha = (2.0 * depth) ** 0.25

    cos_tab, sin_tab = _rope_tables(seq)
    cos_t, sin_t = _dft_tables(seq)
    cos_x, sin_x = _dft_tables(ctx_len)
    cos_c, sin_c = _dft_tables(HEAD_DIM)
    dft = {"cos_t": cos_t.astype(BF16), "sin_t": sin_t.astype(BF16),
           "cos_x": cos_x.astype(BF16), "sin_x": sin_x.astype(BF16),
           "cos_c": cos_c.astype(BF16), "sin_c": sin_c.astype(BF16)}
    consts = (cos_tab, sin_tab, _in_proj_tiles(), dft)

    n_c_rows = 8
    c_rows = jnp.concatenate([c, c_ctx[None, :], jnp.zeros((n_c_rows - n_batch - 1, d), c.dtype)], axis=0)

    p = {"w_in": _prep_w_in(w_in), "w_out": w_out, "a_sink": a_sink, "b_lambda": b_lambda,
         "b_subln_g": b_subln_g, "c_q_norm_g": c_q_norm_g, "c_kv_norm_g": c_kv_norm_g,
         "c_w_uq": _prep_w_uq(c_w_uq), "c_w_ukv": c_w_ukv, "d_w": d_w}
    mod_all = _mod_vectors(c_rows, w_mod, b_mod).reshape(depth, n_c_rows, N_MOD, d)
    mods = [mod_all[l] for l in range(depth)]
    xs, u = _modulate(x.reshape(n_batch * seq, d), ctx.reshape(n_batch * ctx_len, d), mods[0], rg, shift_idx=0)
    pending = None
    for l in range(depth):
        last = l == depth - 1
        lam_init = 0.8 - 0.6 * math.exp(-0.3 * l)
        mod = mods[l]
        if pending is None:
            y = _ffn(u, ffn1_w_gu, ffn1_w_d, l, rows=rg.n, tm=rg.tm)
        else:
            y, xs = _ffn_after_ln(pending, ffn1_w_gu, ffn1_w_d, l, rg, rows=rg.n)
        xs, u = _ln_res(xs, y, mod, ln_g[l, 0], ln_b[l, 0], rg, rows=rg.n, alpha=alpha,
                        gate_idx=2, gate_mul=0.5, next_mod=mod, next_shift_idx=3, name="ln_ffn1")
        y = _mixing(u, p, l, consts, rg, lam_init=lam_init, need_ctx=not last)
        rows = rg.nl if last else rg.n
        ln_mix = {"x": xs, "y": y, "mod": mod, "next_mod": mod, "g": ln_g[l, 1], "b": ln_b[l, 1], "alpha": alpha,
                  "gate_idx": 5, "gate_mul": 1.0, "shift_idx": 6}
        y, xs = _ffn_after_ln(ln_mix, ffn2_w_gu, ffn2_w_d, l, rg, rows=rows)
        if last:
            xs, _ = _ln_res(xs, y, mod, ln_g[l, 2], ln_b[l, 2], rg, rows=rows, alpha=alpha,
                            gate_idx=8, gate_mul=0.5, next_mod=None, next_shift_idx=None, name="ln_ffn2")
        else:
            pending = {"x": xs, "y": y, "mod": mod, "next_mod": mods[l + 1], "g": ln_g[l, 2], "b": ln_b[l, 2],
                       "alpha": alpha, "gate_idx": 8, "gate_mul": 0.5, "shift_idx": 0}
    return xs[:rg.nl].reshape(n_batch, seq, d)
```

```python
import functools
import math

import numpy as np
import jax
import jax.numpy as jnp
from jax import lax
from jax.experimental import pallas as pl
from jax.experimental.pallas import tpu as pltpu

F32 = jnp.float32
BF16 = jnp.bfloat16

GRID_W = 64
HEAD_DIM = 128
WINDOW = 128
N_HEADS = 8
A_KV_HEADS = 2
A_GROUP = N_HEADS // A_KV_HEADS
B_QK_DIM = 64
C_Q_LORA = 1536
C_KV_LORA = 512
C_NOPE = 128
C_ROPE = 64
N_MOD = 9
ROPE_BASE = 10000.0
LN_EPS = 1e-5
RMS_EPS = 1e-6
NEG_INF = -1e30

H_CQ, H_AQ, H_BQ, H_DX = 0, 1536, 2560, 3584
H_AK, H_AV, H_BK, H_BV, H_CKV, H_CKR = 4608, 4864, 5120, 6144, 7168, 7680
H_COLS = 8192

ROPE_NONE, ROPE_A, ROPE_B, ROPE_KR, ROPE_QR = 0, 1, 2, 3, 4
ROPE_HALF_LANES = {ROPE_A: HEAD_DIM // 4, ROPE_B: B_QK_DIM // 4, ROPE_KR: C_ROPE // 4, ROPE_QR: C_ROPE // 4}

VMEM_LIMIT_BYTES = 56 * 1024 * 1024
ROPE_TN = 256
PROJ_HALVES = 2
Q_SUB = 256
Q_SUB_DIFF = 512
LOG2E = math.log2(math.e)


def _cparams(n_axes):
    return pltpu.CompilerParams(dimension_semantics=("arbitrary",) * n_axes,
                                vmem_limit_bytes=VMEM_LIMIT_BYTES)


def _dot(a, b):
    return jnp.dot(a, b, preferred_element_type=F32)


def _dot_nt(a, b):
    return lax.dot_general(a, b, (((1,), (1,)), ((), ())), preferred_element_type=F32)


def _mm_kernel(a_ref, b_ref, o_ref, acc_ref, *, nk):
    k, j = pl.program_id(1), pl.program_id(2)

    def part():
        return _dot(a_ref[...], b_ref[...].astype(BF16))

    @pl.when(k == 0)
    def _():
        acc_ref[j] = part()

    if nk > 2:
        @pl.when((k > 0) & (k < nk - 1))
        def _():
            acc_ref[j] += part()

    @pl.when(k == nk - 1)
    def _():
        o_ref[...] = (acc_ref[j] + part()).astype(o_ref.dtype)


def _mm(a, b, layer, *, rows, tm, tn, tk, name):
    _, kdim, n = b.shape
    assert rows % tm == 0 and n % tn == 0 and kdim % tk == 0 and kdim // tk >= 2
    nk = kdim // tk
    return pl.pallas_call(
        functools.partial(_mm_kernel, nk=nk),
        out_shape=jax.ShapeDtypeStruct((rows, n), BF16),
        grid=(rows // tm, nk, n // tn),
        in_specs=[pl.BlockSpec((tm, tk), lambda i, k, j: (i, k)),
                  pl.BlockSpec((None, tk, tn), lambda i, k, j: (layer, k, j))],
        out_specs=pl.BlockSpec((tm, tn), lambda i, k, j: (i, jnp.where(k == nk - 1, j, 0))),
        scratch_shapes=[pltpu.VMEM((n // tn, tm, tn), F32)],
        compiler_params=_cparams(3),
        name=name,
    )(a, b)


def _out_proj_kernel(*refs):
    *y_refs, w_ref, o_ref = refs
    width = y_refs[0].shape[1]
    acc = None
    for m, y_ref in enumerate(y_refs):
        part = _dot(y_ref[...], w_ref[m * width:(m + 1) * width, :].astype(BF16))
        acc = part if acc is None else acc + part
    o_ref[...] = acc.astype(o_ref.dtype)


def _out_proj(ys, w_out, layer, *, rows, tm, tn):
    _, kdim, n = w_out.shape
    width = ys[0].shape[1]
    assert kdim == width * len(ys)
    y_spec = pl.BlockSpec((tm, width), lambda i, j: (i, 0))
    return pl.pallas_call(
        _out_proj_kernel,
        out_shape=jax.ShapeDtypeStruct((rows, n), BF16),
        grid=(rows // tm, n // tn),
        in_specs=[y_spec] * len(ys) + [pl.BlockSpec((None, kdim, tn), lambda i, j: (layer, 0, j))],
        out_specs=pl.BlockSpec((tm, tn), lambda i, j: (i, j)),
        compiler_params=_cparams(2),
        name="out_proj",
    )(*ys, w_out)


def _mod_kernel(c_ref, w_ref, b_ref, o_ref):
    c = c_ref[...]
    sc = (c * jax.nn.sigmoid(c)).astype(BF16)
    o_ref[...] = _dot(sc, w_ref[...].astype(BF16)) + b_ref[...]


def _mod_vectors(c_rows, w_mod, b_mod):
    depth, d, n = w_mod.shape
    r = c_rows.shape[0]
    tn = 1024
    return pl.pallas_call(
        _mod_kernel,
        out_shape=jax.ShapeDtypeStruct((depth, r, n), F32),
        grid=(depth, n // tn),
        in_specs=[pl.BlockSpec((r, d), lambda l, j: (0, 0)),
                  pl.BlockSpec((None, d, tn), lambda l, j: (l, 0, j)),
                  pl.BlockSpec((None, 1, tn), lambda l, j: (l, 0, j))],
        out_specs=pl.BlockSpec((None, r, tn), lambda l, j: (l, 0, j)),
        compiler_params=_cparams(2),
        name="mod_vectors",
    )(c_rows, w_mod, b_mod.reshape(depth, 1, n))


def _seg_index(i, n_lat_tiles, tiles_per_batch, n_batch):
    return jnp.where(i < n_lat_tiles, i // tiles_per_batch, n_batch)


def _modulate_kernel(x_ref, c_ref, mod_ref, xs_ref, u_ref, *, shift_idx, n_lat_tiles):
    shift = mod_ref[shift_idx:shift_idx + 1, :]
    scale = mod_ref[shift_idx + 1:shift_idx + 2, :]

    def emit(src_ref):
        x = src_ref[...]
        xs_ref[...] = x
        u_ref[...] = (x * (1.0 + scale) + shift).astype(u_ref.dtype)

    @pl.when(pl.program_id(0) < n_lat_tiles)
    def _():
        emit(x_ref)

    @pl.when(pl.program_id(0) >= n_lat_tiles)
    def _():
        emit(c_ref)


def _res_ln(x, y, mod_ref, g_ref, b_ref, *, alpha, gate_idx, gate_mul):
    gate = mod_ref[gate_idx:gate_idx + 1, :]
    z = alpha * x + (gate_mul * gate) * y.astype(F32)
    mu = jnp.mean(z, axis=-1, keepdims=True)
    zc = z - mu
    var = jnp.mean(zc * zc, axis=-1, keepdims=True)
    return zc * lax.rsqrt(var + LN_EPS) * g_ref[...] + b_ref[...]


def _modulated(xn, nmod_ref, shift_idx):
    shift = nmod_ref[shift_idx:shift_idx + 1, :]
    scale = nmod_ref[shift_idx + 1:shift_idx + 2, :]
    return (xn * (1.0 + scale) + shift).astype(BF16)


def _ln_res_kernel(x_ref, y_ref, mod_ref, g_ref, b_ref, *rest, alpha, gate_idx, gate_mul, next_shift_idx):
    xn = _res_ln(x_ref[...], y_ref[...], mod_ref, g_ref, b_ref, alpha=alpha, gate_idx=gate_idx, gate_mul=gate_mul)
    if next_shift_idx is None:
        xo_ref, = rest
        xo_ref[...] = xn
    else:
        nmod_ref, xo_ref, u_ref = rest
        xo_ref[...] = xn
        u_ref[...] = _modulated(xn, nmod_ref, next_shift_idx)


class _Rows:
    def __init__(self, n_batch, seq, ctx_len):
        self.b, self.t, self.c = n_batch, seq, ctx_len
        self.nl, self.nc = n_batch * seq, n_batch * ctx_len
        self.n = self.nl + self.nc
        self.tr = math.gcd(256, math.gcd(seq, ctx_len))
        self.tm = math.gcd(1024, math.gcd(seq, self.nc))

    def mod_spec(self, tile, d):
        n_lat_tiles, per_batch, nb = self.nl // tile, self.t // tile, self.b
        return pl.BlockSpec((None, N_MOD, d),
                            lambda i, *_: (_seg_index(i, n_lat_tiles, per_batch, nb), 0, 0))


def _modulate(x, ctx, mod, rg, *, shift_idx):
    d = x.shape[1]
    tr = rg.tr
    n_lat_tiles, n_ctx_tiles = rg.nl // tr, rg.nc // tr
    row_spec = pl.BlockSpec((tr, d), lambda i: (i, 0))
    x_spec = pl.BlockSpec((tr, d), lambda i: (jnp.minimum(i, n_lat_tiles - 1), 0))
    c_spec = pl.BlockSpec((tr, d), lambda i: (jnp.clip(i - n_lat_tiles, 0, n_ctx_tiles - 1), 0))
    return pl.pallas_call(
        functools.partial(_modulate_kernel, shift_idx=shift_idx, n_lat_tiles=n_lat_tiles),
        out_shape=[jax.ShapeDtypeStruct((rg.n, d), F32), jax.ShapeDtypeStruct((rg.n, d), BF16)],
        grid=(rg.n // tr,),
        in_specs=[x_spec, c_spec, rg.mod_spec(tr, d)],
        out_specs=[row_spec, row_spec],
        compiler_params=_cparams(1),
        name="modulate",
    )(x, ctx, mod)


def _ln_res(x, y, mod, ln_g, ln_b, rg, *, rows, alpha, gate_idx, gate_mul, next_mod, next_shift_idx, name,
            out_rows=None):
    d = x.shape[1]
    tr = rg.tr
    row_spec = pl.BlockSpec((tr, d), lambda i: (i, 0))
    vec_spec = pl.BlockSpec((1, d), lambda i: (0, 0))
    out_shape = [jax.ShapeDtypeStruct((out_rows or rows, d), F32)]
    out_specs = [row_spec]
    in_specs = [row_spec, row_spec, rg.mod_spec(tr, d), vec_spec, vec_spec]
    args = [x, y, mod, ln_g.reshape(1, d), ln_b.reshape(1, d)]
    if next_shift_idx is not None:
        in_specs.append(rg.mod_spec(tr, d))
        args.append(next_mod)
        out_shape.append(jax.ShapeDtypeStruct((rows, d), BF16))
        out_specs.append(row_spec)
    res = pl.pallas_call(
        functools.partial(_ln_res_kernel, alpha=alpha, gate_idx=gate_idx, gate_mul=gate_mul,
                          next_shift_idx=next_shift_idx),
        out_shape=out_shape,
        grid=(rows // tr,),
        in_specs=in_specs,
        out_specs=out_specs,
        compiler_params=_cparams(1),
        name=name,
    )(*args)
    return (res[0], res[1]) if next_shift_idx is not None else (res[0], None)


def _ffn_up_kernel(a_ref, wg_ref, wu_ref, o_ref):
    a = a_ref[...]
    g = _dot(a, wg_ref[...].astype(BF16))
    up = _dot(a, wu_ref[...].astype(BF16))
    o_ref[...] = (g * jax.nn.sigmoid(g) * up).astype(o_ref.dtype)


def _ffn_up(u, w_gu, layer, *, rows, tm, tn):
    _, d, two_ff = w_gu.shape
    ff = two_ff // 2
    up0 = ff // tn
    return pl.pallas_call(
        _ffn_up_kernel,
        out_shape=jax.ShapeDtypeStruct((rows, ff), BF16),
        grid=(rows // tm, ff // tn),
        in_specs=[pl.BlockSpec((tm, d), lambda i, j: (i, 0)),
                  pl.BlockSpec((None, d, tn), lambda i, j: (layer, 0, j)),
                  pl.BlockSpec((None, d, tn), lambda i, j: (layer, 0, up0 + j))],
        out_specs=pl.BlockSpec((tm, tn), lambda i, j: (i, j)),
        compiler_params=_cparams(2),
        name="ffn_up",
    )(u, w_gu, w_gu)


def _ffn_up_ln_kernel(x_ref, y_ref, mod_ref, nmod_ref, g_ref, b_ref, u0_ref, _xs_head, wg_ref, wu_ref,
                      hid_ref, xo_ref, a_even, a_odd, *, alpha, gate_idx, gate_mul, shift_idx, chunk):
    i, j = pl.program_id(0), pl.program_id(1)

    @pl.when((i == 0) & (j == 0))
    def _():
        a_even[...] = u0_ref[...]

    def step(a_cur, a_next):
        xn = _res_ln(x_ref[...], y_ref[...], mod_ref, g_ref, b_ref,
                     alpha=alpha, gate_idx=gate_idx, gate_mul=gate_mul)
        xo_ref[...] = xn
        a_next[pl.ds(pl.multiple_of(j * chunk, chunk), chunk), :] = _modulated(xn, nmod_ref, shift_idx)
        a = a_cur[...]
        g = _dot(a, wg_ref[...].astype(BF16))
        up = _dot(a, wu_ref[...].astype(BF16))
        hid_ref[...] = (g * jax.nn.sigmoid(g) * up).astype(hid_ref.dtype)

    @pl.when(i % 2 == 0)
    def _():
        step(a_even, a_odd)

    @pl.when(i % 2 == 1)
    def _():
        step(a_odd, a_even)


def _ffn_up_ln(ln, w_gu, layer, rg, *, rows, tm, tn):
    _, d, two_ff = w_gu.shape
    ff = two_ff // 2
    n_i, n_j = rows // tm, ff // tn
    chunk = tm // n_j
    assert rows % tm == 0 and tm % n_j == 0 and chunk % 16 == 0 and rg.t % chunk == 0 and rg.nl % tm == 0
    xs_head, u0 = _ln_res(ln["x"], ln["y"], ln["mod"], ln["g"], ln["b"], rg, rows=tm, out_rows=rows,
                          alpha=ln["alpha"], gate_idx=ln["gate_idx"], gate_mul=ln["gate_mul"],
                          next_mod=ln["next_mod"], next_shift_idx=ln["shift_idx"], name="ln_head")

    def chunk_block(i, j):
        return jnp.where(i < n_i - 1, (i + 1) * n_j + j, n_i * n_j - 1)

    def seg(i, j):
        row0 = chunk_block(i, j) * chunk
        return jnp.where(row0 < rg.nl, row0 // rg.t, rg.b)

    chunk_spec = pl.BlockSpec((chunk, d), lambda i, j: (chunk_block(i, j), 0))
    mod_spec = pl.BlockSpec((None, N_MOD, d), lambda i, j: (seg(i, j), 0, 0))
    vec_spec = pl.BlockSpec((1, d), lambda i, j: (0, 0))
    up0 = ff // tn
    hid, xs_new = pl.pallas_call(
        functools.partial(_ffn_up_ln_kernel, alpha=ln["alpha"], gate_idx=ln["gate_idx"], gate_mul=ln["gate_mul"],
                          shift_idx=ln["shift_idx"], chunk=chunk),
        out_shape=[jax.ShapeDtypeStruct((rows, ff), BF16), jax.ShapeDtypeStruct((rows, d), F32)],
        grid=(n_i, n_j),
        in_specs=[chunk_spec, chunk_spec, mod_spec, mod_spec, vec_spec, vec_spec,
                  pl.BlockSpec((tm, d), lambda i, j: (0, 0), pipeline_mode=pl.Buffered(1)),
                  pl.BlockSpec(memory_space=pl.ANY),
                  pl.BlockSpec((None, d, tn), lambda i, j: (layer, 0, j)),
                  pl.BlockSpec((None, d, tn), lambda i, j: (layer, 0, up0 + j))],
        out_specs=[pl.BlockSpec((tm, tn), lambda i, j: (i, j)), chunk_spec],
        scratch_shapes=[pltpu.VMEM((tm, d), BF16), pltpu.VMEM((tm, d), BF16)],
        input_output_aliases={7: 1},
        compiler_params=_cparams(2),
        name="ffn_up_ln",
    )(ln["x"], ln["y"], ln["mod"], ln["next_mod"], ln["g"].reshape(1, d), ln["b"].reshape(1, d), u0, xs_head,
      w_gu, w_gu)
    return hid, xs_new


def _rope_tile(x, cos, sin, half):
    lane = lax.broadcasted_iota(jnp.int32, x.shape, 1)
    first = (lane % (2 * half)) < half
    width = x.shape[1]
    partner = jnp.where(first, pltpu.roll(x, width - half, 1), pltpu.roll(x, half, 1))
    return x * cos + partner * sin


def _proj_kernel(kind_ref, src_ref, scale_ref, a_ref, b_ref, *rest, has_gain, n_lat_tiles, rope_kinds):
    if has_gain:
        gain_ref, *tab_refs, o_ref, an_ref = rest
    else:
        *tab_refs, o_ref = rest
    tables = {kind: (tab_refs[2 * n], tab_refs[2 * n + 1]) for n, kind in enumerate(rope_kinds)}
    i, j = pl.program_id(0), pl.program_id(1)

    if has_gain:
        @pl.when(j == 0)
        def _():
            x = a_ref[...].astype(F32)
            ms = jnp.mean(x * x, axis=-1, keepdims=True)
            an_ref[...] = (x * lax.rsqrt(ms + RMS_EPS) * gain_ref[...]).astype(BF16)
    lhs_ref = an_ref if has_gain else a_ref

    for half in range(PROJ_HALVES):
        cols = slice(half * ROPE_TN, (half + 1) * ROPE_TN)
        tile = PROJ_HALVES * j + half
        kind = jnp.where(i < n_lat_tiles, kind_ref[tile], ROPE_NONE)
        out_scale = scale_ref[tile]

        def acc():
            return _dot(lhs_ref[...], b_ref[:, cols].astype(BF16)) * out_scale

        @pl.when(kind == ROPE_NONE)
        def _():
            o_ref[:, cols] = acc().astype(o_ref.dtype)

        for rope_kind, (cos_ref, sin_ref) in tables.items():
            @pl.when(kind == rope_kind)
            def _(cos_ref=cos_ref, sin_ref=sin_ref, rope_kind=rope_kind):
                roped = _rope_tile(acc(), cos_ref[...], sin_ref[...], ROPE_HALF_LANES[rope_kind])
                o_ref[:, cols] = roped.astype(o_ref.dtype)


def _proj(a, w, layer, kinds, scales, src_tiles, cos_tab, sin_tab, rg, *, rows, rope_kinds, a_col0=0, gain=None,
          name):
    _, kdim, n = w.shape
    tm, tn = rg.tm, PROJ_HALVES * ROPE_TN
    assert a_col0 % kdim == 0 and n % tn == 0
    a_blk = a_col0 // kdim
    n_lat_tiles = rg.nl // tm
    t_tiles = rg.t // tm
    has_gain = gain is not None

    def tab_spec(kind):
        return pl.BlockSpec((None, tm, ROPE_TN), lambda i, j, kr, sr: (kind, i % t_tiles, 0))

    in_specs = [pl.BlockSpec(memory_space=pltpu.SMEM),
                pl.BlockSpec((tm, kdim), lambda i, j, kr, sr: (i, a_blk)),
                pl.BlockSpec((None, kdim, tn), lambda i, j, kr, sr: (layer, 0, sr[j]))]
    args = [scales, a, w]
    scratch = []
    if has_gain:
        in_specs.append(pl.BlockSpec((None, 1, kdim), lambda i, j, kr, sr: (layer, 0, 0)))
        args.append(gain.reshape(gain.shape[0], 1, kdim))
        scratch.append(pltpu.VMEM((tm, kdim), BF16))
    for kind in rope_kinds:
        in_specs += [tab_spec(kind), tab_spec(kind)]
        args += [cos_tab, sin_tab]
    return pl.pallas_call(
        functools.partial(_proj_kernel, has_gain=has_gain, n_lat_tiles=n_lat_tiles, rope_kinds=tuple(rope_kinds)),
        out_shape=jax.ShapeDtypeStruct((rows, n), BF16),
        grid_spec=pltpu.PrefetchScalarGridSpec(
            num_scalar_prefetch=2,
            grid=(rows // tm, n // tn),
            in_specs=in_specs,
            out_specs=pl.BlockSpec((tm, tn), lambda i, j, kr, sr: (i, j)),
            scratch_shapes=scratch),
        compiler_params=_cparams(2),
        name=name,
    )(kinds, src_tiles, *args)


def _rope_tables(seq):
    t = np.arange(seq)
    row, col = (t // GRID_W).astype(np.float64), (t % GRID_W).astype(np.float64)

    def pattern(rot_dim):
        axis_dim = rot_dim // 2
        inv = ROPE_BASE ** (-np.arange(0, axis_dim, 2, dtype=np.float64) / axis_dim)
        ar, ac = row[:, None] * inv[None, :], col[:, None] * inv[None, :]
        cos = np.concatenate([np.cos(ar), np.cos(ar), np.cos(ac), np.cos(ac)], axis=1)
        sin = np.concatenate([-np.sin(ar), np.sin(ar), -np.sin(ac), np.sin(ac)], axis=1)
        return cos, sin

    cos = np.ones((5, seq, ROPE_TN), np.float64)
    sin = np.zeros((5, seq, ROPE_TN), np.float64)
    c128, s128 = pattern(HEAD_DIM)
    c64, s64 = pattern(2 * 32)
    cos[ROPE_A], sin[ROPE_A] = np.tile(c128, (1, 2)), np.tile(s128, (1, 2))
    cos[ROPE_B], sin[ROPE_B] = np.tile(c64, (1, 4)), np.tile(s64, (1, 4))
    cos[ROPE_KR, :, :64], sin[ROPE_KR, :, :64] = c64, s64
    cos[ROPE_QR, :, 128:192], sin[ROPE_QR, :, 128:192] = c64, s64
    return jnp.asarray(cos, F32), jnp.asarray(sin, F32)


def _mixer_call(body, rg, *, latent, dst, out_rows, grid, in_specs, args, tq, n_q, out_block_w, name):
    row0 = 0 if latent else rg.nl // tq
    aliases = {}
    if not latent:
        in_specs = in_specs + [pl.BlockSpec(memory_space=pl.ANY)]
        args = args + [dst]
        aliases = {len(args) - 1: 0}
    return pl.pallas_call(
        body,
        out_shape=jax.ShapeDtypeStruct((out_rows, N_HEADS * HEAD_DIM), BF16),
        grid=grid,
        in_specs=in_specs,
        out_specs=pl.BlockSpec((tq, out_block_w), lambda b, hh, qi: (row0 + b * n_q + qi, hh)),
        input_output_aliases=aliases,
        compiler_params=_cparams(3),
        name=name,
    )(*args)


def _attn_win_kernel(sink_ref, q_ref, kc_ref, vc_ref, *rest, has_lat, tq, seq, layer):
    if has_lat:
        kl_ref, vl_ref, o_ref = rest
    else:
        _, o_ref = rest
    g, qi = pl.program_id(1), pl.program_id(2)
    kc, vc = kc_ref[...], vc_ref[...]
    sub = min(tq, Q_SUB)
    for r in range(tq // sub):
        rows = slice(r * sub, (r + 1) * sub)
        row0 = qi * tq + r * sub
        if has_lat:
            win = min(seq, sub + 2 * WINDOW)
            ws = pl.multiple_of(jnp.clip(row0 - WINDOW, 0, seq - win), WINDOW)
            kw, vw = kl_ref[pl.ds(ws, win), :], vl_ref[pl.ds(ws, win), :]
            qpos = row0 + lax.broadcasted_iota(jnp.int32, (sub, win), 0)
            kpos = ws + lax.broadcasted_iota(jnp.int32, (sub, win), 1)
            valid = jnp.abs(qpos - kpos) <= WINDOW
        for j in range(A_GROUP):
            cols = slice(j * HEAD_DIM, (j + 1) * HEAD_DIM)
            q = q_ref[rows, cols]
            sink = sink_ref[layer, g * A_GROUP + j] * LOG2E
            s_c = _dot_nt(q, kc)
            m = jnp.maximum(jnp.max(s_c, axis=-1, keepdims=True), sink)
            if has_lat:
                s_l = jnp.where(valid, _dot_nt(q, kw), NEG_INF)
                m = jnp.maximum(m, jnp.max(s_l, axis=-1, keepdims=True))
            p_c = jnp.exp2(s_c - m)
            den = jnp.sum(p_c, axis=-1, keepdims=True) + jnp.exp2(sink - m)
            o = _dot(p_c.astype(BF16), vc)
            if has_lat:
                p_l = jnp.exp2(s_l - m)
                den = den + jnp.sum(p_l, axis=-1, keepdims=True)
                o = o + _dot(p_l.astype(BF16), vw)
            o_ref[rows, cols] = (o / den).astype(o_ref.dtype)


def _attn_win(h, a_sink, layer, rg, *, latent, out_rows, dst=None):
    gw = A_GROUP * HEAD_DIM
    tq, n_q, q0 = _q_geometry(rg, latent, 1024)
    c_blk0 = rg.nl // rg.c
    in_specs = [pl.BlockSpec(memory_space=pltpu.SMEM),
                pl.BlockSpec((tq, gw), lambda b, g, qi: (q0 + b * n_q + qi, H_AQ // gw + g)),
                pl.BlockSpec((rg.c, HEAD_DIM), lambda b, g, qi: (c_blk0 + b, H_AK // HEAD_DIM + g)),
                pl.BlockSpec((rg.c, HEAD_DIM), lambda b, g, qi: (c_blk0 + b, H_AV // HEAD_DIM + g))]
    args = [a_sink, h, h, h]
    if latent:
        in_specs += [pl.BlockSpec((rg.t, HEAD_DIM), lambda b, g, qi: (b, H_AK // HEAD_DIM + g)),
                     pl.BlockSpec((rg.t, HEAD_DIM), lambda b, g, qi: (b, H_AV // HEAD_DIM + g))]
        args += [h, h]
    return _mixer_call(functools.partial(_attn_win_kernel, has_lat=latent, tq=tq, seq=rg.t, layer=layer), rg,
                       latent=latent, dst=dst, out_rows=out_rows, grid=(rg.b, A_KV_HEADS, n_q), in_specs=in_specs, args=args,
                       tq=tq, n_q=n_q, out_block_w=gw, name="attn_win_lat" if latent else "attn_win_ctx")


def _softmax_pv(q, keys, values):
    scores = [_dot_nt(q, k) for k in keys]
    m = functools.reduce(jnp.maximum, [jnp.max(s, axis=-1, keepdims=True) for s in scores])
    den, out = None, None
    for s, v in zip(scores, values):
        p = jnp.exp2(s - m)
        d = jnp.sum(p, axis=-1, keepdims=True)
        o = _dot(p.astype(BF16), v)
        den = d if den is None else den + d
        out = o if out is None else out + o
    return out / den


def _attn_diff_kernel(lam_ref, gsub_ref, q_ref, kc_ref, vc_ref, *rest, has_lat, lam_init):
    if has_lat:
        kl_ref, vl_ref, o_ref = rest
    else:
        _, o_ref = rest
    lf = lam_ref[...]
    lam = (jnp.exp(jnp.sum(lf[0:1] * lf[1:2], axis=-1, keepdims=True))
           - jnp.exp(jnp.sum(lf[2:3] * lf[3:4], axis=-1, keepdims=True)) + lam_init)
    keys = [kc_ref[...]] + ([kl_ref[...]] if has_lat else [])
    values = [vc_ref[...]] + ([vl_ref[...]] if has_lat else [])
    tq = q_ref.shape[0]
    sub = min(tq, Q_SUB_DIFF)
    for r in range(tq // sub):
        q = q_ref[r * sub:(r + 1) * sub, :]
        first_map = lax.broadcasted_iota(jnp.int32, q.shape, 1) < B_QK_DIM
        zero = jnp.zeros_like(q)

        o = (_softmax_pv(jnp.where(first_map, q, zero), keys, values)
             - lam * _softmax_pv(jnp.where(first_map, zero, q), keys, values))
        ms = jnp.mean(o * o, axis=-1, keepdims=True)
        o = o * lax.rsqrt(ms + RMS_EPS) * gsub_ref[...] * (1.0 - lam_init)
        o_ref[r * sub:(r + 1) * sub, :] = o.astype(o_ref.dtype)


def _kv_specs(rg, col_k, col_v, latent, k_width=HEAD_DIM):
    c_blk0 = rg.nl // rg.c
    specs = [pl.BlockSpec((rg.c, k_width), lambda b, hh, qi: (c_blk0 + b, col_k(hh))),
             pl.BlockSpec((rg.c, HEAD_DIM), lambda b, hh, qi: (c_blk0 + b, col_v(hh)))]
    if latent:
        specs += [pl.BlockSpec((rg.t, k_width), lambda b, hh, qi: (b, col_k(hh))),
                  pl.BlockSpec((rg.t, HEAD_DIM), lambda b, hh, qi: (b, col_v(hh)))]
    return specs


def _q_geometry(rg, latent, tq_lat):
    if latent:
        tq = min(tq_lat, rg.t)
        return tq, rg.t // tq, 0
    return rg.c, 1, rg.nl // rg.c


def _attn_diff(h, b_lambda, b_subln_g, layer, rg, *, latent, lam_init, out_rows, dst=None):
    tq, n_q, q0 = _q_geometry(rg, latent, 2048)
    in_specs = [pl.BlockSpec((None, 4, B_QK_DIM), lambda b, hh, qi: (layer, 0, 0)),
                pl.BlockSpec((None, 1, HEAD_DIM), lambda b, hh, qi: (layer, 0, 0)),
                pl.BlockSpec((tq, HEAD_DIM), lambda b, hh, qi: (q0 + b * n_q + qi, H_BQ // HEAD_DIM + hh))]
    in_specs += _kv_specs(rg, lambda hh: H_BK // HEAD_DIM + hh, lambda hh: H_BV // HEAD_DIM + hh, latent)
    n_kv = 4 if latent else 2
    args = [b_lambda, b_subln_g.reshape(b_subln_g.shape[0], 1, HEAD_DIM), h] + [h] * n_kv
    return _mixer_call(functools.partial(_attn_diff_kernel, has_lat=latent, lam_init=lam_init), rg,
                       latent=latent, dst=dst, out_rows=out_rows, grid=(rg.b, N_HEADS, n_q), in_specs=in_specs, args=args,
                       tq=tq, n_q=n_q, out_block_w=HEAD_DIM, name="attn_diff_lat" if latent else "attn_diff_ctx")


def _attn_mla_kernel(q_ref, knc_ref, vc_ref, krc_ref, *rest, has_lat):
    if has_lat:
        knl_ref, vl_ref, krl_ref, o_ref = rest
    else:
        _, o_ref = rest
    segs = [(knc_ref, krc_ref, vc_ref)] + ([(knl_ref, krl_ref, vl_ref)] if has_lat else [])
    keys = [jnp.concatenate([kn[...], kr[...]], axis=1) for kn, kr, _ in segs]
    values = [v[...] for _, _, v in segs]
    tq = q_ref.shape[0]
    sub = min(tq, Q_SUB)
    for r in range(tq // sub):
        rows = slice(r * sub, (r + 1) * sub)
        o_ref[rows, :] = _softmax_pv(q_ref[rows, :], keys, values).astype(o_ref.dtype)


def _attn_mla(qc, kv2, h, rg, *, latent, out_rows, dst=None):
    tq, n_q, q0 = _q_geometry(rg, latent, 2048)
    c_blk0 = rg.nl // rg.c
    kr_col = H_CKR // HEAD_DIM
    kv = _kv_specs(rg, lambda hh: 2 * hh, lambda hh: 2 * hh + 1, latent)
    in_specs = [pl.BlockSpec((tq, 2 * HEAD_DIM), lambda b, hh, qi: (q0 + b * n_q + qi, hh)),
                kv[0], kv[1],
                pl.BlockSpec((rg.c, HEAD_DIM), lambda b, hh, qi: (c_blk0 + b, kr_col))]
    args = [qc, kv2, kv2, h]
    if latent:
        in_specs += [kv[2], kv[3], pl.BlockSpec((rg.t, HEAD_DIM), lambda b, hh, qi: (b, kr_col))]
        args += [kv2, kv2, h]
    return _mixer_call(functools.partial(_attn_mla_kernel, has_lat=latent), rg,
                       latent=latent, dst=dst, out_rows=out_rows, grid=(rg.b, N_HEADS, n_q), in_specs=in_specs, args=args,
                       tq=tq, n_q=n_q, out_block_w=HEAD_DIM, name="attn_mla_lat" if latent else "attn_mla_ctx")


def _dft_tables(n):
    idx = np.arange(n, dtype=np.int64)
    ang = (np.outer(idx, idx) % n).astype(np.float64) * (2.0 * np.pi / n)
    return jnp.asarray(np.cos(ang), F32), jnp.asarray(np.sin(ang), F32)


def _fourier_chan_kernel(z_ref, cc_ref, sc_ref, dw_ref, zc_ref, zs_ref):
    for g in range(dw_ref.shape[0]):
        cols = slice(g * HEAD_DIM, (g + 1) * HEAD_DIM)
        dw = dw_ref[g].astype(BF16)
        z = z_ref[:, cols]
        zc_ref[:, cols] = _dot(z, _dot(cc_ref[...], dw).astype(BF16)).astype(zc_ref.dtype)
        zs_ref[:, cols] = _dot(z, _dot(sc_ref[...], dw).astype(BF16)).astype(zs_ref.dtype)


def _fourier_chan(h, cos_c, sin_c, d_w, layer, rg, *, rows):
    tm = rg.tm
    groups = N_HEADS // 2
    width = groups * HEAD_DIM
    assert H_DX % width == 0
    tile = pl.BlockSpec((tm, width), lambda i, s: (i, s))
    const = pl.BlockSpec((HEAD_DIM, HEAD_DIM), lambda i, s: (0, 0))
    out = jax.ShapeDtypeStruct((rows, N_HEADS * HEAD_DIM), BF16)
    return pl.pallas_call(
        _fourier_chan_kernel,
        out_shape=[out, out],
        grid=(rows // tm, N_HEADS // groups),
        in_specs=[pl.BlockSpec((tm, width), lambda i, s: (i, H_DX // width + s)), const, const,
                  pl.BlockSpec((None, groups, HEAD_DIM, HEAD_DIM), lambda i, s: (layer, s, 0, 0))],
        out_specs=[tile, tile],
        compiler_params=_cparams(2),
        name="fourier_chan",
    )(h, cos_c, sin_c, d_w)


def _fourier_seq_kernel(ct_ref, st_ref, zc_ref, zs_ref, *rest, norm):
    o_ref = rest[-1]
    o = _dot(ct_ref[...], zc_ref[...]) - _dot(st_ref[...], zs_ref[...])
    o_ref[...] = (o * norm).astype(o_ref.dtype)


def _fourier_seq(cos_t, sin_t, zc, zs, rg, *, latent, out_rows, dst=None):
    length = rg.t if latent else rg.c
    tm = min(512, length)
    n_i = length // tm
    z_blk0 = 0 if latent else rg.nl // rg.c
    row0 = 0 if latent else rg.nl // tm
    width = zc.shape[1]
    dft_spec = pl.BlockSpec((tm, length), lambda b, i: (i, 0))
    z_spec = pl.BlockSpec((length, width), lambda b, i: (z_blk0 + b, 0))
    in_specs, args, aliases = [dft_spec, dft_spec, z_spec, z_spec], [cos_t, sin_t, zc, zs], {}
    if not latent:
        in_specs.append(pl.BlockSpec(memory_space=pl.ANY))
        args.append(dst)
        aliases = {len(args) - 1: 0}
    return pl.pallas_call(
        functools.partial(_fourier_seq_kernel, norm=(length * HEAD_DIM) ** -0.5),
        out_shape=jax.ShapeDtypeStruct((out_rows, width), BF16),
        grid=(rg.b, n_i),
        in_specs=in_specs,
        out_specs=pl.BlockSpec((tm, width), lambda b, i: (row0 + b * n_i + i, 0)),
        input_output_aliases=aliases,
        compiler_params=_cparams(2),
        name="fourier_seq_lat" if latent else "fourier_seq_ctx",
    )(*args)


def _prep_w_in(w_in):
    return jnp.pad(w_in, ((0, 0), (0, 0), (0, H_COLS - w_in.shape[2]))).astype(BF16)


def _prep_w_uq(w_uq):
    depth, k, _ = w_uq.shape
    w = w_uq.reshape(depth, k, N_HEADS, C_NOPE + C_ROPE)
    w = jnp.pad(w, ((0, 0), (0, 0), (0, 0), (0, 2 * HEAD_DIM - C_NOPE - C_ROPE)))
    return w.reshape(depth, k, N_HEADS * 2 * HEAD_DIM).astype(BF16)


def _in_proj_tiles():
    step = PROJ_HALVES * ROPE_TN
    src_cols = {"aq": 0, "bq": 1024, "cq": 2048, "dx": 3584, "ak": 4608, "bk": 5120, "bv": 6144,
                "ckv": 7168, "ckr": 7680}
    layout = [("cq", H_CQ, H_AQ, (ROPE_NONE, ROPE_NONE), 1.0),
              ("aq", H_AQ, H_BQ, (ROPE_A, ROPE_A), HEAD_DIM ** -0.5 * LOG2E),
              ("bq", H_BQ, H_DX, (ROPE_B, ROPE_B), B_QK_DIM ** -0.5 * LOG2E),
              ("dx", H_DX, H_AK, (ROPE_NONE, ROPE_NONE), 1.0),
              ("ak", H_AK, H_BK, (ROPE_A, ROPE_NONE), 1.0),
              ("bk", H_BK, H_BV, (ROPE_B, ROPE_B), 1.0),
              ("bv", H_BV, H_CKV, (ROPE_NONE, ROPE_NONE), 1.0),
              ("ckv", H_CKV, H_CKR, (ROPE_NONE, ROPE_NONE), 1.0),
              ("ckr", H_CKR, H_COLS, (ROPE_KR, ROPE_NONE), 1.0)]
    kinds = np.zeros((H_COLS // ROPE_TN,), np.int32)
    scales = np.ones((H_COLS // ROPE_TN,), np.float32)
    src = np.zeros((H_COLS // step,), np.int32)
    for name, lo, hi, kind, scale in layout:
        assert lo % step == 0 and hi % step == 0 and src_cols[name] % step == 0
        n = (hi - lo) // step
        kinds[lo // ROPE_TN:hi // ROPE_TN] = np.tile(np.asarray(kind, np.int32), n)
        scales[lo // ROPE_TN:hi // ROPE_TN] = scale
        src[lo // step:hi // step] = src_cols[name] // step + np.arange(n)
    return jnp.asarray(kinds), jnp.asarray(scales), jnp.asarray(src)


def _ffn(u, w_gu, w_d, layer, *, rows, tm):
    tm_up = next(t for t in (3 * tm // 2, tm) if rows % t == 0)
    hid = _ffn_up(u, w_gu, layer, rows=rows, tm=tm_up, tn=256)
    return _mm(hid, w_d, layer, rows=rows, tm=tm, tn=512, tk=w_d.shape[1] // 2, name="ffn_down")


def _ffn_after_ln(ln, w_gu, w_d, layer, rg, *, rows):
    hid, xs_new = _ffn_up_ln(ln, w_gu, layer, rg, rows=rows, tm=rg.tm, tn=256)
    y = _mm(hid, w_d, layer, rows=rows, tm=rg.tm, tn=512, tk=w_d.shape[1] // 2, name="ffn_down")
    return y, xs_new


def _mixing(u, p, layer, consts, rg, *, lam_init, need_ctx):
    cos_tab, sin_tab, (kinds_in, scales_in, src_in), dft = consts
    rows = rg.n if need_ctx else rg.nl
    h = _proj(u, p["w_in"], layer, kinds_in, scales_in, src_in, cos_tab, sin_tab, rg, rows=rg.n,
              rope_kinds=(ROPE_A, ROPE_B, ROPE_KR), name="in_proj")
    n_tiles = N_HEADS * 2 * HEAD_DIM // ROPE_TN
    ident = jnp.arange(n_tiles // PROJ_HALVES, dtype=jnp.int32)
    q_scale = jnp.full((n_tiles,), (C_NOPE + C_ROPE) ** -0.5 * LOG2E, F32)
    qc = _proj(h, p["c_w_uq"], layer, jnp.full((n_tiles,), ROPE_QR, jnp.int32), q_scale, ident, cos_tab, sin_tab,
               rg, rows=rows, rope_kinds=(ROPE_QR,), a_col0=H_CQ, gain=p["c_q_norm_g"], name="mla_q_up")
    kv2 = _proj(h, p["c_w_ukv"], layer, jnp.zeros((n_tiles,), jnp.int32), jnp.ones((n_tiles,), F32), ident,
                cos_tab, sin_tab, rg, rows=rg.n, rope_kinds=(), a_col0=H_CKV, gain=p["c_kv_norm_g"],
                name="mla_kv_up")
    zc, zs = _fourier_chan(h, dft["cos_c"], dft["sin_c"], p["d_w"], layer, rg, rows=rows)

    ya = _attn_win(h, p["a_sink"], layer, rg, latent=True, out_rows=rows)
    yb = _attn_diff(h, p["b_lambda"], p["b_subln_g"], layer, rg, latent=True, lam_init=lam_init, out_rows=rows)
    yc = _attn_mla(qc, kv2, h, rg, latent=True, out_rows=rows)
    yd = _fourier_seq(dft["cos_t"], dft["sin_t"], zc, zs, rg, latent=True, out_rows=rows)
    if need_ctx:
        ya = _attn_win(h, p["a_sink"], layer, rg, latent=False, out_rows=rows, dst=ya)
        yb = _attn_diff(h, p["b_lambda"], p["b_subln_g"], layer, rg, latent=False, lam_init=lam_init,
                        out_rows=rows, dst=yb)
        yc = _attn_mla(qc, kv2, h, rg, latent=False, out_rows=rows, dst=yc)
        yd = _fourier_seq(dft["cos_x"], dft["sin_x"], zc, zs, rg, latent=False, out_rows=rows, dst=yd)
    return _out_proj([ya, yb, yc, yd], p["w_out"], layer, rows=rows, tm=rg.tm, tn=512)


def kernel(x, c, ctx, c_ctx, w_mod, b_mod, ffn1_w_gu, ffn1_w_d, ffn2_w_gu, ffn2_w_d, ln_g, ln_b, w_in, w_out,
           a_sink, b_lambda, b_subln_g, c_q_norm_g, c_kv_norm_g, c_w_uq, c_w_ukv, d_w):
    n_batch, seq, d = x.shape
    ctx_len = ctx.shape[1]
    depth = w_mod.shape[0]
    rg = _Rows(n_batch, seq, ctx_len)
    alpha = (2.0 * depth) ** 0.25

    cos_tab, sin_tab = _rope_tables(seq)
    cos_t, sin_t = _dft_tables(seq)
    cos_x, sin_x = _dft_tables(ctx_len)
    cos_c, sin_c = _dft_tables(HEAD_DIM)
    dft = {"cos_t": cos_t.astype(BF16), "sin_t": sin_t.astype(BF16),
           "cos_x": cos_x.astype(BF16), "sin_x": sin_x.astype(BF16),
           "cos_c": cos_c.astype(BF16), "sin_c": sin_c.astype(BF16)}
    consts = (cos_tab, sin_tab, _in_proj_tiles(), dft)

    n_c_rows = 8
    c_rows = jnp.concatenate([c, c_ctx[None, :], jnp.zeros((n_c_rows - n_batch - 1, d), c.dtype)], axis=0)

    p = {"w_in": _prep_w_in(w_in), "w_out": w_out, "a_sink": a_sink, "b_lambda": b_lambda,
         "b_subln_g": b_subln_g, "c_q_norm_g": c_q_norm_g, "c_kv_norm_g": c_kv_norm_g,
         "c_w_uq": _prep_w_uq(c_w_uq), "c_w_ukv": c_w_ukv, "d_w": d_w}
    mod_all = _mod_vectors(c_rows, w_mod, b_mod).reshape(depth, n_c_rows, N_MOD, d)
    mods = [mod_all[l] for l in range(depth)]
    xs, u = _modulate(x.reshape(n_batch * seq, d), ctx.reshape(n_batch * ctx_len, d), mods[0], rg, shift_idx=0)
    pending = None
    for l in range(depth):
        last = l == depth - 1
        lam_init = 0.8 - 0.6 * math.exp(-0.3 * l)
        mod = mods[l]
        if pending is None:
            y = _ffn(u, ffn1_w_gu, ffn1_w_d, l, rows=rg.n, tm=rg.tm)
        else:
            y, xs = _ffn_after_ln(pending, ffn1_w_gu, ffn1_w_d, l, rg, rows=rg.n)
        xs, u = _ln_res(xs, y, mod, ln_g[l, 0], ln_b[l, 0], rg, rows=rg.n, alpha=alpha,
                        gate_idx=2, gate_mul=0.5, next_mod=mod, next_shift_idx=3, name="ln_ffn1")
        y = _mixing(u, p, l, consts, rg, lam_init=lam_init, need_ctx=not last)
        rows = rg.nl if last else rg.n
        ln_mix = {"x": xs, "y": y, "mod": mod, "next_mod": mod, "g": ln_g[l, 1], "b": ln_b[l, 1], "alpha": alpha,
                  "gate_idx": 5, "gate_mul": 1.0, "shift_idx": 6}
        y, xs = _ffn_after_ln(ln_mix, ffn2_w_gu, ffn2_w_d, l, rg, rows=rows)
        if last:
            xs, _ = _ln_res(xs, y, mod, ln_g[l, 2], ln_b[l, 2], rg, rows=rows, alpha=alpha,
                            gate_idx=8, gate_mul=0.5, next_mod=None, next_shift_idx=None, name="ln_ffn2")
        else:
            pending = {"x": xs, "y": y, "mod": mod, "next_mod": mods[l + 1], "g": ln_g[l, 2], "b": ln_b[l, 2],
                       "alpha": alpha, "gate_idx": 8, "gate_mul": 0.5, "shift_idx": 0}
    return xs[:rg.nl].reshape(n_batch, seq, d)
```

```python
import functools
import math

import numpy as np
import jax
import jax.numpy as jnp
from jax import lax
from jax.experimental import pallas as pl
from jax.experimental.pallas import tpu as pltpu

F32 = jnp.float32
BF16 = jnp.bfloat16

GRID_W = 64
HEAD_DIM = 128
WINDOW = 128
N_HEADS = 8
A_KV_HEADS = 2
A_GROUP = N_HEADS // A_KV_HEADS
B_QK_DIM = 64
C_Q_LORA = 1536
C_KV_LORA = 512
C_NOPE = 128
C_ROPE = 64
N_MOD = 9
ROPE_BASE = 10000.0
LN_EPS = 1e-5
RMS_EPS = 1e-6
NEG_INF = -1e30

H_CQ, H_AQ, H_BQ, H_DX = 0, 1536, 2560, 3584
H_AK, H_AV, H_BK, H_BV, H_CKV, H_CKR = 4608, 4864, 5120, 6144, 7168, 7680
H_COLS = 8192

ROPE_NONE, ROPE_A, ROPE_B, ROPE_KR, ROPE_QR = 0, 1, 2, 3, 4
ROPE_HALF_LANES = {ROPE_A: HEAD_DIM // 4, ROPE_B: B_QK_DIM // 4, ROPE_KR: C_ROPE // 4, ROPE_QR: C_ROPE // 4}

VMEM_LIMIT_BYTES = 56 * 1024 * 1024
ROPE_TN = 256
PROJ_HALVES = 2
Q_SUB = 256
Q_SUB_DIFF = 512
LOG2E = math.log2(math.e)


def _cparams(n_axes):
    return pltpu.CompilerParams(dimension_semantics=("arbitrary",) * n_axes,
                                vmem_limit_bytes=VMEM_LIMIT_BYTES)


def _dot(a, b):
    return jnp.dot(a, b, preferred_element_type=F32)


def _dot_nt(a, b):
    return lax.dot_general(a, b, (((1,), (1,)), ((), ())), preferred_element_type=F32)


def _mm_kernel(a_ref, b_ref, o_ref, acc_ref, *, nk):
    k, j = pl.program_id(1), pl.program_id(2)

    def part():
        return _dot(a_ref[...], b_ref[...].astype(BF16))

    @pl.when(k == 0)
    def _():
        acc_ref[j] = part()

    if nk > 2:
        @pl.when((k > 0) & (k < nk - 1))
        def _():
            acc_ref[j] += part()

    @pl.when(k == nk - 1)
    def _():
        o_ref[...] = (acc_ref[j] + part()).astype(o_ref.dtype)


def _mm(a, b, layer, *, rows, tm, tn, tk, name):
    _, kdim, n = b.shape
    assert rows % tm == 0 and n % tn == 0 and kdim % tk == 0 and kdim // tk >= 2
    nk = kdim // tk
    return pl.pallas_call(
        functools.partial(_mm_kernel, nk=nk),
        out_shape=jax.ShapeDtypeStruct((rows, n), BF16),
        grid=(rows // tm, nk, n // tn),
        in_specs=[pl.BlockSpec((tm, tk), lambda i, k, j: (i, k)),
                  pl.BlockSpec((None, tk, tn), lambda i, k, j: (layer, k, j))],
        out_specs=pl.BlockSpec((tm, tn), lambda i, k, j: (i, jnp.where(k == nk - 1, j, 0))),
        scratch_shapes=[pltpu.VMEM((n // tn, tm, tn), F32)],
        compiler_params=_cparams(3),
        name=name,
    )(a, b)


def _out_proj_kernel(*refs):
    *y_refs, w_ref, o_ref = refs
    width = y_refs[0].shape[1]
    acc = None
    for m, y_ref in enumerate(y_refs):
        part = _dot(y_ref[...], w_ref[m * width:(m + 1) * width, :].astype(BF16))
        acc = part if acc is None else acc + part
    o_ref[...] = acc.astype(o_ref.dtype)


def _out_proj(ys, w_out, layer, *, rows, tm, tn):
    _, kdim, n = w_out.shape
    width = ys[0].shape[1]
    assert kdim == width * len(ys)
    y_spec = pl.BlockSpec((tm, width), lambda i, j: (i, 0))
    return pl.pallas_call(
        _out_proj_kernel,
        out_shape=jax.ShapeDtypeStruct((rows, n), BF16),
        grid=(rows // tm, n // tn),
        in_specs=[y_spec] * len(ys) + [pl.BlockSpec((None, kdim, tn), lambda i, j: (layer, 0, j))],
        out_specs=pl.BlockSpec((tm, tn), lambda i, j: (i, j)),
        compiler_params=_cparams(2),
        name="out_proj",
    )(*ys, w_out)


def _mod_kernel(c_ref, w_ref, b_ref, o_ref):
    c = c_ref[...]
    sc = (c * jax.nn.sigmoid(c)).astype(BF16)
    o_ref[...] = _dot(sc, w_ref[...].astype(BF16)) + b_ref[...]


def _mod_vectors(c_rows, w_mod, b_mod):
    depth, d, n = w_mod.shape
    r = c_rows.shape[0]
    tn = 1024
    return pl.pallas_call(
        _mod_kernel,
        out_shape=jax.ShapeDtypeStruct((depth, r, n), F32),
        grid=(depth, n // tn),
        in_specs=[pl.BlockSpec((r, d), lambda l, j: (0, 0)),
                  pl.BlockSpec((None, d, tn), lambda l, j: (l, 0, j)),
                  pl.BlockSpec((None, 1, tn), lambda l, j: (l, 0, j))],
        out_specs=pl.BlockSpec((None, r, tn), lambda l, j: (l, 0, j)),
        compiler_params=_cparams(2),
        name="mod_vectors",
    )(c_rows, w_mod, b_mod.reshape(depth, 1, n))


def _seg_index(i, n_lat_tiles, tiles_per_batch, n_batch):
    return jnp.where(i < n_lat_tiles, i // tiles_per_batch, n_batch)


def _modulate_kernel(x_ref, c_ref, mod_ref, xs_ref, u_ref, *, shift_idx, n_lat_tiles):
    shift = mod_ref[shift_idx:shift_idx + 1, :]
    scale = mod_ref[shift_idx + 1:shift_idx + 2, :]

    def emit(src_ref):
        x = src_ref[...]
        xs_ref[...] = x
        u_ref[...] = (x * (1.0 + scale) + shift).astype(u_ref.dtype)

    @pl.when(pl.program_id(0) < n_lat_tiles)
    def _():
        emit(x_ref)

    @pl.when(pl.program_id(0) >= n_lat_tiles)
    def _():
        emit(c_ref)


def _res_ln(x, y, mod_ref, g_ref, b_ref, *, alpha, gate_idx, gate_mul):
    gate = mod_ref[gate_idx:gate_idx + 1, :]
    z = alpha * x + (gate_mul * gate) * y.astype(F32)
    mu = jnp.mean(z, axis=-1, keepdims=True)
    zc = z - mu
    var = jnp.mean(zc * zc, axis=-1, keepdims=True)
    return zc * lax.rsqrt(var + LN_EPS) * g_ref[...] + b_ref[...]


def _modulated(xn, nmod_ref, shift_idx):
    shift = nmod_ref[shift_idx:shift_idx + 1, :]
    scale = nmod_ref[shift_idx + 1:shift_idx + 2, :]
    return (xn * (1.0 + scale) + shift).astype(BF16)


def _ln_res_kernel(x_ref, y_ref, mod_ref, g_ref, b_ref, *rest, alpha, gate_idx, gate_mul, next_shift_idx):
    xn = _res_ln(x_ref[...], y_ref[...], mod_ref, g_ref, b_ref, alpha=alpha, gate_idx=gate_idx, gate_mul=gate_mul)
    if next_shift_idx is None:
        xo_ref, = rest
        xo_ref[...] = xn
    else:
        nmod_ref, xo_ref, u_ref = rest
        xo_ref[...] = xn
        u_ref[...] = _modulated(xn, nmod_ref, next_shift_idx)


class _Rows:
    def __init__(self, n_batch, seq, ctx_len):
        self.b, self.t, self.c = n_batch, seq, ctx_len
        self.nl, self.nc = n_batch * seq, n_batch * ctx_len
        self.n = self.nl + self.nc
        self.tr = math.gcd(256, math.gcd(seq, ctx_len))
        self.tm = math.gcd(1024, math.gcd(seq, self.nc))

    def mod_spec(self, tile, d):
        n_lat_tiles, per_batch, nb = self.nl // tile, self.t // tile, self.b
        return pl.BlockSpec((None, N_MOD, d),
                            lambda i, *_: (_seg_index(i, n_lat_tiles, per_batch, nb), 0, 0))


def _modulate(x, ctx, mod, rg, *, shift_idx):
    d = x.shape[1]
    tr = rg.tr
    n_lat_tiles, n_ctx_tiles = rg.nl // tr, rg.nc // tr
    row_spec = pl.BlockSpec((tr, d), lambda i: (i, 0))
    x_spec = pl.BlockSpec((tr, d), lambda i: (jnp.minimum(i, n_lat_tiles - 1), 0))
    c_spec = pl.BlockSpec((tr, d), lambda i: (jnp.clip(i - n_lat_tiles, 0, n_ctx_tiles - 1), 0))
    return pl.pallas_call(
        functools.partial(_modulate_kernel, shift_idx=shift_idx, n_lat_tiles=n_lat_tiles),
        out_shape=[jax.ShapeDtypeStruct((rg.n, d), F32), jax.ShapeDtypeStruct((rg.n, d), BF16)],
        grid=(rg.n // tr,),
        in_specs=[x_spec, c_spec, rg.mod_spec(tr, d)],
        out_specs=[row_spec, row_spec],
        compiler_params=_cparams(1),
        name="modulate",
    )(x, ctx, mod)


def _ln_res(x, y, mod, ln_g, ln_b, rg, *, rows, alpha, gate_idx, gate_mul, next_mod, next_shift_idx, name,
            out_rows=None):
    d = x.shape[1]
    tr = rg.tr
    row_spec = pl.BlockSpec((tr, d), lambda i: (i, 0))
    vec_spec = pl.BlockSpec((1, d), lambda i: (0, 0))
    out_shape = [jax.ShapeDtypeStruct((out_rows or rows, d), F32)]
    out_specs = [row_spec]
    in_specs = [row_spec, row_spec, rg.mod_spec(tr, d), vec_spec, vec_spec]
    args = [x, y, mod, ln_g.reshape(1, d), ln_b.reshape(1, d)]
    if next_shift_idx is not None:
        in_specs.append(rg.mod_spec(tr, d))
        args.append(next_mod)
        out_shape.append(jax.ShapeDtypeStruct((rows, d), BF16))
        out_specs.append(row_spec)
    res = pl.pallas_call(
        functools.partial(_ln_res_kernel, alpha=alpha, gate_idx=gate_idx, gate_mul=gate_mul,
                          next_shift_idx=next_shift_idx),
        out_shape=out_shape,
        grid=(rows // tr,),
        in_specs=in_specs,
        out_specs=out_specs,
        compiler_params=_cparams(1),
        name=name,
    )(*args)
    return (res[0], res[1]) if next_shift_idx is not None else (res[0], None)


def _ffn_up_kernel(a_ref, wg_ref, wu_ref, o_ref):
    a = a_ref[...]
    g = _dot(a, wg_ref[...].astype(BF16))
    up = _dot(a, wu_ref[...].astype(BF16))
    o_ref[...] = (g * jax.nn.sigmoid(g) * up).astype(o_ref.dtype)


def _ffn_up(u, w_gu, layer, *, rows, tm, tn):
    _, d, two_ff = w_gu.shape
    ff = two_ff // 2
    up0 = ff // tn
    return pl.pallas_call(
        _ffn_up_kernel,
        out_shape=jax.ShapeDtypeStruct((rows, ff), BF16),
        grid=(rows // tm, ff // tn),
        in_specs=[pl.BlockSpec((tm, d), lambda i, j: (i, 0)),
                  pl.BlockSpec((None, d, tn), lambda i, j: (layer, 0, j)),
                  pl.BlockSpec((None, d, tn), lambda i, j: (layer, 0, up0 + j))],
        out_specs=pl.BlockSpec((tm, tn), lambda i, j: (i, j)),
        compiler_params=_cparams(2),
        name="ffn_up",
    )(u, w_gu, w_gu)


def _ffn_up_ln_kernel(x_ref, y_ref, mod_ref, nmod_ref, g_ref, b_ref, u0_ref, _xs_head, wg_ref, wu_ref,
                      hid_ref, xo_ref, a_even, a_odd, *, alpha, gate_idx, gate_mul, shift_idx, chunk):
    i, j = pl.program_id(0), pl.program_id(1)

    @pl.when((i == 0) & (j == 0))
    def _():
        a_even[...] = u0_ref[...]

    def step(a_cur, a_next):
        xn = _res_ln(x_ref[...], y_ref[...], mod_ref, g_ref, b_ref,
                     alpha=alpha, gate_idx=gate_idx, gate_mul=gate_mul)
        xo_ref[...] = xn
        a_next[pl.ds(pl.multiple_of(j * chunk, chunk), chunk), :] = _modulated(xn, nmod_ref, shift_idx)
        a = a_cur[...]
        g = _dot(a, wg_ref[...].astype(BF16))
        up = _dot(a, wu_ref[...].astype(BF16))
        hid_ref[...] = (g * jax.nn.sigmoid(g) * up).astype(hid_ref.dtype)

    @pl.when(i % 2 == 0)
    def _():
        step(a_even, a_odd)

    @pl.when(i % 2 == 1)
    def _():
        step(a_odd, a_even)


def _ffn_up_ln(ln, w_gu, layer, rg, *, rows, tm, tn):
    _, d, two_ff = w_gu.shape
    ff = two_ff // 2
    n_i, n_j = rows // tm, ff // tn
    chunk = tm // n_j
    assert rows % tm == 0 and tm % n_j == 0 and chunk % 16 == 0 and rg.t % chunk == 0 and rg.nl % tm == 0
    xs_head, u0 = _ln_res(ln["x"], ln["y"], ln["mod"], ln["g"], ln["b"], rg, rows=tm, out_rows=rows,
                          alpha=ln["alpha"], gate_idx=ln["gate_idx"], gate_mul=ln["gate_mul"],
                          next_mod=ln["next_mod"], next_shift_idx=ln["shift_idx"], name="ln_head")

    def chunk_block(i, j):
        return jnp.where(i < n_i - 1, (i + 1) * n_j + j, n_i * n_j - 1)

    def seg(i, j):
        row0 = chunk_block(i, j) * chunk
        return jnp.where(row0 < rg.nl, row0 // rg.t, rg.b)

    chunk_spec = pl.BlockSpec((chunk, d), lambda i, j: (chunk_block(i, j), 0))
    mod_spec = pl.BlockSpec((None, N_MOD, d), lambda i, j: (seg(i, j), 0, 0))
    vec_spec = pl.BlockSpec((1, d), lambda i, j: (0, 0))
    up0 = ff // tn
    hid, xs_new = pl.pallas_call(
        functools.partial(_ffn_up_ln_kernel, alpha=ln["alpha"], gate_idx=ln["gate_idx"], gate_mul=ln["gate_mul"],
                          shift_idx=ln["shift_idx"], chunk=chunk),
        out_shape=[jax.ShapeDtypeStruct((rows, ff), BF16), jax.ShapeDtypeStruct((rows, d), F32)],
        grid=(n_i, n_j),
        in_specs=[chunk_spec, chunk_spec, mod_spec, mod_spec, vec_spec, vec_spec,
                  pl.BlockSpec((tm, d), lambda i, j: (0, 0), pipeline_mode=pl.Buffered(1)),
                  pl.BlockSpec(memory_space=pl.ANY),
                  pl.BlockSpec((None, d, tn), lambda i, j: (layer, 0, j)),
                  pl.BlockSpec((None, d, tn), lambda i, j: (layer, 0, up0 + j))],
        out_specs=[pl.BlockSpec((tm, tn), lambda i, j: (i, j)), chunk_spec],
        scratch_shapes=[pltpu.VMEM((tm, d), BF16), pltpu.VMEM((tm, d), BF16)],
        input_output_aliases={7: 1},
        compiler_params=_cparams(2),
        name="ffn_up_ln",
    )(ln["x"], ln["y"], ln["mod"], ln["next_mod"], ln["g"].reshape(1, d), ln["b"].reshape(1, d), u0, xs_head,
      w_gu, w_gu)
    return hid, xs_new


def _rope_tile(x, cos, sin, half):
    lane = lax.broadcasted_iota(jnp.int32, x.shape, 1)
    first = (lane % (2 * half)) < half
    width = x.shape[1]
    partner = jnp.where(first, pltpu.roll(x, width - half, 1), pltpu.roll(x, half, 1))
    return x * cos + partner * sin


def _proj_kernel(kind_ref, src_ref, scale_ref, a_ref, b_ref, *rest, has_gain, n_lat_tiles, rope_kinds):
    if has_gain:
        gain_ref, *tab_refs, o_ref, an_ref = rest
    else:
        *tab_refs, o_ref = rest
    tables = {kind: (tab_refs[2 * n], tab_refs[2 * n + 1]) for n, kind in enumerate(rope_kinds)}
    i, j = pl.program_id(0), pl.program_id(1)

    if has_gain:
        @pl.when(j == 0)
        def _():
            x = a_ref[...].astype(F32)
            ms = jnp.mean(x * x, axis=-1, keepdims=True)
            an_ref[...] = (x * lax.rsqrt(ms + RMS_EPS) * gain_ref[...]).astype(BF16)
    lhs_ref = an_ref if has_gain else a_ref

    for half in range(PROJ_HALVES):
        cols = slice(half * ROPE_TN, (half + 1) * ROPE_TN)
        tile = PROJ_HALVES * j + half
        kind = jnp.where(i < n_lat_tiles, kind_ref[tile], ROPE_NONE)
        out_scale = scale_ref[tile]

        def acc():
            return _dot(lhs_ref[...], b_ref[:, cols].astype(BF16)) * out_scale

        @pl.when(kind == ROPE_NONE)
        def _():
            o_ref[:, cols] = acc().astype(o_ref.dtype)

        for rope_kind, (cos_ref, sin_ref) in tables.items():
            @pl.when(kind == rope_kind)
            def _(cos_ref=cos_ref, sin_ref=sin_ref, rope_kind=rope_kind):
                roped = _rope_tile(acc(), cos_ref[...], sin_ref[...], ROPE_HALF_LANES[rope_kind])
                o_ref[:, cols] = roped.astype(o_ref.dtype)


def _proj(a, w, layer, kinds, scales, src_tiles, cos_tab, sin_tab, rg, *, rows, rope_kinds, a_col0=0, gain=None,
          name):
    _, kdim, n = w.shape
    tm, tn = rg.tm, PROJ_HALVES * ROPE_TN
    assert a_col0 % kdim == 0 and n % tn == 0
    a_blk = a_col0 // kdim
    n_lat_tiles = rg.nl // tm
    t_tiles = rg.t // tm
    has_gain = gain is not None

    def tab_spec(kind):
        return pl.BlockSpec((None, tm, ROPE_TN), lambda i, j, kr, sr: (kind, i % t_tiles, 0))

    in_specs = [pl.BlockSpec(memory_space=pltpu.SMEM),
                pl.BlockSpec((tm, kdim), lambda i, j, kr, sr: (i, a_blk)),
                pl.BlockSpec((None, kdim, tn), lambda i, j, kr, sr: (layer, 0, sr[j]))]
    args = [scales, a, w]
    scratch = []
    if has_gain:
        in_specs.append(pl.BlockSpec((None, 1, kdim), lambda i, j, kr, sr: (layer, 0, 0)))
        args.append(gain.reshape(gain.shape[0], 1, kdim))
        scratch.append(pltpu.VMEM((tm, kdim), BF16))
    for kind in rope_kinds:
        in_specs += [tab_spec(kind), tab_spec(kind)]
        args += [cos_tab, sin_tab]
    return pl.pallas_call(
        functools.partial(_proj_kernel, has_gain=has_gain, n_lat_tiles=n_lat_tiles, rope_kinds=tuple(rope_kinds)),
        out_shape=jax.ShapeDtypeStruct((rows, n), BF16),
        grid_spec=pltpu.PrefetchScalarGridSpec(
            num_scalar_prefetch=2,
            grid=(rows // tm, n // tn),
            in_specs=in_specs,
            out_specs=pl.BlockSpec((tm, tn), lambda i, j, kr, sr: (i, j)),
            scratch_shapes=scratch),
        compiler_params=_cparams(2),
        name=name,
    )(kinds, src_tiles, *args)


def _rope_tables(seq):
    t = np.arange(seq)
    row, col = (t // GRID_W).astype(np.float64), (t % GRID_W).astype(np.float64)

    def pattern(rot_dim):
        axis_dim = rot_dim // 2
        inv = ROPE_BASE ** (-np.arange(0, axis_dim, 2, dtype=np.float64) / axis_dim)
        ar, ac = row[:, None] * inv[None, :], col[:, None] * inv[None, :]
        cos = np.concatenate([np.cos(ar), np.cos(ar), np.cos(ac), np.cos(ac)], axis=1)
        sin = np.concatenate([-np.sin(ar), np.sin(ar), -np.sin(ac), np.sin(ac)], axis=1)
        return cos, sin

    cos = np.ones((5, seq, ROPE_TN), np.float64)
    sin = np.zeros((5, seq, ROPE_TN), np.float64)
    c128, s128 = pattern(HEAD_DIM)
    c64, s64 = pattern(2 * 32)
    cos[ROPE_A], sin[ROPE_A] = np.tile(c128, (1, 2)), np.tile(s128, (1, 2))
    cos[ROPE_B], sin[ROPE_B] = np.tile(c64, (1, 4)), np.tile(s64, (1, 4))
    cos[ROPE_KR, :, :64], sin[ROPE_KR, :, :64] = c64, s64
    cos[ROPE_QR, :, 128:192], sin[ROPE_QR, :, 128:192] = c64, s64
    return jnp.asarray(cos, F32), jnp.asarray(sin, F32)


def _mixer_call(body, rg, *, latent, dst, out_rows, grid, in_specs, args, tq, n_q, out_block_w, name):
    row0 = 0 if latent else rg.nl // tq
    aliases = {}
    if not latent:
        in_specs = in_specs + [pl.BlockSpec(memory_space=pl.ANY)]
        args = args + [dst]
        aliases = {len(args) - 1: 0}
    return pl.pallas_call(
        body,
        out_shape=jax.ShapeDtypeStruct((out_rows, N_HEADS * HEAD_DIM), BF16),
        grid=grid,
        in_specs=in_specs,
        out_specs=pl.BlockSpec((tq, out_block_w), lambda b, hh, qi: (row0 + b * n_q + qi, hh)),
        input_output_aliases=aliases,
        compiler_params=_cparams(3),
        name=name,
    )(*args)


def _attn_win_kernel(sink_ref, q_ref, kc_ref, vc_ref, *rest, has_lat, tq, seq, layer):
    if has_lat:
        kl_ref, vl_ref, o_ref = rest
    else:
        _, o_ref = rest
    g, qi = pl.program_id(1), pl.program_id(2)
    kc, vc = kc_ref[...], vc_ref[...]
    sub = min(tq, Q_SUB)
    for r in range(tq // sub):
        rows = slice(r * sub, (r + 1) * sub)
        row0 = qi * tq + r * sub
        if has_lat:
            win = min(seq, sub + 2 * WINDOW)
            ws = pl.multiple_of(jnp.clip(row0 - WINDOW, 0, seq - win), WINDOW)
            kw, vw = kl_ref[pl.ds(ws, win), :], vl_ref[pl.ds(ws, win), :]
            qpos = row0 + lax.broadcasted_iota(jnp.int32, (sub, win), 0)
            kpos = ws + lax.broadcasted_iota(jnp.int32, (sub, win), 1)
            valid = jnp.abs(qpos - kpos) <= WINDOW
        for j in range(A_GROUP):
            cols = slice(j * HEAD_DIM, (j + 1) * HEAD_DIM)
            q = q_ref[rows, cols]
            sink = sink_ref[layer, g * A_GROUP + j] * LOG2E
            s_c = _dot_nt(q, kc)
            m = jnp.maximum(jnp.max(s_c, axis=-1, keepdims=True), sink)
            if has_lat:
                s_l = jnp.where(valid, _dot_nt(q, kw), NEG_INF)
                m = jnp.maximum(m, jnp.max(s_l, axis=-1, keepdims=True))
            p_c = jnp.exp2(s_c - m)
            den = jnp.sum(p_c, axis=-1, keepdims=True) + jnp.exp2(sink - m)
            o = _dot(p_c.astype(BF16), vc)
            if has_lat:
                p_l = jnp.exp2(s_l - m)
                den = den + jnp.sum(p_l, axis=-1, keepdims=True)
                o = o + _dot(p_l.astype(BF16), vw)
            o_ref[rows, cols] = (o / den).astype(o_ref.dtype)


def _attn_win(h, a_sink, layer, rg, *, latent, out_rows, dst=None):
    gw = A_GROUP * HEAD_DIM
    tq, n_q, q0 = _q_geometry(rg, latent, 1024)
    c_blk0 = rg.nl // rg.c
    in_specs = [pl.BlockSpec(memory_space=pltpu.SMEM),
                pl.BlockSpec((tq, gw), lambda b, g, qi: (q0 + b * n_q + qi, H_AQ // gw + g)),
                pl.BlockSpec((rg.c, HEAD_DIM), lambda b, g, qi: (c_blk0 + b, H_AK // HEAD_DIM + g)),
                pl.BlockSpec((rg.c, HEAD_DIM), lambda b, g, qi: (c_blk0 + b, H_AV // HEAD_DIM + g))]
    args = [a_sink, h, h, h]
    if latent:
        in_specs += [pl.BlockSpec((rg.t, HEAD_DIM), lambda b, g, qi: (b, H_AK // HEAD_DIM + g)),
                     pl.BlockSpec((rg.t, HEAD_DIM), lambda b, g, qi: (b, H_AV // HEAD_DIM + g))]
        args += [h, h]
    return _mixer_call(functools.partial(_attn_win_kernel, has_lat=latent, tq=tq, seq=rg.t, layer=layer), rg,
                       latent=latent, dst=dst, out_rows=out_rows, grid=(rg.b, A_KV_HEADS, n_q), in_specs=in_specs, args=args,
                       tq=tq, n_q=n_q, out_block_w=gw, name="attn_win_lat" if latent else "attn_win_ctx")


def _softmax_pv(q, keys, values):
    scores = [_dot_nt(q, k) for k in keys]
    m = functools.reduce(jnp.maximum, [jnp.max(s, axis=-1, keepdims=True) for s in scores])
    den, out = None, None
    for s, v in zip(scores, values):
        p = jnp.exp2(s - m)
        d = jnp.sum(p, axis=-1, keepdims=True)
        o = _dot(p.astype(BF16), v)
        den = d if den is None else den + d
        out = o if out is None else out + o
    return out / den


def _attn_diff_kernel(lam_ref, gsub_ref, q_ref, kc_ref, vc_ref, *rest, has_lat, lam_init):
    if has_lat:
        kl_ref, vl_ref, o_ref = rest
    else:
        _, o_ref = rest
    lf = lam_ref[...]
    lam = (jnp.exp(jnp.sum(lf[0:1] * lf[1:2], axis=-1, keepdims=True))
           - jnp.exp(jnp.sum(lf[2:3] * lf[3:4], axis=-1, keepdims=True)) + lam_init)
    keys = [kc_ref[...]] + ([kl_ref[...]] if has_lat else [])
    values = [vc_ref[...]] + ([vl_ref[...]] if has_lat else [])
    tq = q_ref.shape[0]
    sub = min(tq, Q_SUB_DIFF)
    for r in range(tq // sub):
        q = q_ref[r * sub:(r + 1) * sub, :]
        first_map = lax.broadcasted_iota(jnp.int32, q.shape, 1) < B_QK_DIM
        zero = jnp.zeros_like(q)

        o = (_softmax_pv(jnp.where(first_map, q, zero), keys, values)
             - lam * _softmax_pv(jnp.where(first_map, zero, q), keys, values))
        ms = jnp.mean(o * o, axis=-1, keepdims=True)
        o = o * lax.rsqrt(ms + RMS_EPS) * gsub_ref[...] * (1.0 - lam_init)
        o_ref[r * sub:(r + 1) * sub, :] = o.astype(o_ref.dtype)


def _kv_specs(rg, col_k, col_v, latent, k_width=HEAD_DIM):
    c_blk0 = rg.nl // rg.c
    specs = [pl.BlockSpec((rg.c, k_width), lambda b, hh, qi: (c_blk0 + b, col_k(hh))),
             pl.BlockSpec((rg.c, HEAD_DIM), lambda b, hh, qi: (c_blk0 + b, col_v(hh)))]
    if latent:
        specs += [pl.BlockSpec((rg.t, k_width), lambda b, hh, qi: (b, col_k(hh))),
                  pl.BlockSpec((rg.t, HEAD_DIM), lambda b, hh, qi: (b, col_v(hh)))]
    return specs


def _q_geometry(rg, latent, tq_lat):
    if latent:
        tq = min(tq_lat, rg.t)
        return tq, rg.t // tq, 0
    return rg.c, 1, rg.nl // rg.c


def _attn_diff(h, b_lambda, b_subln_g, layer, rg, *, latent, lam_init, out_rows, dst=None):
    tq, n_q, q0 = _q_geometry(rg, latent, 2048)
    in_specs = [pl.BlockSpec((None, 4, B_QK_DIM), lambda b, hh, qi: (layer, 0, 0)),
                pl.BlockSpec((None, 1, HEAD_DIM), lambda b, hh, qi: (layer, 0, 0)),
                pl.BlockSpec((tq, HEAD_DIM), lambda b, hh, qi: (q0 + b * n_q + qi, H_BQ // HEAD_DIM + hh))]
    in_specs += _kv_specs(rg, lambda hh: H_BK // HEAD_DIM + hh, lambda hh: H_BV // HEAD_DIM + hh, latent)
    n_kv = 4 if latent else 2
    args = [b_lambda, b_subln_g.reshape(b_subln_g.shape[0], 1, HEAD_DIM), h] + [h] * n_kv
    return _mixer_call(functools.partial(_attn_diff_kernel, has_lat=latent, lam_init=lam_init), rg,
                       latent=latent, dst=dst, out_rows=out_rows, grid=(rg.b, N_HEADS, n_q), in_specs=in_specs, args=args,
                       tq=tq, n_q=n_q, out_block_w=HEAD_DIM, name="attn_diff_lat" if latent else "attn_diff_ctx")


def _attn_mla_kernel(q_ref, knc_ref, vc_ref, krc_ref, *rest, has_lat):
    if has_lat:
        knl_ref, vl_ref, krl_ref, o_ref = rest
    else:
        _, o_ref = rest
    segs = [(knc_ref, krc_ref, vc_ref)] + ([(knl_ref, krl_ref, vl_ref)] if has_lat else [])
    keys = [jnp.concatenate([kn[...], kr[...]], axis=1) for kn, kr, _ in segs]
    values = [v[...] for _, _, v in segs]
    tq = q_ref.shape[0]
    sub = min(tq, Q_SUB_DIFF)
    for r in range(tq // sub):
        rows = slice(r * sub, (r + 1) * sub)
        o_ref[rows, :] = _softmax_pv(q_ref[rows, :], keys, values).astype(o_ref.dtype)


def _attn_mla(qc, kv2, h, rg, *, latent, out_rows, dst=None):
    tq, n_q, q0 = _q_geometry(rg, latent, 2048)
    c_blk0 = rg.nl // rg.c
    kr_col = H_CKR // HEAD_DIM
    kv = _kv_specs(rg, lambda hh: 2 * hh, lambda hh: 2 * hh + 1, latent)
    in_specs = [pl.BlockSpec((tq, 2 * HEAD_DIM), lambda b, hh, qi: (q0 + b * n_q + qi, hh)),
                kv[0], kv[1],
                pl.BlockSpec((rg.c, HEAD_DIM), lambda b, hh, qi: (c_blk0 + b, kr_col))]
    args = [qc, kv2, kv2, h]
    if latent:
        in_specs += [kv[2], kv[3], pl.BlockSpec((rg.t, HEAD_DIM), lambda b, hh, qi: (b, kr_col))]
        args += [kv2, kv2, h]
    return _mixer_call(functools.partial(_attn_mla_kernel, has_lat=latent), rg,
                       latent=latent, dst=dst, out_rows=out_rows, grid=(rg.b, N_HEADS, n_q), in_specs=in_specs, args=args,
                       tq=tq, n_q=n_q, out_block_w=HEAD_DIM, name="attn_mla_lat" if latent else "attn_mla_ctx")


def _dft_tables(n):
    idx = np.arange(n, dtype=np.int64)
    ang = (np.outer(idx, idx) % n).astype(np.float64) * (2.0 * np.pi / n)
    return jnp.asarray(np.cos(ang), F32), jnp.asarray(np.sin(ang), F32)


def _fourier_chan_kernel(z_ref, cc_ref, sc_ref, dw_ref, zc_ref, zs_ref):
    for g in range(dw_ref.shape[0]):
        cols = slice(g * HEAD_DIM, (g + 1) * HEAD_DIM)
        dw = dw_ref[g].astype(BF16)
        z = z_ref[:, cols]
        zc_ref[:, cols] = _dot(z, _dot(cc_ref[...], dw).astype(BF16)).astype(zc_ref.dtype)
        zs_ref[:, cols] = _dot(z, _dot(sc_ref[...], dw).astype(BF16)).astype(zs_ref.dtype)


def _fourier_chan(h, cos_c, sin_c, d_w, layer, rg, *, rows):
    tm = rg.tm
    groups = N_HEADS // 2
    width = groups * HEAD_DIM
    assert H_DX % width == 0
    tile = pl.BlockSpec((tm, width), lambda i, s: (i, s))
    const = pl.BlockSpec((HEAD_DIM, HEAD_DIM), lambda i, s: (0, 0))
    out = jax.ShapeDtypeStruct((rows, N_HEADS * HEAD_DIM), BF16)
    return pl.pallas_call(
        _fourier_chan_kernel,
        out_shape=[out, out],
        grid=(rows // tm, N_HEADS // groups),
        in_specs=[pl.BlockSpec((tm, width), lambda i, s: (i, H_DX // width + s)), const, const,
                  pl.BlockSpec((None, groups, HEAD_DIM, HEAD_DIM), lambda i, s: (layer, s, 0, 0))],
        out_specs=[tile, tile],
        compiler_params=_cparams(2),
        name="fourier_chan",
    )(h, cos_c, sin_c, d_w)


def _fourier_seq_kernel(ct_ref, st_ref, zc_ref, zs_ref, *rest, norm):
    o_ref = rest[-1]
    o = _dot(ct_ref[...], zc_ref[...]) - _dot(st_ref[...], zs_ref[...])
    o_ref[...] = (o * norm).astype(o_ref.dtype)


def _fourier_seq(cos_t, sin_t, zc, zs, rg, *, latent, out_rows, dst=None):
    length = rg.t if latent else rg.c
    tm = min(512, length)
    n_i = length // tm
    z_blk0 = 0 if latent else rg.nl // rg.c
    row0 = 0 if latent else rg.nl // tm
    width = zc.shape[1]
    dft_spec = pl.BlockSpec((tm, length), lambda b, i: (i, 0))
    z_spec = pl.BlockSpec((length, width), lambda b, i: (z_blk0 + b, 0))
    in_specs, args, aliases = [dft_spec, dft_spec, z_spec, z_spec], [cos_t, sin_t, zc, zs], {}
    if not latent:
        in_specs.append(pl.BlockSpec(memory_space=pl.ANY))
        args.append(dst)
        aliases = {len(args) - 1: 0}
    return pl.pallas_call(
        functools.partial(_fourier_seq_kernel, norm=(length * HEAD_DIM) ** -0.5),
        out_shape=jax.ShapeDtypeStruct((out_rows, width), BF16),
        grid=(rg.b, n_i),
        in_specs=in_specs,
        out_specs=pl.BlockSpec((tm, width), lambda b, i: (row0 + b * n_i + i, 0)),
        input_output_aliases=aliases,
        compiler_params=_cparams(2),
        name="fourier_seq_lat" if latent else "fourier_seq_ctx",
    )(*args)


def _prep_w_in(w_in):
    return jnp.pad(w_in, ((0, 0), (0, 0), (0, H_COLS - w_in.shape[2]))).astype(BF16)


def _prep_w_uq(w_uq):
    depth, k, _ = w_uq.shape
    w = w_uq.reshape(depth, k, N_HEADS, C_NOPE + C_ROPE)
    w = jnp.pad(w, ((0, 0), (0, 0), (0, 0), (0, 2 * HEAD_DIM - C_NOPE - C_ROPE)))
    return w.reshape(depth, k, N_HEADS * 2 * HEAD_DIM).astype(BF16)


def _in_proj_tiles():
    step = PROJ_HALVES * ROPE_TN
    src_cols = {"aq": 0, "bq": 1024, "cq": 2048, "dx": 3584, "ak": 4608, "bk": 5120, "bv": 6144,
                "ckv": 7168, "ckr": 7680}
    layout = [("cq", H_CQ, H_AQ, (ROPE_NONE, ROPE_NONE), 1.0),
              ("aq", H_AQ, H_BQ, (ROPE_A, ROPE_A), HEAD_DIM ** -0.5 * LOG2E),
              ("bq", H_BQ, H_DX, (ROPE_B, ROPE_B), B_QK_DIM ** -0.5 * LOG2E),
              ("dx", H_DX, H_AK, (ROPE_NONE, ROPE_NONE), 1.0),
              ("ak", H_AK, H_BK, (ROPE_A, ROPE_NONE), 1.0),
              ("bk", H_BK, H_BV, (ROPE_B, ROPE_B), 1.0),
              ("bv", H_BV, H_CKV, (ROPE_NONE, ROPE_NONE), 1.0),
              ("ckv", H_CKV, H_CKR, (ROPE_NONE, ROPE_NONE), 1.0),
              ("ckr", H_CKR, H_COLS, (ROPE_KR, ROPE_NONE), 1.0)]
    kinds = np.zeros((H_COLS // ROPE_TN,), np.int32)
    scales = np.ones((H_COLS // ROPE_TN,), np.float32)
    src = np.zeros((H_COLS // step,), np.int32)
    for name, lo, hi, kind, scale in layout:
        assert lo % step == 0 and hi % step == 0 and src_cols[name] % step == 0
        n = (hi - lo) // step
        kinds[lo // ROPE_TN:hi // ROPE_TN] = np.tile(np.asarray(kind, np.int32), n)
        scales[lo // ROPE_TN:hi // ROPE_TN] = scale
        src[lo // step:hi // step] = src_cols[name] // step + np.arange(n)
    return jnp.asarray(kinds), jnp.asarray(scales), jnp.asarray(src)


def _ffn(u, w_gu, w_d, layer, *, rows, tm):
    tm_up = next(t for t in (3 * tm // 2, tm) if rows % t == 0)
    hid = _ffn_up(u, w_gu, layer, rows=rows, tm=tm_up, tn=256)
    return _mm(hid, w_d, layer, rows=rows, tm=tm, tn=512, tk=w_d.shape[1] // 2, name="ffn_down")


def _ffn_after_ln(ln, w_gu, w_d, layer, rg, *, rows):
    hid, xs_new = _ffn_up_ln(ln, w_gu, layer, rg, rows=rows, tm=rg.tm, tn=256)
    y = _mm(hid, w_d, layer, rows=rows, tm=rg.tm, tn=512, tk=w_d.shape[1] // 2, name="ffn_down")
    return y, xs_new


def _mixing(u, p, layer, consts, rg, *, lam_init, need_ctx):
    cos_tab, sin_tab, (kinds_in, scales_in, src_in), dft = consts
    rows = rg.n if need_ctx else rg.nl
    h = _proj(u, p["w_in"], layer, kinds_in, scales_in, src_in, cos_tab, sin_tab, rg, rows=rg.n,
              rope_kinds=(ROPE_A, ROPE_B, ROPE_KR), name="in_proj")
    n_tiles = N_HEADS * 2 * HEAD_DIM // ROPE_TN
    ident = jnp.arange(n_tiles // PROJ_HALVES, dtype=jnp.int32)
    q_scale = jnp.full((n_tiles,), (C_NOPE + C_ROPE) ** -0.5 * LOG2E, F32)
    qc = _proj(h, p["c_w_uq"], layer, jnp.full((n_tiles,), ROPE_QR, jnp.int32), q_scale, ident, cos_tab, sin_tab,
               rg, rows=rows, rope_kinds=(ROPE_QR,), a_col0=H_CQ, gain=p["c_q_norm_g"], name="mla_q_up")
    kv2 = _proj(h, p["c_w_ukv"], layer, jnp.zeros((n_tiles,), jnp.int32), jnp.ones((n_tiles,), F32), ident,
                cos_tab, sin_tab, rg, rows=rg.n, rope_kinds=(), a_col0=H_CKV, gain=p["c_kv_norm_g"],
                name="mla_kv_up")
    zc, zs = _fourier_chan(h, dft["cos_c"], dft["sin_c"], p["d_w"], layer, rg, rows=rows)

    ya = _attn_win(h, p["a_sink"], layer, rg, latent=True, out_rows=rows)
    yb = _attn_diff(h, p["b_lambda"], p["b_subln_g"], layer, rg, latent=True, lam_init=lam_init, out_rows=rows)
    yc = _attn_mla(qc, kv2, h, rg, latent=True, out_rows=rows)
    yd = _fourier_seq(dft["cos_t"], dft["sin_t"], zc, zs, rg, latent=True, out_rows=rows)
    if need_ctx:
        ya = _attn_win(h, p["a_sink"], layer, rg, latent=False, out_rows=rows, dst=ya)
        yb = _attn_diff(h, p["b_lambda"], p["b_subln_g"], layer, rg, latent=False, lam_init=lam_init,
                        out_rows=rows, dst=yb)
        yc = _attn_mla(qc, kv2, h, rg, latent=False, out_rows=rows, dst=yc)
        yd = _fourier_seq(dft["cos_x"], dft["sin_x"], zc, zs, rg, latent=False, out_rows=rows, dst=yd)
    return _out_proj([ya, yb, yc, yd], p["w_out"], layer, rows=rows, tm=rg.tm, tn=512)


def kernel(x, c, ctx, c_ctx, w_mod, b_mod, ffn1_w_gu, ffn1_w_d, ffn2_w_gu, ffn2_w_d, ln_g, ln_b, w_in, w_out,
           a_sink, b_lambda, b_subln_g, c_q_norm_g, c_kv_norm_g, c_w_uq, c_w_ukv, d_w):
    n_batch, seq, d = x.shape
    ctx_len = ctx.shape[1]
    depth = w_mod.shape[0]
    rg = _Rows(n_batch, seq, ctx_len)
    alpha = (2.0 * depth) ** 0.25

    cos_tab, sin_tab = _rope_tables(seq)
    cos_t, sin_t = _dft_tables(seq)
    cos_x, sin_x = _dft_tables(ctx_len)
    cos_c, sin_c = _dft_tables(HEAD_DIM)
    dft = {"cos_t": cos_t.astype(BF16), "sin_t": sin_t.astype(BF16),
           "cos_x": cos_x.astype(BF16), "sin_x": sin_x.astype(BF16),
           "cos_c": cos_c.astype(BF16), "sin_c": sin_c.astype(BF16)}
    consts = (cos_tab, sin_tab, _in_proj_tiles(), dft)

    n_c_rows = 8
    c_rows = jnp.concatenate([c, c_ctx[None, :], jnp.zeros((n_c_rows - n_batch - 1, d), c.dtype)], axis=0)

    p = {"w_in": _prep_w_in(w_in), "w_out": w_out, "a_sink": a_sink, "b_lambda": b_lambda,
         "b_subln_g": b_subln_g, "c_q_norm_g": c_q_norm_g, "c_kv_norm_g": c_kv_norm_g,
         "c_w_uq": _prep_w_uq(c_w_uq), "c_w_ukv": c_w_ukv, "d_w": d_w}
    mod_all = _mod_vectors(c_rows, w_mod, b_mod).reshape(depth, n_c_rows, N_MOD, d)
    mods = [mod_all[l] for l in range(depth)]
    xs, u = _modulate(x.reshape(n_batch * seq, d), ctx.reshape(n_batch * ctx_len, d), mods[0], rg, shift_idx=0)
    pending = None
    for l in range(depth):
        last = l == depth - 1
        lam_init = 0.8 - 0.6 * math.exp(-0.3 * l)
        mod = mods[l]
        if pending is None:
            y = _ffn(u, ffn1_w_gu, ffn1_w_d, l, rows=rg.n, tm=rg.tm)
        else:
            y, xs = _ffn_after_ln(pending, ffn1_w_gu, ffn1_w_d, l, rg, rows=rg.n)
        xs, u = _ln_res(xs, y, mod, ln_g[l, 0], ln_b[l, 0], rg, rows=rg.n, alpha=alpha,
                        gate_idx=2, gate_mul=0.5, next_mod=mod, next_shift_idx=3, name="ln_ffn1")
        y = _mixing(u, p, l, consts, rg, lam_init=lam_init, need_ctx=not last)
        rows = rg.nl if last else rg.n
        ln_mix = {"x": xs, "y": y, "mod": mod, "next_mod": mod, "g": ln_g[l, 1], "b": ln_b[l, 1], "alpha": alpha,
                  "gate_idx": 5, "gate_mul": 1.0, "shift_idx": 6}
        y, xs = _ffn_after_ln(ln_mix, ffn2_w_gu, ffn2_w_d, l, rg, rows=rows)
        if last:
            xs, _ = _ln_res(xs, y, mod, ln_g[l, 2], ln_b[l, 2], rg, rows=rows, alpha=alpha,
                            gate_idx=8, gate_mul=0.5, next_mod=None, next_shift_idx=None, name="ln_ffn2")
        else:
            pending = {"x": xs, "y": y, "mod": mod, "next_mod": mods[l + 1], "g": ln_g[l, 2], "b": ln_b[l, 2],
                       "alpha": alpha, "gate_idx": 8, "gate_mul": 0.5, "shift_idx": 0}
    return xs[:rg.nl].reshape(n_batch, seq, d)
```
